```python
import math
import jax, jax.numpy as jnp
from jax import lax
import numpy as np

D_MODEL = 2048
BATCH = 4
SEQ = 2048
DEPTH = 1

GRID_W = 64
CTX_LEN = 256
N_HEADS = 16
QK_NOPE = 128
QK_ROPE = 64
V_DIM = 128
Q_LORA = 512
KV_LORA = 512
ROPE_THETA = 10000.0
N_FOURIER_GROUPS = 4
FOURIER_GROUP_DIM = 256
FOURIER_DIM = N_FOURIER_GROUPS * FOURIER_GROUP_DIM
KV_END = KV_LORA + QK_ROPE
IN_SPLITS = (KV_LORA, KV_END, KV_END + Q_LORA, KV_END + Q_LORA + FOURIER_DIM,
             KV_END + Q_LORA + FOURIER_DIM + D_MODEL)
IN_DIM = KV_END + Q_LORA + FOURIER_DIM + 2 * D_MODEL
PEER_HEADS = 8
N_KEYS = 128
N_EXPERTS = N_KEYS * N_KEYS
PEER_KEY_DIM = 256
PEER_HALF = PEER_KEY_DIM // 2
PEER_TOPK = 16
Q_BLOCK = 128
TOKEN_BLOCK = 128
DEEPNORM_ALPHA = (2.0 * DEPTH) ** 0.25
DEEPNORM_BETA = (8.0 * DEPTH) ** -0.25
EPS = 1e-6

kernel_name = 'hybrid_mla_fnet_peer_dit_block'


def _layer_norm(x, g=None, b=None):
    xf = x.astype(jnp.float32)
    mu = jnp.mean(xf, axis=-1, keepdims=True)
    var = jnp.mean(jnp.square(xf - mu), axis=-1, keepdims=True)
    y = (xf - mu) * lax.rsqrt(var + EPS)
    if g is not None:
        y = y * g + b
    return y.astype(x.dtype)


def _rms_norm(x, g):
    xf = x.astype(jnp.float32)
    y = xf * lax.rsqrt(jnp.mean(jnp.square(xf), axis=-1, keepdims=True) + EPS)
    return (y * g).astype(x.dtype)


def _modulate(x, shift, scale):
    return _layer_norm(x) * (1 + scale) + shift


def _axial_rope(row, col):
    half = QK_ROPE // 2
    inv = ROPE_THETA ** (-jnp.arange(0, half, 2, dtype=jnp.float32) / half)
    ang = jnp.concatenate([row[:, None] * inv, col[:, None] * inv], axis=-1)
    return jnp.cos(ang), jnp.sin(ang)


def _apply_rope(x, cos, sin):
    xp = x.reshape(x.shape[:-1] + (QK_ROPE // 2, 2))
    x1, x2 = xp[..., 0], xp[..., 1]
    cos = cos.astype(x.dtype)
    sin = sin.astype(x.dtype)
    out = jnp.stack([x1 * cos - x2 * sin, x1 * sin + x2 * cos], axis=-1)
    return out.reshape(x.shape)


def _mla_kv(ckv, kr, kv_norm_g, w_ukv, rope):
    B, T = ckv.shape[:2]
    kv = (_rms_norm(ckv, kv_norm_g) @ w_ukv).reshape(B, T, N_HEADS, QK_NOPE + V_DIM)
    k_nope, v = kv[..., :QK_NOPE], kv[..., QK_NOPE:]
    if rope is not None:
        kr = _apply_rope(kr, rope[0], rope[1])
    k_rope = jnp.broadcast_to(kr[:, :, None, :], (B, T, N_HEADS, QK_ROPE))
    return jnp.concatenate([k_nope, k_rope], axis=-1), v


def _mla_q(cq, q_norm_g, w_uq, rope):
    B, S = cq.shape[:2]
    q = (_rms_norm(cq, q_norm_g) @ w_uq).reshape(B, S, N_HEADS, QK_NOPE + QK_ROPE)
    q_nope, q_rope = q[..., :QK_NOPE], q[..., QK_NOPE:]
    if rope is not None:
        q_rope = _apply_rope(q_rope, rope[0][:, None, :], rope[1][:, None, :])
    return jnp.concatenate([q_nope, q_rope], axis=-1)


def _attend_blocks(q, k, v):
    B, S, H, Dqk = q.shape
    Dv = v.shape[-1]
    nb = S // Q_BLOCK
    scale = Dqk ** -0.5
    qb = q.reshape(B, nb, Q_BLOCK, H, Dqk).transpose(1, 0, 2, 3, 4)

    def one(q_blk):
        s = jnp.einsum('bqhd,bkhd->bhqk', q_blk, k, preferred_element_type=jnp.float32) * scale
        p = jax.nn.softmax(s, axis=-1)
        return jnp.einsum('bhqk,bkhd->bqhd', p.astype(v.dtype), v)

    o = lax.map(one, qb)
    return o.transpose(1, 0, 2, 3, 4).reshape(B, S, H, Dv)


def _fourier_mix(f):
    B, S, _ = f.shape
    g = f.reshape(B, S, N_FOURIER_GROUPS, FOURIER_GROUP_DIM).astype(jnp.float32)
    y = jnp.real(jnp.fft.fft2(g, axes=(1, 3), norm='ortho'))
    return y.reshape(B, S, FOURIER_DIM).astype(f.dtype)


def _mixer_out(q, k, v, f, ga, gb, w_o_mla, w_fourier, w_out):
    B, S = q.shape[:2]
    y_a = _attend_blocks(q, k, v).reshape(B, S, N_HEADS * V_DIM) @ w_o_mla
    y_b = _fourier_mix(f) @ w_fourier
    merged = jax.nn.sigmoid(ga) * y_a + jax.nn.sigmoid(gb) * y_b
    return merged @ w_out


def _peer(h, wq, keys, u, v):
    B, S, D = h.shape
    T = B * S
    hf = h.reshape(T, D)
    q = (hf @ wq).reshape(T, PEER_HEADS, 2, PEER_HALF)
    s = jnp.einsum('thpd,hpkd->thpk', q, keys, preferred_element_type=jnp.float32)
    s1, i1 = lax.top_k(s[:, :, 0], PEER_TOPK)
    s2, i2 = lax.top_k(s[:, :, 1], PEER_TOPK)
    cand_s = (s1[..., :, None] + s2[..., None, :]).reshape(T, PEER_HEADS, PEER_TOPK * PEER_TOPK)
    cand_i = (i1[..., :, None] * N_KEYS + i2[..., None, :]).reshape(T, PEER_HEADS, PEER_TOPK * PEER_TOPK)
    top_s, pos = lax.top_k(cand_s, PEER_TOPK)
    idx = jnp.take_along_axis(cand_i, pos, axis=-1)
    w = jax.nn.softmax(top_s, axis=-1)
    nb = T // TOKEN_BLOCK

    def one(args):
        hb, ib, wb = args
        a = jnp.einsum('td,thkd->thk', hb, u[ib], preferred_element_type=jnp.float32)
        coef = (wb * jax.nn.gelu(a, approximate=False)).astype(v.dtype)
        return jnp.einsum('thk,thkd->td', coef, v[ib])

    out = lax.map(one, (hf.reshape(nb, TOKEN_BLOCK, D),
                        idx.reshape(nb, TOKEN_BLOCK, PEER_HEADS, PEER_TOPK),
                        w.reshape(nb, TOKEN_BLOCK, PEER_HEADS, PEER_TOPK)))
    return out.reshape(B, S, D)


def setup_inputs(seed: int = 0) -> dict:
    key = jax.random.key(seed)
    ks = jax.random.split(key, 24)
    L, D = DEPTH, D_MODEL

    def nrm(k, shape, s):
        return jax.random.normal(k, shape, jnp.float32) * s

    return {
        'x': nrm(ks[0], (BATCH, SEQ, D), 1.0),
        'c': nrm(ks[1], (BATCH, D), 1.0),
        'ctx': nrm(ks[2], (BATCH, CTX_LEN, D), 1.0),
        'c_ctx': nrm(ks[3], (D,), 1.0),
        'w_mod': nrm(ks[4], (L, D, 6 * D), 0.5 * D ** -0.5),
        'b_mod': nrm(ks[5], (L, 6 * D), 0.01),
        'w_in': nrm(ks[6], (L, D, IN_DIM), D ** -0.5),
        'b_in': nrm(ks[7], (L, IN_DIM), 0.01),
        'q_norm_g': 1.0 + nrm(ks[8], (L, Q_LORA), 0.01),
        'w_uq': nrm(ks[9], (L, Q_LORA, N_HEADS * (QK_NOPE + QK_ROPE)), Q_LORA ** -0.5),
        'kv_norm_g': 1.0 + nrm(ks[10], (L, KV_LORA), 0.01),
        'w_ukv': nrm(ks[11], (L, KV_LORA, N_HEADS * (QK_NOPE + V_DIM)), KV_LORA ** -0.5),
        'w_o_mla': nrm(ks[12], (L, N_HEADS * V_DIM, D), (N_HEADS * V_DIM) ** -0.5),
        'w_fourier': nrm(ks[13], (L, FOURIER_DIM, D), FOURIER_DIM ** -0.5),
        'w_out': nrm(ks[14], (L, D, D), DEEPNORM_BETA * D ** -0.5),
        'ln1_g': 1.0 + nrm(ks[15], (L, D), 0.01),
        'ln1_b': nrm(ks[16], (L, D), 0.01),
        'peer_wq': nrm(ks[17], (L, D, PEER_HEADS * PEER_KEY_DIM), D ** -0.5),
        'peer_keys': nrm(ks[18], (L, PEER_HEADS, 2, N_KEYS, PEER_HALF), PEER_HALF ** -0.5),
        'peer_u': nrm(ks[19], (L, N_EXPERTS, D), D ** -0.5),
        'peer_v': nrm(ks[20], (L, N_EXPERTS, D), DEEPNORM_BETA * PEER_HEADS ** -0.5),
        'ln2_g': 1.0 + nrm(ks[21], (L, D), 0.01),
        'ln2_b': nrm(ks[22], (L, D), 0.01),
    }


def reference(x, c, ctx, c_ctx, w_mod, b_mod, w_in, b_in, q_norm_g, w_uq, kv_norm_g, w_ukv,
              w_o_mla, w_fourier, w_out, ln1_g, ln1_b, peer_wq, peer_keys, peer_u, peer_v,
              ln2_g, ln2_b):
    D = D_MODEL
    ROWS = x.shape[1] // GRID_W
    row = jnp.repeat(jnp.arange(ROWS, dtype=jnp.float32), GRID_W)
    col = jnp.tile(jnp.arange(GRID_W, dtype=jnp.float32), ROWS)
    rope = _axial_rope(row, col)

    for l in range(DEPTH):
        last = l == DEPTH - 1
        mod_x = (jax.nn.silu(c) @ w_mod[l] + b_mod[l])[:, None, :]
        sh1, sc1, g1, sh2, sc2, g2 = jnp.split(mod_x, 6, axis=-1)
        n_cm = 2 if last else 6
        mod_c = jax.nn.silu(c_ctx) @ w_mod[l][:, :n_cm * D] + b_mod[l][:n_cm * D]
        mc = jnp.split(mod_c, n_cm)

        hx = _modulate(x, sh1, sc1)
        hc = _modulate(ctx, mc[0], mc[1])
        px = hx @ w_in[l] + b_in[l]
        ckv_x, kr_x, cq_x, f_x, ga_x, gb_x = jnp.split(px, IN_SPLITS, axis=-1)
        n_cc = KV_END if last else IN_DIM
        pc = hc @ w_in[l][:, :n_cc] + b_in[l][:n_cc]
        parts_c = jnp.split(pc, IN_SPLITS[:1] if last else IN_SPLITS, axis=-1)

        k_c, v_c = _mla_kv(parts_c[0], parts_c[1], kv_norm_g[l], w_ukv[l], None)
        k_x, v_x = _mla_kv(ckv_x, kr_x, kv_norm_g[l], w_ukv[l], rope)
        q_x = _mla_q(cq_x, q_norm_g[l], w_uq[l], rope)
        y_x = _mixer_out(q_x, jnp.concatenate([k_c, k_x], axis=1), jnp.concatenate([v_c, v_x], axis=1),
                         f_x, ga_x, gb_x, w_o_mla[l], w_fourier[l], w_out[l])
        if not last:
            q_c = _mla_q(parts_c[2], q_norm_g[l], w_uq[l], None)
            y_c = _mixer_out(q_c, k_c, v_c, parts_c[3], parts_c[4], parts_c[5],
                             w_o_mla[l], w_fourier[l], w_out[l])
            ctx = _layer_norm(DEEPNORM_ALPHA * ctx + mc[2] * y_c, ln1_g[l], ln1_b[l])
        x = _layer_norm(DEEPNORM_ALPHA * x + g1 * y_x, ln1_g[l], ln1_b[l])

        y_x = _peer(_modulate(x, sh2, sc2), peer_wq[l], peer_keys[l], peer_u[l], peer_v[l])
        x = _layer_norm(DEEPNORM_ALPHA * x + g2 * y_x, ln2_g[l], ln2_b[l])
        if not last:
            y_c = _peer(_modulate(ctx, mc[3], mc[4]), peer_wq[l], peer_keys[l], peer_u[l], peer_v[l])
            ctx = _layer_norm(DEEPNORM_ALPHA * ctx + mc[5] * y_c, ln2_g[l], ln2_b[l])
    return x
```

```python
import functools
import math

import numpy as np
import jax
import jax.numpy as jnp
from jax import lax
from jax.experimental import pallas as pl
from jax.experimental.pallas import tpu as pltpu

D_MODEL = 2048
GRID_W = 64
N_HEADS = 16
QK_NOPE = 128
QK_ROPE = 64
V_DIM = 128
Q_LORA = 512
KV_LORA = 512
ROPE_THETA = 10000.0
N_FOURIER_GROUPS = 4
FOURIER_GROUP_DIM = 256
FOURIER_DIM = N_FOURIER_GROUPS * FOURIER_GROUP_DIM
KV_END = KV_LORA + QK_ROPE
PEER_HEADS = 8
N_KEYS = 128
N_EXPERTS = N_KEYS * N_KEYS
PEER_HALF = 128
PEER_TOPK = 16
DEPTH = 1
DEEPNORM_ALPHA = (2.0 * DEPTH) ** 0.25
EPS = 1e-6

LANES = 128
QK_PAD = 2 * LANES
VMEM_LIMIT = 56 * 1024 * 1024

BF = jnp.bfloat16
F32 = jnp.float32


def _params(n_axes, vmem=VMEM_LIMIT):
    return pltpu.CompilerParams(
        dimension_semantics=("arbitrary",) * n_axes, vmem_limit_bytes=vmem)


def _dot(a, b):
    return jnp.dot(a, b, preferred_element_type=F32)


def _layer_norm_rows(x):
    mu = jnp.mean(x, axis=-1, keepdims=True)
    xc = x - mu
    var = jnp.mean(xc * xc, axis=-1, keepdims=True)
    return xc * lax.rsqrt(var + EPS)


def _mod_kernel(c_ref, w_ref, b_ref, o_ref):
    a = jax.nn.silu(c_ref[...]).astype(BF)
    o_ref[...] = _dot(a, w_ref[...].astype(BF)) + b_ref[...]


def _mod_call(cmat, w_mod, b_mod):
    n = w_mod.shape[1]
    tn = 1024
    return pl.pallas_call(
        _mod_kernel,
        grid=(n // tn,),
        in_specs=[pl.BlockSpec((8, D_MODEL), lambda j: (0, 0)),
                  pl.BlockSpec((D_MODEL, tn), lambda j: (0, j)),
                  pl.BlockSpec((1, tn), lambda j: (0, j))],
        out_specs=pl.BlockSpec((8, tn), lambda j: (0, j)),
        out_shape=jax.ShapeDtypeStruct((8, n), F32),
        compiler_params=_params(1),
        name="adaln_mod",
    )(cmat, w_mod, b_mod.reshape(1, n))


def _ln_mod_kernel(x_ref, sh_ref, sc_ref, o_ref):
    y = _layer_norm_rows(x_ref[0])
    o_ref[0] = (y * (1.0 + sc_ref[0]) + sh_ref[0]).astype(BF)


def _ln_mod_call(x, shift, scale, tm):
    b, s, d = x.shape
    bm = shift.shape[0]
    mod_map = (lambda i, j: (i, 0, 0)) if bm == b else (lambda i, j: (0, 0, 0))
    return pl.pallas_call(
        _ln_mod_kernel,
        grid=(b, s // tm),
        in_specs=[pl.BlockSpec((1, tm, d), lambda i, j: (i, j, 0)),
                  pl.BlockSpec((1, 1, d), mod_map),
                  pl.BlockSpec((1, 1, d), mod_map)],
        out_specs=pl.BlockSpec((1, tm, d), lambda i, j: (i, j, 0)),
        out_shape=jax.ShapeDtypeStruct((b, s, d), BF),
        compiler_params=_params(2),
        name="ln_modulate",
    )(x, shift, scale)


LAT_COLS = KV_LORA + 2 * LANES + Q_LORA


def _latent_kernel(h_ref, w_ref, b_ref, gkv_ref, gq_ref, cos_ref, sin_ref,
                   ckv_ref, kr_ref, cq_ref):
    acc = _dot(h_ref[0], w_ref[...]) + b_ref[...]
    ckv = acc[:, :KV_LORA]
    ka = acc[:, KV_LORA:KV_LORA + LANES]
    kb = acc[:, KV_LORA + LANES:KV_LORA + 2 * LANES]
    cq = acc[:, KV_LORA + 2 * LANES:]
    ckv_n = ckv * lax.rsqrt(jnp.mean(ckv * ckv, axis=-1, keepdims=True) + EPS)
    cq_n = cq * lax.rsqrt(jnp.mean(cq * cq, axis=-1, keepdims=True) + EPS)
    ckv_ref[0] = (ckv_n * gkv_ref[...]).astype(BF)
    cq_ref[0] = (cq_n * gq_ref[...]).astype(BF)
    kr_ref[0] = (ka * cos_ref[...] + kb * sin_ref[...]).astype(BF)


def _latent_call(h, w_lat, b_lat, gkv, gq, cos, sin, tm):
    b, s, d = h.shape
    row = lambda i, j: (i, j, 0)
    const = lambda i, j: (0, 0)
    return pl.pallas_call(
        _latent_kernel,
        grid=(b, s // tm),
        in_specs=[pl.BlockSpec((1, tm, d), row),
                  pl.BlockSpec((d, LAT_COLS), const),
                  pl.BlockSpec((1, LAT_COLS), const),
                  pl.BlockSpec((1, KV_LORA), const),
                  pl.BlockSpec((1, Q_LORA), const),
                  pl.BlockSpec((tm, LANES), lambda i, j: (j, 0)),
                  pl.BlockSpec((tm, LANES), lambda i, j: (j, 0))],
        out_specs=[pl.BlockSpec((1, tm, KV_LORA), row),
                   pl.BlockSpec((1, tm, LANES), row),
                   pl.BlockSpec((1, tm, Q_LORA), row)],
        out_shape=[jax.ShapeDtypeStruct((b, s, KV_LORA), BF),
                   jax.ShapeDtypeStruct((b, s, LANES), BF),
                   jax.ShapeDtypeStruct((b, s, Q_LORA), BF)],
        compiler_params=_params(2),
        name="latent_proj",
    )(h, w_lat, b_lat, gkv, gq, cos, sin)


def _fproj_kernel(h_ref, w_ref, b_ref, dc_ref, gc_ref, gs_ref):
    f = (_dot(h_ref[...], w_ref[...]) + b_ref[...]).astype(BF)
    for g in range(N_FOURIER_GROUPS):
        lo = g * FOURIER_GROUP_DIM
        r = _dot(f[:, lo:lo + FOURIER_GROUP_DIM], dc_ref[...])
        gc_ref[:, lo:lo + FOURIER_GROUP_DIM] = r[:, :FOURIER_GROUP_DIM].astype(BF)
        gs_ref[:, lo:lo + FOURIER_GROUP_DIM] = r[:, FOURIER_GROUP_DIM:].astype(BF)


def _fproj_call(h2d, w_f, b_f, dc, tm):
    t, d = h2d.shape
    const = lambda i: (0, 0)
    return pl.pallas_call(
        _fproj_kernel,
        grid=(t // tm,),
        in_specs=[pl.BlockSpec((tm, d), lambda i: (i, 0)),
                  pl.BlockSpec((d, FOURIER_DIM), const),
                  pl.BlockSpec((1, FOURIER_DIM), const),
                  pl.BlockSpec((FOURIER_GROUP_DIM, 2 * FOURIER_GROUP_DIM), const)],
        out_specs=[pl.BlockSpec((tm, FOURIER_DIM), lambda i: (i, 0)),
                   pl.BlockSpec((tm, FOURIER_DIM), lambda i: (i, 0))],
        out_shape=[jax.ShapeDtypeStruct((t, FOURIER_DIM), BF),
                   jax.ShapeDtypeStruct((t, FOURIER_DIM), BF)],
        compiler_params=_params(1),
        name="fourier_in_proj",
    )(h2d, w_f, b_f, dc)


def _gate_kernel(h_ref, w_ref, b_ref, o_ref):
    o_ref[...] = jax.nn.sigmoid(_dot(h_ref[...], w_ref[...]) + b_ref[...]).astype(BF)


def _gate_call(h2d, w_g, b_g, tm, tn):
    t, d = h2d.shape
    n = w_g.shape[1]
    return pl.pallas_call(
        _gate_kernel,
        grid=(n // tn, t // tm),
        in_specs=[pl.BlockSpec((tm, d), lambda j, i: (i, 0)),
                  pl.BlockSpec((d, tn), lambda j, i: (0, j)),
                  pl.BlockSpec((1, tn), lambda j, i: (0, j))],
        out_specs=pl.BlockSpec((tm, tn), lambda j, i: (i, j)),
        out_shape=jax.ShapeDtypeStruct((t, n), BF),
        compiler_params=_params(2),
        name="gate_proj",
    )(h2d, w_g, b_g)


def _kv_up_kernel(lat_ref, kr_ref, w_ref, k_ref, v_ref):
    lat = lat_ref[0]
    kr = kr_ref[0]
    for h in range(N_HEADS):
        lo = h * (QK_NOPE + V_DIM)
        kv = _dot(lat, w_ref[:, lo:lo + QK_NOPE + V_DIM])
        k_ref[0, h] = jnp.concatenate([kv[:, :QK_NOPE].astype(BF), kr], axis=-1)
        v_ref[0, h] = kv[:, QK_NOPE:].astype(BF)


def _kv_up_call(lat, kr, w_ukv, tm):
    b, t, _ = lat.shape
    return pl.pallas_call(
        _kv_up_kernel,
        grid=(b, t // tm),
        in_specs=[pl.BlockSpec((1, tm, KV_LORA), lambda i, j: (i, j, 0)),
                  pl.BlockSpec((1, tm, LANES), lambda i, j: (i, j, 0)),
                  pl.BlockSpec(w_ukv.shape, lambda i, j: (0, 0))],
        out_specs=[pl.BlockSpec((1, N_HEADS, tm, QK_PAD), lambda i, j: (i, 0, j, 0)),
                   pl.BlockSpec((1, N_HEADS, tm, V_DIM), lambda i, j: (i, 0, j, 0))],
        out_shape=[jax.ShapeDtypeStruct((b, N_HEADS, t, QK_PAD), BF),
                   jax.ShapeDtypeStruct((b, N_HEADS, t, V_DIM), BF)],
        compiler_params=_params(2),
        name="kv_up_proj",
    )(lat, kr, w_ukv)


Q_HEAD_COLS = 3 * LANES


def _q_up_kernel(cq_ref, w_ref, cos_ref, sin_ref, q_ref):
    cq = cq_ref[0]
    cos = cos_ref[...]
    sin = sin_ref[...]
    scale = (QK_NOPE + QK_ROPE) ** -0.5
    for h in range(N_HEADS):
        lo = h * Q_HEAD_COLS
        acc = _dot(cq, w_ref[:, lo:lo + Q_HEAD_COLS])
        qn = acc[:, :LANES]
        qr = acc[:, LANES:2 * LANES] * cos + acc[:, 2 * LANES:] * sin
        q_ref[0, h] = (jnp.concatenate([qn, qr], axis=-1) * scale).astype(BF)


def _q_up_call(cq, w_q, cos, sin, tm):
    b, s, _ = cq.shape
    return pl.pallas_call(
        _q_up_kernel,
        grid=(b, s // tm),
        in_specs=[pl.BlockSpec((1, tm, Q_LORA), lambda i, j: (i, j, 0)),
                  pl.BlockSpec(w_q.shape, lambda i, j: (0, 0)),
                  pl.BlockSpec((tm, LANES), lambda i, j: (j, 0)),
                  pl.BlockSpec((tm, LANES), lambda i, j: (j, 0))],
        out_specs=pl.BlockSpec((1, N_HEADS, tm, QK_PAD), lambda i, j: (i, 0, j, 0)),
        out_shape=jax.ShapeDtypeStruct((b, N_HEADS, s, QK_PAD), BF),
        compiler_params=_params(2),
        name="q_up_proj",
    )(cq, w_q, cos, sin)


def _attn_kernel(q_ref, k_ref, v_ref, o_ref):
    s = lax.dot_general(q_ref[0, 0], k_ref[0, 0], (((1,), (1,)), ((), ())),
                        preferred_element_type=F32)
    m = jnp.max(s, axis=-1, keepdims=True)
    p = jnp.exp(s - m)
    l = jnp.sum(p, axis=-1, keepdims=True)
    o = _dot(p.astype(BF), v_ref[0, 0])
    o_ref[0] = (o / l).astype(BF)


def _attn_call(q, k, v, tq):
    b, h, s, _ = q.shape
    t = k.shape[2]
    return pl.pallas_call(
        _attn_kernel,
        grid=(b, h, s // tq),
        in_specs=[pl.BlockSpec((1, 1, tq, QK_PAD), lambda i, j, n: (i, j, n, 0)),
                  pl.BlockSpec((1, 1, t, QK_PAD), lambda i, j, n: (i, j, 0, 0)),
                  pl.BlockSpec((1, 1, t, V_DIM), lambda i, j, n: (i, j, 0, 0))],
        out_specs=pl.BlockSpec((1, tq, V_DIM), lambda i, j, n: (i, n, j)),
        out_shape=jax.ShapeDtypeStruct((b, s, h * V_DIM), BF),
        compiler_params=_params(3),
        name="mla_attention",
    )(q, k, v)


def _pos_dft_kernel(c_ref, s_ref, gc_ref, gs_ref, o_ref):
    o_ref[0] = (_dot(c_ref[...], gc_ref[0]) - _dot(s_ref[...], gs_ref[0])).astype(BF)


def _pos_dft_call(cs, ss, gc, gs, tm, tn):
    b, s, n = gc.shape
    return pl.pallas_call(
        _pos_dft_kernel,
        grid=(b, n // tn, s // tm),
        in_specs=[pl.BlockSpec((tm, s), lambda i, j, m: (m, 0)),
                  pl.BlockSpec((tm, s), lambda i, j, m: (m, 0)),
                  pl.BlockSpec((1, s, tn), lambda i, j, m: (i, 0, j)),
                  pl.BlockSpec((1, s, tn), lambda i, j, m: (i, 0, j))],
        out_specs=pl.BlockSpec((1, tm, tn), lambda i, j, m: (i, m, j)),
        out_shape=jax.ShapeDtypeStruct((b, s, n), BF),
        compiler_params=_params(3),
        name="position_dft",
    )(cs, ss, gc, gs)


def _merge_kernel(a_ref, f_ref, wo_ref, wf_ref, ga_ref, gb_ref, o_ref):
    ya = _dot(a_ref[...], wo_ref[...])
    yb = _dot(f_ref[...], wf_ref[...])
    o_ref[...] = (ga_ref[...].astype(F32) * ya + gb_ref[...].astype(F32) * yb).astype(BF)


def _merge_call(attn, fm, w_o, w_f, gates, tm, tn):
    t, d = attn.shape
    nb = D_MODEL // tn
    return pl.pallas_call(
        _merge_kernel,
        grid=(nb, t // tm),
        in_specs=[pl.BlockSpec((tm, d), lambda j, i: (i, 0)),
                  pl.BlockSpec((tm, FOURIER_DIM), lambda j, i: (i, 0)),
                  pl.BlockSpec((d, tn), lambda j, i: (0, j)),
                  pl.BlockSpec((FOURIER_DIM, tn), lambda j, i: (0, j)),
                  pl.BlockSpec((tm, tn), lambda j, i: (i, j)),
                  pl.BlockSpec((tm, tn), lambda j, i: (i, j + nb))],
        out_specs=pl.BlockSpec((tm, tn), lambda j, i: (i, j)),
        out_shape=jax.ShapeDtypeStruct((t, D_MODEL), BF),
        compiler_params=_params(2),
        name="branch_merge",
    )(attn, fm, w_o, w_f, gates, gates)


def _outproj_kernel(m_ref, w_ref, x_ref, g1_ref, lg_ref, lb_ref, sh_ref, sc_ref,
                    x1_ref, ht_ref):
    y = _dot(m_ref[...], w_ref[...])
    z = DEEPNORM_ALPHA * x_ref[0] + g1_ref[0] * y
    x1 = _layer_norm_rows(z) * lg_ref[...] + lb_ref[...]
    x1_ref[0] = x1
    h2 = _layer_norm_rows(x1) * (1.0 + sc_ref[0]) + sh_ref[0]
    ht_ref[...] = h2.T.astype(BF)


def _outproj_call(merged, w_out, x, g1, ln_g, ln_b, sh2, sc2, tm):
    b, s, d = x.shape
    nb = s // tm
    bmap = lambda i, j: (i, 0, 0)
    const = lambda i, j: (0, 0)
    return pl.pallas_call(
        _outproj_kernel,
        grid=(b, nb),
        in_specs=[pl.BlockSpec((tm, d), lambda i, j: (i * nb + j, 0)),
                  pl.BlockSpec((d, d), const),
                  pl.BlockSpec((1, tm, d), lambda i, j: (i, j, 0)),
                  pl.BlockSpec((1, 1, d), bmap),
                  pl.BlockSpec((1, d), const),
                  pl.BlockSpec((1, d), const),
                  pl.BlockSpec((1, 1, d), bmap),
                  pl.BlockSpec((1, 1, d), bmap)],
        out_specs=[pl.BlockSpec((1, tm, d), lambda i, j: (i, j, 0)),
                   pl.BlockSpec((d, tm), lambda i, j: (0, i * nb + j))],
        out_shape=[jax.ShapeDtypeStruct((b, s, d), F32),
                   jax.ShapeDtypeStruct((d, b * s), BF)],
        compiler_params=_params(2),
        name="out_proj_deepnorm",
    )(merged, w_out, x, g1, ln_g, ln_b, sh2, sc2)


N_HP = 2 * PEER_HEADS
SUBLANES = 8
TOP_ROWS = -(-(PEER_TOPK + 1) // SUBLANES) * SUBLANES


def _top_values(s, k):
    tops = []
    for _ in range(k):
        m = jnp.max(s, axis=0, keepdims=True)
        tops.append(m)
        s = jnp.where(s == m, -jnp.inf, s)
    return tops


def _select_kernel(wq_ref, keys_ref, ht_ref, thr_ref, e1_ref, s2_ref, e2_ref,
                   s_scr, top_scr):
    qt = _dot(wq_ref[...], ht_ref[...]).astype(BF)
    for hp in range(N_HP):
        s_scr[hp] = _dot(keys_ref[hp], qt[hp * PEER_HALF:(hp + 1) * PEER_HALF])

    def tops_body(hp, carry):
        tops = _top_values(s_scr[hp], PEER_TOPK + 1)
        for r in range(PEER_TOPK + 1):
            top_scr[hp, r:r + 1, :] = tops[r]
        return carry
    lax.fori_loop(0, N_HP, tops_body, 0)

    def head_body(h, carry):
        t1 = top_scr[2 * h, 0:PEER_TOPK, :]
        t2 = top_scr[2 * h + 1, 0:PEER_TOPK, :]
        m1 = t1[0:1, :]
        m2 = t2[0:1, :]
        cand = jnp.concatenate([t1[a:a + 1, :] + t2 for a in range(PEER_TOPK)], axis=0)
        best = _top_values(cand, PEER_TOPK + 1)
        outside = jnp.maximum(top_scr[2 * h, PEER_TOPK:PEER_TOPK + 1, :] + m2,
                              m1 + top_scr[2 * h + 1, PEER_TOPK:PEER_TOPK + 1, :])
        runner_up = jnp.maximum(best[PEER_TOPK], outside)
        tau = 0.5 * (best[PEER_TOPK - 1] + runner_up)
        z = jnp.sum(jnp.where(cand >= tau, jnp.exp(cand - (m1 + m2)), 0.0),
                    axis=0, keepdims=True)
        s1 = s_scr[2 * h]
        s2 = s_scr[2 * h + 1]
        thr_ref[h] = tau - s1
        e1_ref[h] = jnp.exp(s1 - m1) / z
        s2_ref[h] = s2
        e2_ref[h] = jnp.exp(s2 - m2)
        return carry
    lax.fori_loop(0, PEER_HEADS, head_body, 0)


def _select_call(wq_t, keys, h_t, tn):
    d, t = h_t.shape
    out = jax.ShapeDtypeStruct((PEER_HEADS, N_KEYS, t), F32)
    ospec = pl.BlockSpec((PEER_HEADS, N_KEYS, tn), lambda i: (0, 0, i))
    return pl.pallas_call(
        _select_kernel,
        grid=(t // tn,),
        in_specs=[pl.BlockSpec(wq_t.shape, lambda i: (0, 0)),
                  pl.BlockSpec(keys.shape, lambda i: (0, 0, 0)),
                  pl.BlockSpec((d, tn), lambda i: (0, i))],
        out_specs=[ospec, ospec, ospec, ospec],
        out_shape=[out, out, out, out],
        scratch_shapes=[pltpu.VMEM((N_HP, N_KEYS, tn), F32),
                        pltpu.VMEM((N_HP, TOP_ROWS, tn), F32)],
        compiler_params=_params(1),
        name="peer_select",
    )(wq_t, keys, h_t)


def _gelu(x):
    return 0.5 * x * (1.0 + lax.erf(x * math.sqrt(0.5)))


def _peer_kernel(u_ref, vt_ref, ht_ref, thr_ref, e1_ref, s2_ref, e2_ref, o_ref,
                 c_scr, *, rows_per_step):
    e = pl.program_id(1)

    @pl.when(e == 0)
    def _():
        o_ref[...] = jnp.zeros_like(o_ref)

    a = _dot(u_ref[...], ht_ref[...])
    for r in range(rows_per_step):
        i1 = e * rows_per_step + r
        w = None
        for h in range(PEER_HEADS):
            thr = thr_ref[h, pl.ds(i1, 1), :]
            e1 = e1_ref[h, pl.ds(i1, 1), :]
            contrib = jnp.where(s2_ref[h] >= thr, e1 * e2_ref[h], 0.0)
            w = contrib if w is None else w + contrib
        act = _gelu(a[r * N_KEYS:(r + 1) * N_KEYS])
        c_scr[r * N_KEYS:(r + 1) * N_KEYS, :] = (w * act).astype(BF)
    o_ref[...] += _dot(vt_ref[...], c_scr[...])


def _peer_call(u, v_t, h_t, thr, e1, s2, e2, tn, te):
    d, t = h_t.shape
    rows = te // N_KEYS
    sel = pl.BlockSpec((PEER_HEADS, N_KEYS, tn), lambda i, e: (0, 0, i))
    return pl.pallas_call(
        functools.partial(_peer_kernel, rows_per_step=rows),
        grid=(t // tn, N_EXPERTS // te),
        in_specs=[pl.BlockSpec((te, d), lambda i, e: (e, 0)),
                  pl.BlockSpec((d, te), lambda i, e: (0, e)),
                  pl.BlockSpec((d, tn), lambda i, e: (0, i)),
                  sel, sel, sel, sel],
        out_specs=pl.BlockSpec((d, tn), lambda i, e: (0, i)),
        out_shape=jax.ShapeDtypeStruct((d, t), F32),
        scratch_shapes=[pltpu.VMEM((te, tn), BF)],
        compiler_params=_params(2),
        name="peer_dense",
    )(u, v_t, h_t, thr, e1, s2, e2)


def _final_kernel(yt_ref, x_ref, g2_ref, lg_ref, lb_ref, o_ref):
    z = DEEPNORM_ALPHA * x_ref[0] + g2_ref[0] * yt_ref[...].T
    o_ref[0] = _layer_norm_rows(z) * lg_ref[...] + lb_ref[...]


def _final_call(y_t, x1, g2, ln_g, ln_b, tm):
    b, s, d = x1.shape
    nb = s // tm
    return pl.pallas_call(
        _final_kernel,
        grid=(b, nb),
        in_specs=[pl.BlockSpec((d, tm), lambda i, j: (0, i * nb + j)),
                  pl.BlockSpec((1, tm, d), lambda i, j: (i, j, 0)),
                  pl.BlockSpec((1, 1, d), lambda i, j: (i, 0, 0)),
                  pl.BlockSpec((1, d), lambda i, j: (0, 0)),
                  pl.BlockSpec((1, d), lambda i, j: (0, 0))],
        out_specs=pl.BlockSpec((1, tm, d), lambda i, j: (i, j, 0)),
        out_shape=jax.ShapeDtypeStruct((b, s, d), F32),
        compiler_params=_params(2),
        name="final_deepnorm",
    )(y_t, x1, g2, ln_g, ln_b)


def _rope_rotation(w):
    pairs = w.reshape(w.shape[:-1] + (w.shape[-1] // 2, 2))
    return jnp.stack([-pairs[..., 1], pairs[..., 0]], axis=-1).reshape(w.shape)


def _pad_lanes(w):
    return jnp.pad(w, [(0, 0)] * (w.ndim - 1) + [(0, LANES - w.shape[-1])])


def _rope_tables(seq):
    rows = seq // GRID_W
    row = jnp.repeat(jnp.arange(rows, dtype=F32), GRID_W)
    col = jnp.tile(jnp.arange(GRID_W, dtype=F32), rows)
    half = QK_ROPE // 2
    inv = ROPE_THETA ** (-jnp.arange(0, half, 2, dtype=F32) / half)
    ang = jnp.concatenate([row[:, None] * inv, col[:, None] * inv], axis=-1)
    cos = _pad_lanes(jnp.repeat(jnp.cos(ang), 2, axis=-1))
    sin = _pad_lanes(jnp.repeat(jnp.sin(ang), 2, axis=-1))
    return cos, sin


def _dft_matrices(n, scale):
    k = np.arange(n, dtype=np.int64)
    ang = 2.0 * np.pi * ((k[:, None] * k[None, :]) % n).astype(np.float64) / n
    return np.cos(ang) * scale, np.sin(ang) * scale


def kernel(x, c, ctx, c_ctx, w_mod, b_mod, w_in, b_in, q_norm_g, w_uq, kv_norm_g, w_ukv,
           w_o_mla, w_fourier, w_out, ln1_g, ln1_b, peer_wq, peer_keys, peer_u, peer_v,
           ln2_g, ln2_b):
    B, S, D = x.shape
    T = B * S
    CT = ctx.shape[1]
    l = 0

    cmat = jnp.concatenate([c, c_ctx[None, :], jnp.zeros((8 - B - 1, D), F32)], axis=0)
    mod = _mod_call(cmat, w_mod[l], b_mod[l])
    mx = mod[:B].reshape(B, 1, 6, D)
    sh1, sc1, g1, sh2, sc2, g2 = [mx[:, :, i, :] for i in range(6)]
    mc = mod[B].reshape(1, 1, 6, D)
    sh1c, sc1c = mc[:, :, 0, :], mc[:, :, 1, :]

    wi, bi = w_in[l], b_in[l]
    w_kr, b_kr = wi[:, KV_LORA:KV_END], bi[KV_LORA:KV_END]
    q0 = KV_END
    f0 = KV_END + Q_LORA
    g0 = f0 + FOURIER_DIM
    w_lat = jnp.concatenate(
        [wi[:, :KV_LORA], _pad_lanes(w_kr), _pad_lanes(_rope_rotation(w_kr)), wi[:, q0:f0]],
        axis=1).astype(BF)
    b_lat = jnp.concatenate(
        [bi[:KV_LORA], _pad_lanes(b_kr), _pad_lanes(_rope_rotation(b_kr)), bi[q0:f0]])[None, :]
    w_f = wi[:, f0:g0].astype(BF)
    b_f = bi[f0:g0][None, :]
    w_g = wi[:, g0:].astype(BF)
    b_g = bi[g0:][None, :]
    wq3 = w_uq[l].reshape(Q_LORA, N_HEADS, QK_NOPE + QK_ROPE)
    wq_rope = wq3[:, :, QK_NOPE:]
    w_q = jnp.concatenate(
        [wq3[:, :, :QK_NOPE], _pad_lanes(wq_rope), _pad_lanes(_rope_rotation(wq_rope))],
        axis=-1).reshape(Q_LORA, N_HEADS * Q_HEAD_COLS).astype(BF)
    w_kv = w_ukv[l].astype(BF)
    gkv = kv_norm_g[l][None, :]
    gq = q_norm_g[l][None, :]

    cos, sin = _rope_tables(S)
    cos_c = _pad_lanes(jnp.ones((CT, QK_ROPE), F32))
    sin_c = jnp.zeros((CT, LANES), F32)
    dc_c, dc_s = _dft_matrices(FOURIER_GROUP_DIM, FOURIER_GROUP_DIM ** -0.5)
    dc = jnp.asarray(np.concatenate([dc_c, dc_s], axis=1), dtype=F32).astype(BF)
    ds_c, ds_s = _dft_matrices(S, S ** -0.5)
    cs = jnp.asarray(ds_c, dtype=F32).astype(BF)
    ss = jnp.asarray(ds_s, dtype=F32).astype(BF)

    hx = _ln_mod_call(x, sh1, sc1, 512)
    hc = _ln_mod_call(ctx, sh1c, sc1c, CT)
    ckv_x, kr_x, cq_x = _latent_call(hx, w_lat, b_lat, gkv, gq, cos, sin, 512)
    ckv_c, kr_c, _ = _latent_call(hc, w_lat, b_lat, gkv, gq, cos_c, sin_c, CT)
    lat = jnp.concatenate([ckv_c, ckv_x], axis=1)
    kr = jnp.concatenate([kr_c, kr_x], axis=1)
    k_all, v_all = _kv_up_call(lat, kr, w_kv, 256)
    q_all = _q_up_call(cq_x, w_q, cos, sin, 256)
    attn = _attn_call(q_all, k_all, v_all, 512)

    hx2d = hx.reshape(T, D)
    gc, gs = _fproj_call(hx2d, w_f, b_f, dc, 512)
    fm = _pos_dft_call(cs, ss, gc.reshape(B, S, FOURIER_DIM), gs.reshape(B, S, FOURIER_DIM),
                       512, 512)
    gates = _gate_call(hx2d, w_g, b_g, 512, 1024)
    merged = _merge_call(attn.reshape(T, D), fm.reshape(T, FOURIER_DIM),
                         w_o_mla[l].astype(BF), w_fourier[l].astype(BF), gates, 512, 512)
    x1, h2_t = _outproj_call(merged, w_out[l].astype(BF), x, g1, ln1_g[l][None, :],
                             ln1_b[l][None, :], sh2, sc2, 256)

    wq_t = peer_wq[l].T.astype(BF)
    keys = peer_keys[l].reshape(N_HP, N_KEYS, PEER_HALF).astype(BF)
    thr, e1, s2, e2 = _select_call(wq_t, keys, h2_t, 256)
    y_t = _peer_call(peer_u[l].astype(BF), peer_v[l].T.astype(BF), h2_t, thr, e1, s2, e2,
                     512, 1024)
    return _final_call(y_t, x1, g2, ln2_g[l][None, :], ln2_b[l][None, :], 256)
```

```python
import functools
import math

import numpy as np
import jax
import jax.numpy as jnp
from jax import lax
from jax.experimental import pallas as pl
from jax.experimental.pallas import tpu as pltpu

D_MODEL = 2048
GRID_W = 64
N_HEADS = 16
QK_NOPE = 128
QK_ROPE = 64
V_DIM = 128
Q_LORA = 512
KV_LORA = 512
ROPE_THETA = 10000.0
N_FOURIER_GROUPS = 4
FOURIER_GROUP_DIM = 256
FOURIER_DIM = N_FOURIER_GROUPS * FOURIER_GROUP_DIM
KV_END = KV_LORA + QK_ROPE
PEER_HEADS = 8
N_KEYS = 128
N_EXPERTS = N_KEYS * N_KEYS
PEER_HALF = 128
PEER_TOPK = 16
DEPTH = 1
DEEPNORM_ALPHA = (2.0 * DEPTH) ** 0.25
EPS = 1e-6

LANES = 128
QK_PAD = 2 * LANES
VMEM_LIMIT = 56 * 1024 * 1024

BF = jnp.bfloat16
F32 = jnp.float32


def _params(n_axes, vmem=VMEM_LIMIT):
    return pltpu.CompilerParams(
        dimension_semantics=("arbitrary",) * n_axes, vmem_limit_bytes=vmem)


def _dot(a, b):
    return jnp.dot(a, b, preferred_element_type=F32)


def _layer_norm_rows(x):
    mu = jnp.mean(x, axis=-1, keepdims=True)
    xc = x - mu
    var = jnp.mean(xc * xc, axis=-1, keepdims=True)
    return xc * lax.rsqrt(var + EPS)


def _mod_kernel(c_ref, w_ref, b_ref, o_ref):
    a = jax.nn.silu(c_ref[...]).astype(BF)
    o_ref[...] = _dot(a, w_ref[...].astype(BF)) + b_ref[...]


def _mod_call(cmat, w_mod, b_mod):
    n = w_mod.shape[1]
    tn = 1024
    return pl.pallas_call(
        _mod_kernel,
        grid=(n // tn,),
        in_specs=[pl.BlockSpec((8, D_MODEL), lambda j: (0, 0)),
                  pl.BlockSpec((D_MODEL, tn), lambda j: (0, j)),
                  pl.BlockSpec((1, tn), lambda j: (0, j))],
        out_specs=pl.BlockSpec((8, tn), lambda j: (0, j)),
        out_shape=jax.ShapeDtypeStruct((8, n), F32),
        compiler_params=_params(1),
        name="adaln_mod",
    )(cmat, w_mod, b_mod.reshape(1, n))


def _ln_mod_kernel(x_ref, sh_ref, sc_ref, o_ref):
    y = _layer_norm_rows(x_ref[0])
    o_ref[0] = (y * (1.0 + sc_ref[0]) + sh_ref[0]).astype(BF)


def _ln_mod_call(x, shift, scale, tm):
    b, s, d = x.shape
    bm = shift.shape[0]
    mod_map = (lambda i, j: (i, 0, 0)) if bm == b else (lambda i, j: (0, 0, 0))
    return pl.pallas_call(
        _ln_mod_kernel,
        grid=(b, s // tm),
        in_specs=[pl.BlockSpec((1, tm, d), lambda i, j: (i, j, 0)),
                  pl.BlockSpec((1, 1, d), mod_map),
                  pl.BlockSpec((1, 1, d), mod_map)],
        out_specs=pl.BlockSpec((1, tm, d), lambda i, j: (i, j, 0)),
        out_shape=jax.ShapeDtypeStruct((b, s, d), BF),
        compiler_params=_params(2),
        name="ln_modulate",
    )(x, shift, scale)


LAT_COLS = KV_LORA + 2 * LANES + Q_LORA


def _latent_kernel(h_ref, w_ref, b_ref, gkv_ref, gq_ref, cos_ref, sin_ref,
                   ckv_ref, kr_ref, cq_ref):
    acc = _dot(h_ref[0], w_ref[...]) + b_ref[...]
    ckv = acc[:, :KV_LORA]
    ka = acc[:, KV_LORA:KV_LORA + LANES]
    kb = acc[:, KV_LORA + LANES:KV_LORA + 2 * LANES]
    cq = acc[:, KV_LORA + 2 * LANES:]
    ckv_n = ckv * lax.rsqrt(jnp.mean(ckv * ckv, axis=-1, keepdims=True) + EPS)
    cq_n = cq * lax.rsqrt(jnp.mean(cq * cq, axis=-1, keepdims=True) + EPS)
    ckv_ref[0] = (ckv_n * gkv_ref[...]).astype(BF)
    cq_ref[0] = (cq_n * gq_ref[...]).astype(BF)
    kr_ref[0] = (ka * cos_ref[...] + kb * sin_ref[...]).astype(BF)


def _latent_call(h, w_lat, b_lat, gkv, gq, cos, sin, tm):
    b, s, d = h.shape
    row = lambda i, j: (i, j, 0)
    const = lambda i, j: (0, 0)
    return pl.pallas_call(
        _latent_kernel,
        grid=(b, s // tm),
        in_specs=[pl.BlockSpec((1, tm, d), row),
                  pl.BlockSpec((d, LAT_COLS), const),
                  pl.BlockSpec((1, LAT_COLS), const),
                  pl.BlockSpec((1, KV_LORA), const),
                  pl.BlockSpec((1, Q_LORA), const),
                  pl.BlockSpec((tm, LANES), lambda i, j: (j, 0)),
                  pl.BlockSpec((tm, LANES), lambda i, j: (j, 0))],
        out_specs=[pl.BlockSpec((1, tm, KV_LORA), row),
                   pl.BlockSpec((1, tm, LANES), row),
                   pl.BlockSpec((1, tm, Q_LORA), row)],
        out_shape=[jax.ShapeDtypeStruct((b, s, KV_LORA), BF),
                   jax.ShapeDtypeStruct((b, s, LANES), BF),
                   jax.ShapeDtypeStruct((b, s, Q_LORA), BF)],
        compiler_params=_params(2),
        name="latent_proj",
    )(h, w_lat, b_lat, gkv, gq, cos, sin)


def _fproj_kernel(h_ref, w_ref, b_ref, dc_ref, gc_ref, gs_ref):
    f = (_dot(h_ref[...], w_ref[...]) + b_ref[...]).astype(BF)
    for g in range(N_FOURIER_GROUPS):
        lo = g * FOURIER_GROUP_DIM
        r = _dot(f[:, lo:lo + FOURIER_GROUP_DIM], dc_ref[...])
        gc_ref[:, lo:lo + FOURIER_GROUP_DIM] = r[:, :FOURIER_GROUP_DIM].astype(BF)
        gs_ref[:, lo:lo + FOURIER_GROUP_DIM] = r[:, FOURIER_GROUP_DIM:].astype(BF)


def _fproj_call(h2d, w_f, b_f, dc, tm):
    t, d = h2d.shape
    const = lambda i: (0, 0)
    return pl.pallas_call(
        _fproj_kernel,
        grid=(t // tm,),
        in_specs=[pl.BlockSpec((tm, d), lambda i: (i, 0)),
                  pl.BlockSpec((d, FOURIER_DIM), const),
                  pl.BlockSpec((1, FOURIER_DIM), const),
                  pl.BlockSpec((FOURIER_GROUP_DIM, 2 * FOURIER_GROUP_DIM), const)],
        out_specs=[pl.BlockSpec((tm, FOURIER_DIM), lambda i: (i, 0)),
                   pl.BlockSpec((tm, FOURIER_DIM), lambda i: (i, 0))],
        out_shape=[jax.ShapeDtypeStruct((t, FOURIER_DIM), BF),
                   jax.ShapeDtypeStruct((t, FOURIER_DIM), BF)],
        compiler_params=_params(1),
        name="fourier_in_proj",
    )(h2d, w_f, b_f, dc)


def _gate_kernel(h_ref, w_ref, b_ref, o_ref):
    o_ref[...] = jax.nn.sigmoid(_dot(h_ref[...], w_ref[...]) + b_ref[...]).astype(BF)


def _gate_call(h2d, w_g, b_g, tm, tn):
    t, d = h2d.shape
    n = w_g.shape[1]
    return pl.pallas_call(
        _gate_kernel,
        grid=(n // tn, t // tm),
        in_specs=[pl.BlockSpec((tm, d), lambda j, i: (i, 0)),
                  pl.BlockSpec((d, tn), lambda j, i: (0, j)),
                  pl.BlockSpec((1, tn), lambda j, i: (0, j))],
        out_specs=pl.BlockSpec((tm, tn), lambda j, i: (i, j)),
        out_shape=jax.ShapeDtypeStruct((t, n), BF),
        compiler_params=_params(2),
        name="gate_proj",
    )(h2d, w_g, b_g)


def _kv_up_kernel(lat_ref, kr_ref, w_ref, k_ref, v_ref):
    lat = lat_ref[0]
    kr = kr_ref[0]
    for h in range(N_HEADS):
        lo = h * (QK_NOPE + V_DIM)
        kv = _dot(lat, w_ref[:, lo:lo + QK_NOPE + V_DIM])
        k_ref[0, h] = jnp.concatenate([kv[:, :QK_NOPE].astype(BF), kr], axis=-1)
        v_ref[0, h] = kv[:, QK_NOPE:].astype(BF)


def _kv_up_call(lat, kr, w_ukv, tm):
    b, t, _ = lat.shape
    return pl.pallas_call(
        _kv_up_kernel,
        grid=(b, t // tm),
        in_specs=[pl.BlockSpec((1, tm, KV_LORA), lambda i, j: (i, j, 0)),
                  pl.BlockSpec((1, tm, LANES), lambda i, j: (i, j, 0)),
                  pl.BlockSpec(w_ukv.shape, lambda i, j: (0, 0))],
        out_specs=[pl.BlockSpec((1, N_HEADS, tm, QK_PAD), lambda i, j: (i, 0, j, 0)),
                   pl.BlockSpec((1, N_HEADS, tm, V_DIM), lambda i, j: (i, 0, j, 0))],
        out_shape=[jax.ShapeDtypeStruct((b, N_HEADS, t, QK_PAD), BF),
                   jax.ShapeDtypeStruct((b, N_HEADS, t, V_DIM), BF)],
        compiler_params=_params(2),
        name="kv_up_proj",
    )(lat, kr, w_ukv)


Q_HEAD_COLS = 3 * LANES


def _q_up_kernel(cq_ref, w_ref, cos_ref, sin_ref, q_ref):
    cq = cq_ref[0]
    cos = cos_ref[...]
    sin = sin_ref[...]
    scale = (QK_NOPE + QK_ROPE) ** -0.5
    for h in range(N_HEADS):
        lo = h * Q_HEAD_COLS
        acc = _dot(cq, w_ref[:, lo:lo + Q_HEAD_COLS])
        qn = acc[:, :LANES]
        qr = acc[:, LANES:2 * LANES] * cos + acc[:, 2 * LANES:] * sin
        q_ref[0, h] = (jnp.concatenate([qn, qr], axis=-1) * scale).astype(BF)


def _q_up_call(cq, w_q, cos, sin, tm):
    b, s, _ = cq.shape
    return pl.pallas_call(
        _q_up_kernel,
        grid=(b, s // tm),
        in_specs=[pl.BlockSpec((1, tm, Q_LORA), lambda i, j: (i, j, 0)),
                  pl.BlockSpec(w_q.shape, lambda i, j: (0, 0)),
                  pl.BlockSpec((tm, LANES), lambda i, j: (j, 0)),
                  pl.BlockSpec((tm, LANES), lambda i, j: (j, 0))],
        out_specs=pl.BlockSpec((1, N_HEADS, tm, QK_PAD), lambda i, j: (i, 0, j, 0)),
        out_shape=jax.ShapeDtypeStruct((b, N_HEADS, s, QK_PAD), BF),
        compiler_params=_params(2),
        name="q_up_proj",
    )(cq, w_q, cos, sin)


def _attn_kernel(q_ref, k_ref, v_ref, o_ref):
    s = lax.dot_general(q_ref[0, 0], k_ref[0, 0], (((1,), (1,)), ((), ())),
                        preferred_element_type=F32)
    m = jnp.max(s, axis=-1, keepdims=True)
    p = jnp.exp(s - m)
    l = jnp.sum(p, axis=-1, keepdims=True)
    o = _dot(p.astype(BF), v_ref[0, 0])
    o_ref[0] = (o / l).astype(BF)


def _attn_call(q, k, v, tq):
    b, h, s, _ = q.shape
    t = k.shape[2]
    return pl.pallas_call(
        _attn_kernel,
        grid=(b, h, s // tq),
        in_specs=[pl.BlockSpec((1, 1, tq, QK_PAD), lambda i, j, n: (i, j, n, 0)),
                  pl.BlockSpec((1, 1, t, QK_PAD), lambda i, j, n: (i, j, 0, 0)),
                  pl.BlockSpec((1, 1, t, V_DIM), lambda i, j, n: (i, j, 0, 0))],
        out_specs=pl.BlockSpec((1, tq, V_DIM), lambda i, j, n: (i, n, j)),
        out_shape=jax.ShapeDtypeStruct((b, s, h * V_DIM), BF),
        compiler_params=_params(3),
        name="mla_attention",
    )(q, k, v)


def _pos_dft_kernel(c_ref, s_ref, gc_ref, gs_ref, o_ref):
    o_ref[0] = (_dot(c_ref[...], gc_ref[0]) - _dot(s_ref[...], gs_ref[0])).astype(BF)


def _pos_dft_call(cs, ss, gc, gs, tm, tn):
    b, s, n = gc.shape
    return pl.pallas_call(
        _pos_dft_kernel,
        grid=(b, n // tn, s // tm),
        in_specs=[pl.BlockSpec((tm, s), lambda i, j, m: (m, 0)),
                  pl.BlockSpec((tm, s), lambda i, j, m: (m, 0)),
                  pl.BlockSpec((1, s, tn), lambda i, j, m: (i, 0, j)),
                  pl.BlockSpec((1, s, tn), lambda i, j, m: (i, 0, j))],
        out_specs=pl.BlockSpec((1, tm, tn), lambda i, j, m: (i, m, j)),
        out_shape=jax.ShapeDtypeStruct((b, s, n), BF),
        compiler_params=_params(3),
        name="position_dft",
    )(cs, ss, gc, gs)


def _merge_kernel(a_ref, f_ref, wo_ref, wf_ref, ga_ref, gb_ref, o_ref):
    ya = _dot(a_ref[...], wo_ref[...])
    yb = _dot(f_ref[...], wf_ref[...])
    o_ref[...] = (ga_ref[...].astype(F32) * ya + gb_ref[...].astype(F32) * yb).astype(BF)


def _merge_call(attn, fm, w_o, w_f, gates, tm, tn):
    t, d = attn.shape
    nb = D_MODEL // tn
    return pl.pallas_call(
        _merge_kernel,
        grid=(nb, t // tm),
        in_specs=[pl.BlockSpec((tm, d), lambda j, i: (i, 0)),
                  pl.BlockSpec((tm, FOURIER_DIM), lambda j, i: (i, 0)),
                  pl.BlockSpec((d, tn), lambda j, i: (0, j)),
                  pl.BlockSpec((FOURIER_DIM, tn), lambda j, i: (0, j)),
                  pl.BlockSpec((tm, tn), lambda j, i: (i, j)),
                  pl.BlockSpec((tm, tn), lambda j, i: (i, j + nb))],
        out_specs=pl.BlockSpec((tm, tn), lambda j, i: (i, j)),
        out_shape=jax.ShapeDtypeStruct((t, D_MODEL), BF),
        compiler_params=_params(2),
        name="branch_merge",
    )(attn, fm, w_o, w_f, gates, gates)


def _outproj_kernel(m_ref, w_ref, x_ref, g1_ref, lg_ref, lb_ref, sh_ref, sc_ref,
                    x1_ref, ht_ref):
    y = _dot(m_ref[...], w_ref[...])
    z = DEEPNORM_ALPHA * x_ref[0] + g1_ref[0] * y
    x1 = _layer_norm_rows(z) * lg_ref[...] + lb_ref[...]
    x1_ref[0] = x1
    h2 = _layer_norm_rows(x1) * (1.0 + sc_ref[0]) + sh_ref[0]
    ht_ref[...] = h2.T.astype(BF)


def _outproj_call(merged, w_out, x, g1, ln_g, ln_b, sh2, sc2, tm):
    b, s, d = x.shape
    nb = s // tm
    bmap = lambda i, j: (i, 0, 0)
    const = lambda i, j: (0, 0)
    return pl.pallas_call(
        _outproj_kernel,
        grid=(b, nb),
        in_specs=[pl.BlockSpec((tm, d), lambda i, j: (i * nb + j, 0)),
                  pl.BlockSpec((d, d), const),
                  pl.BlockSpec((1, tm, d), lambda i, j: (i, j, 0)),
                  pl.BlockSpec((1, 1, d), bmap),
                  pl.BlockSpec((1, d), const),
                  pl.BlockSpec((1, d), const),
                  pl.BlockSpec((1, 1, d), bmap),
                  pl.BlockSpec((1, 1, d), bmap)],
        out_specs=[pl.BlockSpec((1, tm, d), lambda i, j: (i, j, 0)),
                   pl.BlockSpec((d, tm), lambda i, j: (0, i * nb + j))],
        out_shape=[jax.ShapeDtypeStruct((b, s, d), F32),
                   jax.ShapeDtypeStruct((d, b * s), BF)],
        compiler_params=_params(2),
        name="out_proj_deepnorm",
    )(merged, w_out, x, g1, ln_g, ln_b, sh2, sc2)


N_HP = 2 * PEER_HEADS
NOT_TOP = 127.0
HALF_K = PEER_TOPK // 2
BF16_ROWS = 16


def _top_values(s, k):
    tops = []
    for _ in range(k):
        m = jnp.max(s, axis=0, keepdims=True)
        tops.append(m)
        s = jnp.where(s == m, -jnp.inf, s)
    return tops


def _top_values_ranked(s, k):
    tops = []
    rank = jnp.full(s.shape, NOT_TOP, F32)
    for r in range(k):
        m = jnp.max(s, axis=0, keepdims=True)
        tops.append(m)
        hit = s == m
        rank = jnp.where(hit, float(r), rank)
        s = jnp.where(hit, -jnp.inf, s)
    return tops, rank


def _select_kernel(wq_ref, keys_ref, ht_ref, rank_ref, cnt_ref, e1_ref, e2_ref,
                   s_scr, top_scr):
    qt = _dot(wq_ref[...], ht_ref[...]).astype(BF)
    for hp in range(N_HP):
        s_scr[hp] = _dot(keys_ref[hp], qt[hp * PEER_HALF:(hp + 1) * PEER_HALF])

    def head_body(h, carry):
        s1 = s_scr[2 * h]
        s2 = s_scr[2 * h + 1]
        tops1 = _top_values(s1, PEER_TOPK + 1)
        tops2, rank2 = _top_values_ranked(s2, PEER_TOPK + 1)
        for r in range(PEER_TOPK):
            top_scr[0, r:r + 1, :] = tops1[r]
            top_scr[1, r:r + 1, :] = tops2[r]
        t1 = top_scr[0]
        t2 = top_scr[1]
        m1, m2 = tops1[0], tops2[0]
        cand = jnp.concatenate(
            [m1 + t2]
            + [tops1[a] + t2[:HALF_K] for a in range(1, HALF_K)]
            + [t1[HALF_K:] + m2], axis=0)
        best = _top_values(cand, PEER_TOPK + 1)
        outside = jnp.maximum(tops1[PEER_TOPK] + m2, m1 + tops2[PEER_TOPK])
        runner_up = jnp.maximum(best[PEER_TOPK], outside)
        tau = 0.5 * (best[PEER_TOPK - 1] + runner_up)
        sel = cand >= tau
        z = jnp.sum(jnp.where(sel, jnp.exp(cand - (m1 + m2)), 0.0), axis=0, keepdims=True)
        self32 = sel.astype(F32)
        counts = [jnp.sum(self32[:PEER_TOPK], axis=0, keepdims=True)]
        for a in range(1, HALF_K):
            lo = PEER_TOPK + (a - 1) * HALF_K
            counts.append(jnp.sum(self32[lo:lo + HALF_K], axis=0, keepdims=True))
        lo = PEER_TOPK + (HALF_K - 1) * HALF_K
        for a in range(HALF_K, PEER_TOPK):
            counts.append(self32[lo + a - HALF_K:lo + a - HALF_K + 1])
        cnt = jnp.zeros_like(s1)
        for a in range(PEER_TOPK):
            cnt = jnp.where(s1 == tops1[a], counts[a], cnt)
        rank_ref[h] = rank2.astype(BF)
        cnt_ref[h] = cnt
        e1_ref[h] = jnp.exp(s1 - m1) / z
        e2_ref[h] = jnp.exp(s2 - m2).astype(BF)
        return carry
    lax.fori_loop(0, PEER_HEADS, head_body, 0)


def _select_call(wq_t, keys, h_t, tn):
    d, t = h_t.shape
    shape = (PEER_HEADS, N_KEYS, t)
    ospec = pl.BlockSpec((PEER_HEADS, N_KEYS, tn), lambda i: (0, 0, i))
    return pl.pallas_call(
        _select_kernel,
        grid=(t // tn,),
        in_specs=[pl.BlockSpec(wq_t.shape, lambda i: (0, 0)),
                  pl.BlockSpec(keys.shape, lambda i: (0, 0, 0)),
                  pl.BlockSpec((d, tn), lambda i: (0, i))],
        out_specs=[ospec, ospec, ospec, ospec],
        out_shape=[jax.ShapeDtypeStruct(shape, BF), jax.ShapeDtypeStruct(shape, F32),
                   jax.ShapeDtypeStruct(shape, F32), jax.ShapeDtypeStruct(shape, BF)],
        scratch_shapes=[pltpu.VMEM((N_HP, N_KEYS, tn), F32),
                        pltpu.VMEM((2, PEER_TOPK, tn), F32)],
        compiler_params=_params(1),
        name="peer_select",
    )(wq_t, keys, h_t)


def _gelu(x):
    return 0.5 * x * (1.0 + lax.erf(x * math.sqrt(0.5)))


def _peer_kernel(u_ref, vt_ref, ht_ref, rank_ref, cnt_ref, e1_ref, e2_ref, o_ref,
                 a_scr, c_scr, w_scr, *, rows):
    e = pl.program_id(1)
    tn = ht_ref.shape[1]
    d = vt_ref.shape[0]
    half = rows // 2
    hrows = half * N_KEYS

    @pl.when(e == 0)
    def _():
        o_ref[...] = jnp.zeros_like(o_ref)

    def coef_row(r):
        i1 = e * rows + r
        w = None
        for h in range(PEER_HEADS):
            cnt = jnp.broadcast_to(cnt_ref[h, pl.ds(i1, 1), :], (BF16_ROWS, tn)).astype(BF)
            e1 = jnp.broadcast_to(e1_ref[h, pl.ds(i1, 1), :], (BF16_ROWS, tn)).astype(BF)
            contrib = jnp.where(rank_ref[h] < cnt[None], e2_ref[h] * e1[None],
                                jnp.zeros((), BF))
            w = contrib if w is None else w + contrib
        w_scr[...] = w
        act = _gelu(a_scr[r * N_KEYS:(r + 1) * N_KEYS, :]).astype(BF)
        c_scr[r * N_KEYS:(r + 1) * N_KEYS, :] = w_scr[...].reshape(N_KEYS, tn) * act

    ht = ht_ref[...]
    a_scr[0:hrows, :] = _dot(u_ref[0:hrows, :], ht)
    act_slices = half // 2
    srows = hrows // act_slices
    for j in range(act_slices):
        lo = hrows + j * srows
        a_scr[lo:lo + srows, :] = _dot(u_ref[lo:lo + srows, :], ht)
        for r in range(j * half // act_slices, (j + 1) * half // act_slices):
            coef_row(r)
    mrows = d // half
    c_a = c_scr[0:hrows, :]
    for j in range(half):
        o_ref[j * mrows:(j + 1) * mrows, :] += _dot(vt_ref[j * mrows:(j + 1) * mrows, 0:hrows], c_a)
        coef_row(half + j)
    o_ref[...] += _dot(vt_ref[:, hrows:], c_scr[hrows:, :])


def _peer_call(u, v_t, h_t, rank2, cnt, e1, e2, tn, te):
    d, t = h_t.shape
    rows = te // N_KEYS
    groups = N_KEYS // BF16_ROWS
    rank4 = rank2.reshape(PEER_HEADS, groups, BF16_ROWS, t)
    e24 = e2.reshape(PEER_HEADS, groups, BF16_ROWS, t)
    sel3 = pl.BlockSpec((PEER_HEADS, N_KEYS, tn), lambda i, e: (0, 0, i))
    sel4 = pl.BlockSpec((PEER_HEADS, groups, BF16_ROWS, tn), lambda i, e: (0, 0, 0, i))
    return pl.pallas_call(
        functools.partial(_peer_kernel, rows=rows),
        grid=(t // tn, N_EXPERTS // te),
        in_specs=[pl.BlockSpec((te, d), lambda i, e: (e, 0)),
                  pl.BlockSpec((d, te), lambda i, e: (0, e)),
                  pl.BlockSpec((d, tn), lambda i, e: (0, i)),
                  sel4, sel3, sel3, sel4],
        out_specs=pl.BlockSpec((d, tn), lambda i, e: (0, i)),
        out_shape=jax.ShapeDtypeStruct((d, t), F32),
        scratch_shapes=[pltpu.VMEM((te, tn), F32), pltpu.VMEM((te, tn), BF),
                        pltpu.VMEM((groups, BF16_ROWS, tn), BF)],
        compiler_params=_params(2),
        name="peer_dense",
    )(u, v_t, h_t, rank4, cnt, e1, e24)


def _final_kernel(yt_ref, x_ref, g2_ref, lg_ref, lb_ref, o_ref):
    z = DEEPNORM_ALPHA * x_ref[0] + g2_ref[0] * yt_ref[...].T
    o_ref[0] = _layer_norm_rows(z) * lg_ref[...] + lb_ref[...]


def _final_call(y_t, x1, g2, ln_g, ln_b, tm):
    b, s, d = x1.shape
    nb = s // tm
    return pl.pallas_call(
        _final_kernel,
        grid=(b, nb),
        in_specs=[pl.BlockSpec((d, tm), lambda i, j: (0, i * nb + j)),
                  pl.BlockSpec((1, tm, d), lambda i, j: (i, j, 0)),
                  pl.BlockSpec((1, 1, d), lambda i, j: (i, 0, 0)),
                  pl.BlockSpec((1, d), lambda i, j: (0, 0)),
                  pl.BlockSpec((1, d), lambda i, j: (0, 0))],
        out_specs=pl.BlockSpec((1, tm, d), lambda i, j: (i, j, 0)),
        out_shape=jax.ShapeDtypeStruct((b, s, d), F32),
        compiler_params=_params(2),
        name="final_deepnorm",
    )(y_t, x1, g2, ln_g, ln_b)


def _rope_rotation(w):
    pairs = w.reshape(w.shape[:-1] + (w.shape[-1] // 2, 2))
    return jnp.stack([-pairs[..., 1], pairs[..., 0]], axis=-1).reshape(w.shape)


def _pad_lanes(w):
    return jnp.pad(w, [(0, 0)] * (w.ndim - 1) + [(0, LANES - w.shape[-1])])


def _rope_tables(seq):
    rows = seq // GRID_W
    row = jnp.repeat(jnp.arange(rows, dtype=F32), GRID_W)
    col = jnp.tile(jnp.arange(GRID_W, dtype=F32), rows)
    half = QK_ROPE // 2
    inv = ROPE_THETA ** (-jnp.arange(0, half, 2, dtype=F32) / half)
    ang = jnp.concatenate([row[:, None] * inv, col[:, None] * inv], axis=-1)
    cos = _pad_lanes(jnp.repeat(jnp.cos(ang), 2, axis=-1))
    sin = _pad_lanes(jnp.repeat(jnp.sin(ang), 2, axis=-1))
    return cos, sin


def _dft_matrices(n, scale):
    k = np.arange(n, dtype=np.int64)
    ang = 2.0 * np.pi * ((k[:, None] * k[None, :]) % n).astype(np.float64) / n
    return np.cos(ang) * scale, np.sin(ang) * scale


def kernel(x, c, ctx, c_ctx, w_mod, b_mod, w_in, b_in, q_norm_g, w_uq, kv_norm_g, w_ukv,
           w_o_mla, w_fourier, w_out, ln1_g, ln1_b, peer_wq, peer_keys, peer_u, peer_v,
           ln2_g, ln2_b):
    B, S, D = x.shape
    T = B * S
    CT = ctx.shape[1]
    l = 0

    cmat = jnp.concatenate([c, c_ctx[None, :], jnp.zeros((8 - B - 1, D), F32)], axis=0)
    mod = _mod_call(cmat, w_mod[l], b_mod[l])
    mx = mod[:B].reshape(B, 1, 6, D)
    sh1, sc1, g1, sh2, sc2, g2 = [mx[:, :, i, :] for i in range(6)]
    mc = mod[B].reshape(1, 1, 6, D)
    sh1c, sc1c = mc[:, :, 0, :], mc[:, :, 1, :]

    wi, bi = w_in[l], b_in[l]
    w_kr, b_kr = wi[:, KV_LORA:KV_END], bi[KV_LORA:KV_END]
    q0 = KV_END
    f0 = KV_END + Q_LORA
    g0 = f0 + FOURIER_DIM
    w_lat = jnp.concatenate(
        [wi[:, :KV_LORA], _pad_lanes(w_kr), _pad_lanes(_rope_rotation(w_kr)), wi[:, q0:f0]],
        axis=1).astype(BF)
    b_lat = jnp.concatenate(
        [bi[:KV_LORA], _pad_lanes(b_kr), _pad_lanes(_rope_rotation(b_kr)), bi[q0:f0]])[None, :]
    w_f = wi[:, f0:g0].astype(BF)
    b_f = bi[f0:g0][None, :]
    w_g = wi[:, g0:].astype(BF)
    b_g = bi[g0:][None, :]
    wq3 = w_uq[l].reshape(Q_LORA, N_HEADS, QK_NOPE + QK_ROPE)
    wq_rope = wq3[:, :, QK_NOPE:]
    w_q = jnp.concatenate(
        [wq3[:, :, :QK_NOPE], _pad_lanes(wq_rope), _pad_lanes(_rope_rotation(wq_rope))],
        axis=-1).reshape(Q_LORA, N_HEADS * Q_HEAD_COLS).astype(BF)
    w_kv = w_ukv[l].astype(BF)
    gkv = kv_norm_g[l][None, :]
    gq = q_norm_g[l][None, :]

    cos, sin = _rope_tables(S)
    cos_c = _pad_lanes(jnp.ones((CT, QK_ROPE), F32))
    sin_c = jnp.zeros((CT, LANES), F32)
    dc_c, dc_s = _dft_matrices(FOURIER_GROUP_DIM, FOURIER_GROUP_DIM ** -0.5)
    dc = jnp.asarray(np.concatenate([dc_c, dc_s], axis=1), dtype=F32).astype(BF)
    ds_c, ds_s = _dft_matrices(S, S ** -0.5)
    cs = jnp.asarray(ds_c, dtype=F32).astype(BF)
    ss = jnp.asarray(ds_s, dtype=F32).astype(BF)

    hx = _ln_mod_call(x, sh1, sc1, 512)
    hc = _ln_mod_call(ctx, sh1c, sc1c, CT)
    ckv_x, kr_x, cq_x = _latent_call(hx, w_lat, b_lat, gkv, gq, cos, sin, 512)
    ckv_c, kr_c, _ = _latent_call(hc, w_lat, b_lat, gkv, gq, cos_c, sin_c, CT)
    lat = jnp.concatenate([ckv_c, ckv_x], axis=1)
    kr = jnp.concatenate([kr_c, kr_x], axis=1)
    k_all, v_all = _kv_up_call(lat, kr, w_kv, 256)
    q_all = _q_up_call(cq_x, w_q, cos, sin, 256)
    attn = _attn_call(q_all, k_all, v_all, 512)

    hx2d = hx.reshape(T, D)
    gc, gs = _fproj_call(hx2d, w_f, b_f, dc, 512)
    fm = _pos_dft_call(cs, ss, gc.reshape(B, S, FOURIER_DIM), gs.reshape(B, S, FOURIER_DIM),
                       512, 512)
    gates = _gate_call(hx2d, w_g, b_g, 512, 1024)
    merged = _merge_call(attn.reshape(T, D), fm.reshape(T, FOURIER_DIM),
                         w_o_mla[l].astype(BF), w_fourier[l].astype(BF), gates, 512, 512)
    x1, h2_t = _outproj_call(merged, w_out[l].astype(BF), x, g1, ln1_g[l][None, :],
                             ln1_b[l][None, :], sh2, sc2, 256)

    wq_t = peer_wq[l].T.astype(BF)
    keys = peer_keys[l].reshape(N_HP, N_KEYS, PEER_HALF).astype(BF)
    rank2, cnt, e1, e2 = _select_call(wq_t, keys, h2_t, 256)
    y_t = _peer_call(peer_u[l].astype(BF), peer_v[l].T.astype(BF), h2_t, rank2, cnt, e1, e2,
                     512, 1024)
    return _final_call(y_t, x1, g2, ln2_g[l][None, :], ln2_b[l][None, :], 256)
```

```python
import functools
import math

import numpy as np
import jax
import jax.numpy as jnp
from jax import lax
from jax.experimental import pallas as pl
from jax.experimental.pallas import tpu as pltpu

D_MODEL = 2048
GRID_W = 64
N_HEADS = 16
QK_NOPE = 128
QK_ROPE = 64
V_DIM = 128
Q_LORA = 512
KV_LORA = 512
ROPE_THETA = 10000.0
N_FOURIER_GROUPS = 4
FOURIER_GROUP_DIM = 256
FOURIER_DIM = N_FOURIER_GROUPS * FOURIER_GROUP_DIM
KV_END = KV_LORA + QK_ROPE
PEER_HEADS = 8
N_KEYS = 128
N_EXPERTS = N_KEYS * N_KEYS
PEER_HALF = 128
PEER_TOPK = 16
DEPTH = 1
DEEPNORM_ALPHA = (2.0 * DEPTH) ** 0.25
EPS = 1e-6

LANES = 128
QK_PAD = 2 * LANES
VMEM_LIMIT = 56 * 1024 * 1024

BF = jnp.bfloat16
F32 = jnp.float32


def _params(n_axes, vmem=VMEM_LIMIT):
    return pltpu.CompilerParams(
        dimension_semantics=("arbitrary",) * n_axes, vmem_limit_bytes=vmem)


def _dot(a, b):
    return jnp.dot(a, b, preferred_element_type=F32)


def _layer_norm_rows(x):
    mu = jnp.mean(x, axis=-1, keepdims=True)
    xc = x - mu
    var = jnp.mean(xc * xc, axis=-1, keepdims=True)
    return xc * lax.rsqrt(var + EPS)


def _mod_kernel(c_ref, w_ref, b_ref, o_ref):
    a = jax.nn.silu(c_ref[...]).astype(BF)
    o_ref[...] = _dot(a, w_ref[...].astype(BF)) + b_ref[...]


def _mod_call(cmat, w_mod, b_mod):
    n = w_mod.shape[1]
    tn = 1024
    return pl.pallas_call(
        _mod_kernel,
        grid=(n // tn,),
        in_specs=[pl.BlockSpec((8, D_MODEL), lambda j: (0, 0)),
                  pl.BlockSpec((D_MODEL, tn), lambda j: (0, j)),
                  pl.BlockSpec((1, tn), lambda j: (0, j))],
        out_specs=pl.BlockSpec((8, tn), lambda j: (0, j)),
        out_shape=jax.ShapeDtypeStruct((8, n), F32),
        compiler_params=_params(1),
        name="adaln_mod",
    )(cmat, w_mod, b_mod.reshape(1, n))


def _ln_mod_kernel(x_ref, sh_ref, sc_ref, o_ref):
    y = _layer_norm_rows(x_ref[0])
    o_ref[0] = (y * (1.0 + sc_ref[0]) + sh_ref[0]).astype(BF)


def _ln_mod_call(x, shift, scale, tm):
    b, s, d = x.shape
    bm = shift.shape[0]
    mod_map = (lambda i, j: (i, 0, 0)) if bm == b else (lambda i, j: (0, 0, 0))
    return pl.pallas_call(
        _ln_mod_kernel,
        grid=(b, s // tm),
        in_specs=[pl.BlockSpec((1, tm, d), lambda i, j: (i, j, 0)),
                  pl.BlockSpec((1, 1, d), mod_map),
                  pl.BlockSpec((1, 1, d), mod_map)],
        out_specs=pl.BlockSpec((1, tm, d), lambda i, j: (i, j, 0)),
        out_shape=jax.ShapeDtypeStruct((b, s, d), BF),
        compiler_params=_params(2),
        name="ln_modulate",
    )(x, shift, scale)


LAT_COLS = KV_LORA + 2 * LANES + Q_LORA


def _latent_kernel(h_ref, w_ref, b_ref, gkv_ref, gq_ref, cos_ref, sin_ref,
                   ckv_ref, kr_ref, cq_ref):
    acc = _dot(h_ref[0], w_ref[...]) + b_ref[...]
    ckv = acc[:, :KV_LORA]
    ka = acc[:, KV_LORA:KV_LORA + LANES]
    kb = acc[:, KV_LORA + LANES:KV_LORA + 2 * LANES]
    cq = acc[:, KV_LORA + 2 * LANES:]
    ckv_n = ckv * lax.rsqrt(jnp.mean(ckv * ckv, axis=-1, keepdims=True) + EPS)
    cq_n = cq * lax.rsqrt(jnp.mean(cq * cq, axis=-1, keepdims=True) + EPS)
    ckv_ref[0] = (ckv_n * gkv_ref[...]).astype(BF)
    cq_ref[0] = (cq_n * gq_ref[...]).astype(BF)
    kr_ref[0] = (ka * cos_ref[...] + kb * sin_ref[...]).astype(BF)


def _latent_call(h, w_lat, b_lat, gkv, gq, cos, sin, tm):
    b, s, d = h.shape
    row = lambda i, j: (i, j, 0)
    const = lambda i, j: (0, 0)
    return pl.pallas_call(
        _latent_kernel,
        grid=(b, s // tm),
        in_specs=[pl.BlockSpec((1, tm, d), row),
                  pl.BlockSpec((d, LAT_COLS), const),
                  pl.BlockSpec((1, LAT_COLS), const),
                  pl.BlockSpec((1, KV_LORA), const),
                  pl.BlockSpec((1, Q_LORA), const),
                  pl.BlockSpec((tm, LANES), lambda i, j: (j, 0)),
                  pl.BlockSpec((tm, LANES), lambda i, j: (j, 0))],
        out_specs=[pl.BlockSpec((1, tm, KV_LORA), row),
                   pl.BlockSpec((1, tm, LANES), row),
                   pl.BlockSpec((1, tm, Q_LORA), row)],
        out_shape=[jax.ShapeDtypeStruct((b, s, KV_LORA), BF),
                   jax.ShapeDtypeStruct((b, s, LANES), BF),
                   jax.ShapeDtypeStruct((b, s, Q_LORA), BF)],
        compiler_params=_params(2),
        name="latent_proj",
    )(h, w_lat, b_lat, gkv, gq, cos, sin)


def _fproj_kernel(h_ref, w_ref, b_ref, dc_ref, gc_ref, gs_ref):
    f = (_dot(h_ref[...], w_ref[...]) + b_ref[...]).astype(BF)
    for g in range(N_FOURIER_GROUPS):
        lo = g * FOURIER_GROUP_DIM
        r = _dot(f[:, lo:lo + FOURIER_GROUP_DIM], dc_ref[...])
        gc_ref[:, lo:lo + FOURIER_GROUP_DIM] = r[:, :FOURIER_GROUP_DIM].astype(BF)
        gs_ref[:, lo:lo + FOURIER_GROUP_DIM] = r[:, FOURIER_GROUP_DIM:].astype(BF)


def _fproj_call(h2d, w_f, b_f, dc, tm):
    t, d = h2d.shape
    const = lambda i: (0, 0)
    return pl.pallas_call(
        _fproj_kernel,
        grid=(t // tm,),
        in_specs=[pl.BlockSpec((tm, d), lambda i: (i, 0)),
                  pl.BlockSpec((d, FOURIER_DIM), const),
                  pl.BlockSpec((1, FOURIER_DIM), const),
                  pl.BlockSpec((FOURIER_GROUP_DIM, 2 * FOURIER_GROUP_DIM), const)],
        out_specs=[pl.BlockSpec((tm, FOURIER_DIM), lambda i: (i, 0)),
                   pl.BlockSpec((tm, FOURIER_DIM), lambda i: (i, 0))],
        out_shape=[jax.ShapeDtypeStruct((t, FOURIER_DIM), BF),
                   jax.ShapeDtypeStruct((t, FOURIER_DIM), BF)],
        compiler_params=_params(1),
        name="fourier_in_proj",
    )(h2d, w_f, b_f, dc)


def _gate_kernel(h_ref, w_ref, b_ref, o_ref):
    o_ref[...] = jax.nn.sigmoid(_dot(h_ref[...], w_ref[...]) + b_ref[...]).astype(BF)


def _gate_call(h2d, w_g, b_g, tm, tn):
    t, d = h2d.shape
    n = w_g.shape[1]
    return pl.pallas_call(
        _gate_kernel,
        grid=(n // tn, t // tm),
        in_specs=[pl.BlockSpec((tm, d), lambda j, i: (i, 0)),
                  pl.BlockSpec((d, tn), lambda j, i: (0, j)),
                  pl.BlockSpec((1, tn), lambda j, i: (0, j))],
        out_specs=pl.BlockSpec((tm, tn), lambda j, i: (i, j)),
        out_shape=jax.ShapeDtypeStruct((t, n), BF),
        compiler_params=_params(2),
        name="gate_proj",
    )(h2d, w_g, b_g)


def _kv_up_kernel(lat_ref, kr_ref, w_ref, k_ref, v_ref):
    lat = lat_ref[0]
    kr = kr_ref[0]
    for h in range(N_HEADS):
        lo = h * (QK_NOPE + V_DIM)
        kv = _dot(lat, w_ref[:, lo:lo + QK_NOPE + V_DIM])
        k_ref[0, h] = jnp.concatenate([kv[:, :QK_NOPE].astype(BF), kr], axis=-1)
        v_ref[0, h] = kv[:, QK_NOPE:].astype(BF)


def _kv_up_call(lat, kr, w_ukv, tm):
    b, t, _ = lat.shape
    return pl.pallas_call(
        _kv_up_kernel,
        grid=(b, t // tm),
        in_specs=[pl.BlockSpec((1, tm, KV_LORA), lambda i, j: (i, j, 0)),
                  pl.BlockSpec((1, tm, LANES), lambda i, j: (i, j, 0)),
                  pl.BlockSpec(w_ukv.shape, lambda i, j: (0, 0))],
        out_specs=[pl.BlockSpec((1, N_HEADS, tm, QK_PAD), lambda i, j: (i, 0, j, 0)),
                   pl.BlockSpec((1, N_HEADS, tm, V_DIM), lambda i, j: (i, 0, j, 0))],
        out_shape=[jax.ShapeDtypeStruct((b, N_HEADS, t, QK_PAD), BF),
                   jax.ShapeDtypeStruct((b, N_HEADS, t, V_DIM), BF)],
        compiler_params=_params(2),
        name="kv_up_proj",
    )(lat, kr, w_ukv)


Q_HEAD_COLS = 3 * LANES


def _q_up_kernel(cq_ref, w_ref, cos_ref, sin_ref, q_ref):
    cq = cq_ref[0]
    cos = cos_ref[...]
    sin = sin_ref[...]
    scale = (QK_NOPE + QK_ROPE) ** -0.5
    for h in range(N_HEADS):
        lo = h * Q_HEAD_COLS
        acc = _dot(cq, w_ref[:, lo:lo + Q_HEAD_COLS])
        qn = acc[:, :LANES]
        qr = acc[:, LANES:2 * LANES] * cos + acc[:, 2 * LANES:] * sin
        q_ref[0, h] = (jnp.concatenate([qn, qr], axis=-1) * scale).astype(BF)


def _q_up_call(cq, w_q, cos, sin, tm):
    b, s, _ = cq.shape
    return pl.pallas_call(
        _q_up_kernel,
        grid=(b, s // tm),
        in_specs=[pl.BlockSpec((1, tm, Q_LORA), lambda i, j: (i, j, 0)),
                  pl.BlockSpec(w_q.shape, lambda i, j: (0, 0)),
                  pl.BlockSpec((tm, LANES), lambda i, j: (j, 0)),
                  pl.BlockSpec((tm, LANES), lambda i, j: (j, 0))],
        out_specs=pl.BlockSpec((1, N_HEADS, tm, QK_PAD), lambda i, j: (i, 0, j, 0)),
        out_shape=jax.ShapeDtypeStruct((b, N_HEADS, s, QK_PAD), BF),
        compiler_params=_params(2),
        name="q_up_proj",
    )(cq, w_q, cos, sin)


ATTN_LAG = 1


def _attn_kernel(q_ref, k_ref, v_ref, o_ref, s_scr, m_scr, *, kc):
    n = pl.program_id(0)
    tq = q_ref.shape[2]
    t = k_ref.shape[2]

    @pl.when(n == 0)
    def _():
        s_scr[...] = jnp.zeros_like(s_scr)
        m_scr[...] = jnp.zeros_like(m_scr)

    def step(cur, prev):
        q = q_ref[0, 0]
        m_prev = m_scr[prev]
        mrun = None
        lrun = jnp.zeros((tq, LANES), F32)
        acc = jnp.zeros((tq, V_DIM), F32)
        for c in range(t // kc):
            ks = slice(c * kc, (c + 1) * kc)
            s_c = lax.dot_general(q, k_ref[0, 0, ks, :], (((1,), (1,)), ((), ())),
                                  preferred_element_type=F32)
            s_scr[cur, :, ks] = s_c
            pieces = []
            for j in range(kc // LANES):
                lanes = slice(j * LANES, (j + 1) * LANES)
                col = slice(c * kc + j * LANES, c * kc + (j + 1) * LANES)
                p_j = jnp.exp(s_scr[prev, :, col] - m_prev)
                lrun = lrun + p_j
                pieces.append(p_j.astype(BF))
                mrun = s_c[:, lanes] if mrun is None else jnp.maximum(mrun, s_c[:, lanes])
            acc = acc + _dot(jnp.concatenate(pieces, axis=-1), v_ref[0, 0, ks, :])
        o_ref[0] = (acc / jnp.sum(lrun, axis=-1, keepdims=True)).astype(BF)
        m_scr[cur] = jnp.broadcast_to(jnp.max(mrun, axis=-1, keepdims=True), (tq, LANES))

    @pl.when(n % 2 == 0)
    def _():
        step(0, 1)

    @pl.when(n % 2 == 1)
    def _():
        step(1, 0)


def _attn_call(q, k, v, tq, kc):
    b, h, s, _ = q.shape
    t = k.shape[2]
    nq = s // tq
    total = b * h * nq

    def block(n, lag):
        i = jnp.clip(n - lag, 0, total - 1)
        return i // (h * nq), (i // nq) % h, i % nq

    def q_map(n):
        bi, hi, qi = block(n, 0)
        return bi, hi, qi, 0

    def k_map(n):
        bi, hi, _ = block(n, 0)
        return bi, hi, 0, 0

    def v_map(n):
        bi, hi, _ = block(n, ATTN_LAG)
        return bi, hi, 0, 0

    def o_map(n):
        bi, hi, qi = block(n, ATTN_LAG)
        return bi, qi, hi

    return pl.pallas_call(
        functools.partial(_attn_kernel, kc=kc),
        grid=(total + ATTN_LAG,),
        in_specs=[pl.BlockSpec((1, 1, tq, QK_PAD), q_map),
                  pl.BlockSpec((1, 1, t, QK_PAD), k_map),
                  pl.BlockSpec((1, 1, t, V_DIM), v_map)],
        out_specs=pl.BlockSpec((1, tq, V_DIM), o_map),
        out_shape=jax.ShapeDtypeStruct((b, s, h * V_DIM), BF),
        scratch_shapes=[pltpu.VMEM((2, tq, t), F32), pltpu.VMEM((2, tq, LANES), F32)],
        compiler_params=_params(1),
        name="mla_attention",
    )(q, k, v)


def _pos_dft_kernel(c_ref, s_ref, gc_ref, gs_ref, o_ref):
    o_ref[0] = (_dot(c_ref[...], gc_ref[0]) - _dot(s_ref[...], gs_ref[0])).astype(BF)


def _pos_dft_call(cs, ss, gc, gs, tm, tn):
    b, s, n = gc.shape
    return pl.pallas_call(
        _pos_dft_kernel,
        grid=(b, n // tn, s // tm),
        in_specs=[pl.BlockSpec((tm, s), lambda i, j, m: (m, 0)),
                  pl.BlockSpec((tm, s), lambda i, j, m: (m, 0)),
                  pl.BlockSpec((1, s, tn), lambda i, j, m: (i, 0, j)),
                  pl.BlockSpec((1, s, tn), lambda i, j, m: (i, 0, j))],
        out_specs=pl.BlockSpec((1, tm, tn), lambda i, j, m: (i, m, j)),
        out_shape=jax.ShapeDtypeStruct((b, s, n), BF),
        compiler_params=_params(3),
        name="position_dft",
    )(cs, ss, gc, gs)


def _merge_kernel(a_ref, f_ref, wo_ref, wf_ref, ga_ref, gb_ref, o_ref):
    ya = _dot(a_ref[...], wo_ref[...])
    yb = _dot(f_ref[...], wf_ref[...])
    o_ref[...] = (ga_ref[...].astype(F32) * ya + gb_ref[...].astype(F32) * yb).astype(BF)


def _merge_call(attn, fm, w_o, w_f, gates, tm, tn):
    t, d = attn.shape
    nb = D_MODEL // tn
    return pl.pallas_call(
        _merge_kernel,
        grid=(nb, t // tm),
        in_specs=[pl.BlockSpec((tm, d), lambda j, i: (i, 0)),
                  pl.BlockSpec((tm, FOURIER_DIM), lambda j, i: (i, 0)),
                  pl.BlockSpec((d, tn), lambda j, i: (0, j)),
                  pl.BlockSpec((FOURIER_DIM, tn), lambda j, i: (0, j)),
                  pl.BlockSpec((tm, tn), lambda j, i: (i, j)),
                  pl.BlockSpec((tm, tn), lambda j, i: (i, j + nb))],
        out_specs=pl.BlockSpec((tm, tn), lambda j, i: (i, j)),
        out_shape=jax.ShapeDtypeStruct((t, D_MODEL), BF),
        compiler_params=_params(2),
        name="branch_merge",
    )(attn, fm, w_o, w_f, gates, gates)


def _outproj_kernel(m_ref, w_ref, x_ref, g1_ref, lg_ref, lb_ref, sh_ref, sc_ref,
                    x1_ref, ht_ref):
    y = _dot(m_ref[...], w_ref[...])
    z = DEEPNORM_ALPHA * x_ref[0] + g1_ref[0] * y
    x1 = _layer_norm_rows(z) * lg_ref[...] + lb_ref[...]
    x1_ref[0] = x1
    h2 = _layer_norm_rows(x1) * (1.0 + sc_ref[0]) + sh_ref[0]
    ht_ref[...] = h2.T.astype(BF)


def _outproj_call(merged, w_out, x, g1, ln_g, ln_b, sh2, sc2, tm):
    b, s, d = x.shape
    nb = s // tm
    bmap = lambda i, j: (i, 0, 0)
    const = lambda i, j: (0, 0)
    return pl.pallas_call(
        _outproj_kernel,
        grid=(b, nb),
        in_specs=[pl.BlockSpec((tm, d), lambda i, j: (i * nb + j, 0)),
                  pl.BlockSpec((d, d), const),
                  pl.BlockSpec((1, tm, d), lambda i, j: (i, j, 0)),
                  pl.BlockSpec((1, 1, d), bmap),
                  pl.BlockSpec((1, d), const),
                  pl.BlockSpec((1, d), const),
                  pl.BlockSpec((1, 1, d), bmap),
                  pl.BlockSpec((1, 1, d), bmap)],
        out_specs=[pl.BlockSpec((1, tm, d), lambda i, j: (i, j, 0)),
                   pl.BlockSpec((d, tm), lambda i, j: (0, i * nb + j))],
        out_shape=[jax.ShapeDtypeStruct((b, s, d), F32),
                   jax.ShapeDtypeStruct((d, b * s), BF)],
        compiler_params=_params(2),
        name="out_proj_deepnorm",
    )(merged, w_out, x, g1, ln_g, ln_b, sh2, sc2)


N_HP = 2 * PEER_HEADS
NOT_TOP = 127.0
HALF_K = PEER_TOPK // 2
BF16_ROWS = 16


def _top_values(s, k):
    tops = []
    for _ in range(k):
        m = jnp.max(s, axis=0, keepdims=True)
        tops.append(m)
        s = jnp.where(s == m, -jnp.inf, s)
    return tops


def _top_values_ranked(s, k):
    tops = []
    rank = jnp.full(s.shape, NOT_TOP, F32)
    for r in range(k):
        m = jnp.max(s, axis=0, keepdims=True)
        tops.append(m)
        hit = s == m
        rank = jnp.where(hit, float(r), rank)
        s = jnp.where(hit, -jnp.inf, s)
    return tops, rank


def _select_kernel(wq_ref, keys_ref, ht_ref, rank_ref, cnt_ref, e1_ref, e2_ref,
                   s_scr, top_scr):
    qt = _dot(wq_ref[...], ht_ref[...]).astype(BF)
    for hp in range(N_HP):
        s_scr[hp] = _dot(keys_ref[hp], qt[hp * PEER_HALF:(hp + 1) * PEER_HALF])

    def head_body(h, carry):
        s1 = s_scr[2 * h]
        s2 = s_scr[2 * h + 1]
        tops1 = _top_values(s1, PEER_TOPK + 1)
        tops2, rank2 = _top_values_ranked(s2, PEER_TOPK + 1)
        for r in range(PEER_TOPK):
            top_scr[0, r:r + 1, :] = tops1[r]
            top_scr[1, r:r + 1, :] = tops2[r]
        t1 = top_scr[0]
        t2 = top_scr[1]
        m1, m2 = tops1[0], tops2[0]
        cand = jnp.concatenate(
            [m1 + t2]
            + [tops1[a] + t2[:HALF_K] for a in range(1, HALF_K)]
            + [t1[HALF_K:] + m2], axis=0)
        best = _top_values(cand, PEER_TOPK + 1)
        outside = jnp.maximum(tops1[PEER_TOPK] + m2, m1 + tops2[PEER_TOPK])
        runner_up = jnp.maximum(best[PEER_TOPK], outside)
        tau = 0.5 * (best[PEER_TOPK - 1] + runner_up)
        sel = cand >= tau
        z = jnp.sum(jnp.where(sel, jnp.exp(cand - (m1 + m2)), 0.0), axis=0, keepdims=True)
        self32 = sel.astype(F32)
        counts = [jnp.sum(self32[:PEER_TOPK], axis=0, keepdims=True)]
        for a in range(1, HALF_K):
            lo = PEER_TOPK + (a - 1) * HALF_K
            counts.append(jnp.sum(self32[lo:lo + HALF_K], axis=0, keepdims=True))
        lo = PEER_TOPK + (HALF_K - 1) * HALF_K
        for a in range(HALF_K, PEER_TOPK):
            counts.append(self32[lo + a - HALF_K:lo + a - HALF_K + 1])
        cnt = jnp.zeros_like(s1)
        for a in range(PEER_TOPK):
            cnt = jnp.where(s1 == tops1[a], counts[a], cnt)
        rank_ref[h] = rank2.astype(BF)
        cnt_ref[h] = cnt
        e1_ref[h] = jnp.exp(s1 - m1) / z
        e2_ref[h] = jnp.exp(s2 - m2).astype(BF)
        return carry
    lax.fori_loop(0, PEER_HEADS, head_body, 0)


def _select_call(wq_t, keys, h_t, tn):
    d, t = h_t.shape
    shape = (PEER_HEADS, N_KEYS, t)
    ospec = pl.BlockSpec((PEER_HEADS, N_KEYS, tn), lambda i: (0, 0, i))
    return pl.pallas_call(
        _select_kernel,
        grid=(t // tn,),
        in_specs=[pl.BlockSpec(wq_t.shape, lambda i: (0, 0)),
                  pl.BlockSpec(keys.shape, lambda i: (0, 0, 0)),
                  pl.BlockSpec((d, tn), lambda i: (0, i))],
        out_specs=[ospec, ospec, ospec, ospec],
        out_shape=[jax.ShapeDtypeStruct(shape, BF), jax.ShapeDtypeStruct(shape, F32),
                   jax.ShapeDtypeStruct(shape, F32), jax.ShapeDtypeStruct(shape, BF)],
        scratch_shapes=[pltpu.VMEM((N_HP, N_KEYS, tn), F32),
                        pltpu.VMEM((2, PEER_TOPK, tn), F32)],
        compiler_params=_params(1),
        name="peer_select",
    )(wq_t, keys, h_t)


def _gelu(x):
    return 0.5 * x * (1.0 + lax.erf(x * math.sqrt(0.5)))


def _peer_kernel(u_ref, vt_ref, ht_ref, rank_ref, cnt_ref, e1_ref, e2_ref, o_ref,
                 a_scr, c_scr, *, rows):
    e = pl.program_id(1)
    tn = ht_ref.shape[1]
    d = vt_ref.shape[0]
    half = rows // 2
    hrows = half * N_KEYS

    @pl.when(e == 0)
    def _():
        o_ref[...] = jnp.zeros_like(o_ref)

    def coef_row(r):
        i1 = e * rows + r
        w = None
        for h in range(PEER_HEADS):
            cnt = jnp.broadcast_to(cnt_ref[h, pl.ds(i1, 1), :], (BF16_ROWS, tn)).astype(BF)
            e1 = jnp.broadcast_to(e1_ref[h, pl.ds(i1, 1), :], (BF16_ROWS, tn)).astype(BF)
            contrib = jnp.where(rank_ref[h] < cnt[None], e2_ref[h] * e1[None],
                                jnp.zeros((), BF))
            w = contrib if w is None else w + contrib
        act = _gelu(a_scr[r * N_KEYS:(r + 1) * N_KEYS, :]).astype(BF)
        c_scr[r * N_KEYS:(r + 1) * N_KEYS, :] = w.reshape(N_KEYS, tn) * act

    ht = ht_ref[...]
    a_scr[0:hrows, :] = _dot(u_ref[0:hrows, :], ht)
    act_slices = half // 2
    srows = hrows // act_slices
    for j in range(act_slices):
        lo = hrows + j * srows
        a_scr[lo:lo + srows, :] = _dot(u_ref[lo:lo + srows, :], ht)
        for r in range(j * half // act_slices, (j + 1) * half // act_slices):
            coef_row(r)
    mrows = d // half
    c_a = c_scr[0:hrows, :]
    for j in range(half):
        o_ref[j * mrows:(j + 1) * mrows, :] += _dot(vt_ref[j * mrows:(j + 1) * mrows, 0:hrows], c_a)
        coef_row(half + j)
    o_ref[...] += _dot(vt_ref[:, hrows:], c_scr[hrows:, :])


def _peer_call(u, v_t, h_t, rank2, cnt, e1, e2, tn, te):
    d, t = h_t.shape
    rows = te // N_KEYS
    groups = N_KEYS // BF16_ROWS
    rank4 = rank2.reshape(PEER_HEADS, groups, BF16_ROWS, t)
    e24 = e2.reshape(PEER_HEADS, groups, BF16_ROWS, t)
    sel3 = pl.BlockSpec((PEER_HEADS, N_KEYS, tn), lambda i, e: (0, 0, i))
    sel4 = pl.BlockSpec((PEER_HEADS, groups, BF16_ROWS, tn), lambda i, e: (0, 0, 0, i))
    return pl.pallas_call(
        functools.partial(_peer_kernel, rows=rows),
        grid=(t // tn, N_EXPERTS // te),
        in_specs=[pl.BlockSpec((te, d), lambda i, e: (e, 0)),
                  pl.BlockSpec((d, te), lambda i, e: (0, e)),
                  pl.BlockSpec((d, tn), lambda i, e: (0, i)),
                  sel4, sel3, sel3, sel4],
        out_specs=pl.BlockSpec((d, tn), lambda i, e: (0, i)),
        out_shape=jax.ShapeDtypeStruct((d, t), F32),
        scratch_shapes=[pltpu.VMEM((te, tn), F32), pltpu.VMEM((te, tn), BF)],
        compiler_params=_params(2),
        name="peer_dense",
    )(u, v_t, h_t, rank4, cnt, e1, e24)


def _final_kernel(yt_ref, x_ref, g2_ref, lg_ref, lb_ref, o_ref):
    z = DEEPNORM_ALPHA * x_ref[0] + g2_ref[0] * yt_ref[...].T
    o_ref[0] = _layer_norm_rows(z) * lg_ref[...] + lb_ref[...]


def _final_call(y_t, x1, g2, ln_g, ln_b, tm):
    b, s, d = x1.shape
    nb = s // tm
    return pl.pallas_call(
        _final_kernel,
        grid=(b, nb),
        in_specs=[pl.BlockSpec((d, tm), lambda i, j: (0, i * nb + j)),
                  pl.BlockSpec((1, tm, d), lambda i, j: (i, j, 0)),
                  pl.BlockSpec((1, 1, d), lambda i, j: (i, 0, 0)),
                  pl.BlockSpec((1, d), lambda i, j: (0, 0)),
                  pl.BlockSpec((1, d), lambda i, j: (0, 0))],
        out_specs=pl.BlockSpec((1, tm, d), lambda i, j: (i, j, 0)),
        out_shape=jax.ShapeDtypeStruct((b, s, d), F32),
        compiler_params=_params(2),
        name="final_deepnorm",
    )(y_t, x1, g2, ln_g, ln_b)


def _rope_rotation(w):
    pairs = w.reshape(w.shape[:-1] + (w.shape[-1] // 2, 2))
    return jnp.stack([-pairs[..., 1], pairs[..., 0]], axis=-1).reshape(w.shape)


def _pad_lanes(w):
    return jnp.pad(w, [(0, 0)] * (w.ndim - 1) + [(0, LANES - w.shape[-1])])


def _rope_tables(seq):
    rows = seq // GRID_W
    row = jnp.repeat(jnp.arange(rows, dtype=F32), GRID_W)
    col = jnp.tile(jnp.arange(GRID_W, dtype=F32), rows)
    half = QK_ROPE // 2
    inv = ROPE_THETA ** (-jnp.arange(0, half, 2, dtype=F32) / half)
    ang = jnp.concatenate([row[:, None] * inv, col[:, None] * inv], axis=-1)
    cos = _pad_lanes(jnp.repeat(jnp.cos(ang), 2, axis=-1))
    sin = _pad_lanes(jnp.repeat(jnp.sin(ang), 2, axis=-1))
    return cos, sin


def _dft_matrices(n, scale):
    k = np.arange(n, dtype=np.int64)
    ang = 2.0 * np.pi * ((k[:, None] * k[None, :]) % n).astype(np.float64) / n
    return np.cos(ang) * scale, np.sin(ang) * scale


def kernel(x, c, ctx, c_ctx, w_mod, b_mod, w_in, b_in, q_norm_g, w_uq, kv_norm_g, w_ukv,
           w_o_mla, w_fourier, w_out, ln1_g, ln1_b, peer_wq, peer_keys, peer_u, peer_v,
           ln2_g, ln2_b):
    B, S, D = x.shape
    T = B * S
    CT = ctx.shape[1]
    l = 0

    cmat = jnp.concatenate([c, c_ctx[None, :], jnp.zeros((8 - B - 1, D), F32)], axis=0)
    mod = _mod_call(cmat, w_mod[l], b_mod[l])
    mx = mod[:B].reshape(B, 1, 6, D)
    sh1, sc1, g1, sh2, sc2, g2 = [mx[:, :, i, :] for i in range(6)]
    mc = mod[B].reshape(1, 1, 6, D)
    sh1c, sc1c = mc[:, :, 0, :], mc[:, :, 1, :]

    wi, bi = w_in[l], b_in[l]
    w_kr, b_kr = wi[:, KV_LORA:KV_END], bi[KV_LORA:KV_END]
    q0 = KV_END
    f0 = KV_END + Q_LORA
    g0 = f0 + FOURIER_DIM
    w_lat = jnp.concatenate(
        [wi[:, :KV_LORA], _pad_lanes(w_kr), _pad_lanes(_rope_rotation(w_kr)), wi[:, q0:f0]],
        axis=1).astype(BF)
    b_lat = jnp.concatenate(
        [bi[:KV_LORA], _pad_lanes(b_kr), _pad_lanes(_rope_rotation(b_kr)), bi[q0:f0]])[None, :]
    w_f = wi[:, f0:g0].astype(BF)
    b_f = bi[f0:g0][None, :]
    w_g = wi[:, g0:].astype(BF)
    b_g = bi[g0:][None, :]
    wq3 = w_uq[l].reshape(Q_LORA, N_HEADS, QK_NOPE + QK_ROPE)
    wq_rope = wq3[:, :, QK_NOPE:]
    w_q = jnp.concatenate(
        [wq3[:, :, :QK_NOPE], _pad_lanes(wq_rope), _pad_lanes(_rope_rotation(wq_rope))],
        axis=-1).reshape(Q_LORA, N_HEADS * Q_HEAD_COLS).astype(BF)
    w_kv = w_ukv[l].astype(BF)
    gkv = kv_norm_g[l][None, :]
    gq = q_norm_g[l][None, :]

    cos, sin = _rope_tables(S)
    cos_c = _pad_lanes(jnp.ones((CT, QK_ROPE), F32))
    sin_c = jnp.zeros((CT, LANES), F32)
    dc_c, dc_s = _dft_matrices(FOURIER_GROUP_DIM, FOURIER_GROUP_DIM ** -0.5)
    dc = jnp.asarray(np.concatenate([dc_c, dc_s], axis=1), dtype=F32).astype(BF)
    ds_c, ds_s = _dft_matrices(S, S ** -0.5)
    cs = jnp.asarray(ds_c, dtype=F32).astype(BF)
    ss = jnp.asarray(ds_s, dtype=F32).astype(BF)

    hx = _ln_mod_call(x, sh1, sc1, 512)
    hc = _ln_mod_call(ctx, sh1c, sc1c, CT)
    ckv_x, kr_x, cq_x = _latent_call(hx, w_lat, b_lat, gkv, gq, cos, sin, 512)
    ckv_c, kr_c, _ = _latent_call(hc, w_lat, b_lat, gkv, gq, cos_c, sin_c, CT)
    lat = jnp.concatenate([ckv_c, ckv_x], axis=1)
    kr = jnp.concatenate([kr_c, kr_x], axis=1)
    k_all, v_all = _kv_up_call(lat, kr, w_kv, 256)
    q_all = _q_up_call(cq_x, w_q, cos, sin, 256)
    attn = _attn_call(q_all, k_all, v_all, 512, 256)

    hx2d = hx.reshape(T, D)
    gc, gs = _fproj_call(hx2d, w_f, b_f, dc, 512)
    fm = _pos_dft_call(cs, ss, gc.reshape(B, S, FOURIER_DIM), gs.reshape(B, S, FOURIER_DIM),
                       512, 512)
    gates = _gate_call(hx2d, w_g, b_g, 512, 1024)
    merged = _merge_call(attn.reshape(T, D), fm.reshape(T, FOURIER_DIM),
                         w_o_mla[l].astype(BF), w_fourier[l].astype(BF), gates, 512, 512)
    x1, h2_t = _outproj_call(merged, w_out[l].astype(BF), x, g1, ln1_g[l][None, :],
                             ln1_b[l][None, :], sh2, sc2, 256)

    wq_t = peer_wq[l].T.astype(BF)
    keys = peer_keys[l].reshape(N_HP, N_KEYS, PEER_HALF).astype(BF)
    rank2, cnt, e1, e2 = _select_call(wq_t, keys, h2_t, 256)
    y_t = _peer_call(peer_u[l].astype(BF), peer_v[l].T.astype(BF), h2_t, rank2, cnt, e1, e2,
                     512, 1024)
    return _final_call(y_t, x1, g2, ln2_g[l][None, :], ln2_b[l][None, :], 256)
```

```python
import functools
import math

import numpy as np
import jax
import jax.numpy as jnp
from jax import lax
from jax.experimental import pallas as pl
from jax.experimental.pallas import tpu as pltpu

D_MODEL = 2048
GRID_W = 64
N_HEADS = 16
QK_NOPE = 128
QK_ROPE = 64
V_DIM = 128
Q_LORA = 512
KV_LORA = 512
ROPE_THETA = 10000.0
N_FOURIER_GROUPS = 4
FOURIER_GROUP_DIM = 256
FOURIER_DIM = N_FOURIER_GROUPS * FOURIER_GROUP_DIM
KV_END = KV_LORA + QK_ROPE
PEER_HEADS = 8
N_KEYS = 128
N_EXPERTS = N_KEYS * N_KEYS
PEER_HALF = 128
PEER_TOPK = 16
DEPTH = 1
DEEPNORM_ALPHA = (2.0 * DEPTH) ** 0.25
EPS = 1e-6

LANES = 128
QK_PAD = 2 * LANES
VMEM_LIMIT = 56 * 1024 * 1024

BF = jnp.bfloat16
F32 = jnp.float32


def _params(n_axes, vmem=VMEM_LIMIT):
    return pltpu.CompilerParams(
        dimension_semantics=("arbitrary",) * n_axes, vmem_limit_bytes=vmem)


def _dot(a, b):
    return jnp.dot(a, b, preferred_element_type=F32)


def _layer_norm_rows(x):
    mu = jnp.mean(x, axis=-1, keepdims=True)
    xc = x - mu
    var = jnp.mean(xc * xc, axis=-1, keepdims=True)
    return xc * lax.rsqrt(var + EPS)


def _mod_kernel(c_ref, w_ref, b_ref, o_ref):
    a = jax.nn.silu(c_ref[...]).astype(BF)
    o_ref[...] = _dot(a, w_ref[...].astype(BF)) + b_ref[...]


def _mod_call(cmat, w_mod, b_mod):
    n = w_mod.shape[1]
    tn = 1024
    return pl.pallas_call(
        _mod_kernel,
        grid=(n // tn,),
        in_specs=[pl.BlockSpec((8, D_MODEL), lambda j: (0, 0)),
                  pl.BlockSpec((D_MODEL, tn), lambda j: (0, j)),
                  pl.BlockSpec((1, tn), lambda j: (0, j))],
        out_specs=pl.BlockSpec((8, tn), lambda j: (0, j)),
        out_shape=jax.ShapeDtypeStruct((8, n), F32),
        compiler_params=_params(1),
        name="adaln_mod",
    )(cmat, w_mod, b_mod.reshape(1, n))


def _ln_mod_kernel(x_ref, sh_ref, sc_ref, o_ref):
    y = _layer_norm_rows(x_ref[0])
    o_ref[0] = (y * (1.0 + sc_ref[0]) + sh_ref[0]).astype(BF)


def _ln_mod_call(x, shift, scale, tm):
    b, s, d = x.shape
    bm = shift.shape[0]
    mod_map = (lambda i, j: (i, 0, 0)) if bm == b else (lambda i, j: (0, 0, 0))
    return pl.pallas_call(
        _ln_mod_kernel,
        grid=(b, s // tm),
        in_specs=[pl.BlockSpec((1, tm, d), lambda i, j: (i, j, 0)),
                  pl.BlockSpec((1, 1, d), mod_map),
                  pl.BlockSpec((1, 1, d), mod_map)],
        out_specs=pl.BlockSpec((1, tm, d), lambda i, j: (i, j, 0)),
        out_shape=jax.ShapeDtypeStruct((b, s, d), BF),
        compiler_params=_params(2),
        name="ln_modulate",
    )(x, shift, scale)


LAT_COLS = KV_LORA + 2 * LANES + Q_LORA


def _latent_kernel(h_ref, w_ref, b_ref, gkv_ref, gq_ref, cos_ref, sin_ref,
                   ckv_ref, kr_ref, cq_ref):
    acc = _dot(h_ref[0], w_ref[...]) + b_ref[...]
    ckv = acc[:, :KV_LORA]
    ka = acc[:, KV_LORA:KV_LORA + LANES]
    kb = acc[:, KV_LORA + LANES:KV_LORA + 2 * LANES]
    cq = acc[:, KV_LORA + 2 * LANES:]
    ckv_n = ckv * lax.rsqrt(jnp.mean(ckv * ckv, axis=-1, keepdims=True) + EPS)
    cq_n = cq * lax.rsqrt(jnp.mean(cq * cq, axis=-1, keepdims=True) + EPS)
    ckv_ref[0] = (ckv_n * gkv_ref[...]).astype(BF)
    cq_ref[0] = (cq_n * gq_ref[...]).astype(BF)
    kr_ref[0] = (ka * cos_ref[...] + kb * sin_ref[...]).astype(BF)


def _latent_call(h, w_lat, b_lat, gkv, gq, cos, sin, tm):
    b, s, d = h.shape
    row = lambda i, j: (i, j, 0)
    const = lambda i, j: (0, 0)
    return pl.pallas_call(
        _latent_kernel,
        grid=(b, s // tm),
        in_specs=[pl.BlockSpec((1, tm, d), row),
                  pl.BlockSpec((d, LAT_COLS), const),
                  pl.BlockSpec((1, LAT_COLS), const),
                  pl.BlockSpec((1, KV_LORA), const),
                  pl.BlockSpec((1, Q_LORA), const),
                  pl.BlockSpec((tm, LANES), lambda i, j: (j, 0)),
                  pl.BlockSpec((tm, LANES), lambda i, j: (j, 0))],
        out_specs=[pl.BlockSpec((1, tm, KV_LORA), row),
                   pl.BlockSpec((1, tm, LANES), row),
                   pl.BlockSpec((1, tm, Q_LORA), row)],
        out_shape=[jax.ShapeDtypeStruct((b, s, KV_LORA), BF),
                   jax.ShapeDtypeStruct((b, s, LANES), BF),
                   jax.ShapeDtypeStruct((b, s, Q_LORA), BF)],
        compiler_params=_params(2),
        name="latent_proj",
    )(h, w_lat, b_lat, gkv, gq, cos, sin)


def _fproj_kernel(h_ref, w_ref, b_ref, dc_ref, gc_ref, gs_ref):
    f = (_dot(h_ref[...], w_ref[...]) + b_ref[...]).astype(BF)
    for g in range(N_FOURIER_GROUPS):
        lo = g * FOURIER_GROUP_DIM
        r = _dot(f[:, lo:lo + FOURIER_GROUP_DIM], dc_ref[...])
        gc_ref[:, lo:lo + FOURIER_GROUP_DIM] = r[:, :FOURIER_GROUP_DIM].astype(BF)
        gs_ref[:, lo:lo + FOURIER_GROUP_DIM] = r[:, FOURIER_GROUP_DIM:].astype(BF)


def _fproj_call(h2d, w_f, b_f, dc, tm):
    t, d = h2d.shape
    const = lambda i: (0, 0)
    return pl.pallas_call(
        _fproj_kernel,
        grid=(t // tm,),
        in_specs=[pl.BlockSpec((tm, d), lambda i: (i, 0)),
                  pl.BlockSpec((d, FOURIER_DIM), const),
                  pl.BlockSpec((1, FOURIER_DIM), const),
                  pl.BlockSpec((FOURIER_GROUP_DIM, 2 * FOURIER_GROUP_DIM), const)],
        out_specs=[pl.BlockSpec((tm, FOURIER_DIM), lambda i: (i, 0)),
                   pl.BlockSpec((tm, FOURIER_DIM), lambda i: (i, 0))],
        out_shape=[jax.ShapeDtypeStruct((t, FOURIER_DIM), BF),
                   jax.ShapeDtypeStruct((t, FOURIER_DIM), BF)],
        compiler_params=_params(1),
        name="fourier_in_proj",
    )(h2d, w_f, b_f, dc)


def _gate_kernel(h_ref, w_ref, b_ref, o_ref):
    o_ref[...] = jax.nn.sigmoid(_dot(h_ref[...], w_ref[...]) + b_ref[...]).astype(BF)


def _gate_call(h2d, w_g, b_g, tm, tn):
    t, d = h2d.shape
    n = w_g.shape[1]
    return pl.pallas_call(
        _gate_kernel,
        grid=(n // tn, t // tm),
        in_specs=[pl.BlockSpec((tm, d), lambda j, i: (i, 0)),
                  pl.BlockSpec((d, tn), lambda j, i: (0, j)),
                  pl.BlockSpec((1, tn), lambda j, i: (0, j))],
        out_specs=pl.BlockSpec((tm, tn), lambda j, i: (i, j)),
        out_shape=jax.ShapeDtypeStruct((t, n), BF),
        compiler_params=_params(2),
        name="gate_proj",
    )(h2d, w_g, b_g)


def _kv_up_kernel(lat_ref, kr_ref, w_ref, k_ref, v_ref):
    lat = lat_ref[0]
    kr = kr_ref[0]
    for h in range(N_HEADS):
        lo = h * (QK_NOPE + V_DIM)
        kv = _dot(lat, w_ref[:, lo:lo + QK_NOPE + V_DIM])
        k_ref[0, h] = jnp.concatenate([kv[:, :QK_NOPE].astype(BF), kr], axis=-1)
        v_ref[0, h] = kv[:, QK_NOPE:].astype(BF)


def _kv_up_call(lat, kr, w_ukv, tm):
    b, t, _ = lat.shape
    return pl.pallas_call(
        _kv_up_kernel,
        grid=(b, t // tm),
        in_specs=[pl.BlockSpec((1, tm, KV_LORA), lambda i, j: (i, j, 0)),
                  pl.BlockSpec((1, tm, LANES), lambda i, j: (i, j, 0)),
                  pl.BlockSpec(w_ukv.shape, lambda i, j: (0, 0))],
        out_specs=[pl.BlockSpec((1, N_HEADS, tm, QK_PAD), lambda i, j: (i, 0, j, 0)),
                   pl.BlockSpec((1, N_HEADS, tm, V_DIM), lambda i, j: (i, 0, j, 0))],
        out_shape=[jax.ShapeDtypeStruct((b, N_HEADS, t, QK_PAD), BF),
                   jax.ShapeDtypeStruct((b, N_HEADS, t, V_DIM), BF)],
        compiler_params=_params(2),
        name="kv_up_proj",
    )(lat, kr, w_ukv)


Q_HEAD_COLS = 3 * LANES


def _q_up_kernel(cq_ref, w_ref, cos_ref, sin_ref, q_ref):
    cq = cq_ref[0]
    cos = cos_ref[...]
    sin = sin_ref[...]
    scale = (QK_NOPE + QK_ROPE) ** -0.5
    for h in range(N_HEADS):
        lo = h * Q_HEAD_COLS
        acc = _dot(cq, w_ref[:, lo:lo + Q_HEAD_COLS])
        qn = acc[:, :LANES]
        qr = acc[:, LANES:2 * LANES] * cos + acc[:, 2 * LANES:] * sin
        q_ref[0, h] = (jnp.concatenate([qn, qr], axis=-1) * scale).astype(BF)


def _q_up_call(cq, w_q, cos, sin, tm):
    b, s, _ = cq.shape
    return pl.pallas_call(
        _q_up_kernel,
        grid=(b, s // tm),
        in_specs=[pl.BlockSpec((1, tm, Q_LORA), lambda i, j: (i, j, 0)),
                  pl.BlockSpec(w_q.shape, lambda i, j: (0, 0)),
                  pl.BlockSpec((tm, LANES), lambda i, j: (j, 0)),
                  pl.BlockSpec((tm, LANES), lambda i, j: (j, 0))],
        out_specs=pl.BlockSpec((1, N_HEADS, tm, QK_PAD), lambda i, j: (i, 0, j, 0)),
        out_shape=jax.ShapeDtypeStruct((b, N_HEADS, s, QK_PAD), BF),
        compiler_params=_params(2),
        name="q_up_proj",
    )(cq, w_q, cos, sin)


ATTN_LAG = 1


def _attn_kernel(q_ref, k_ref, v_ref, o_ref, s_scr, m_scr, *, kc):
    n = pl.program_id(0)
    tq = q_ref.shape[2]
    t = k_ref.shape[2]

    @pl.when(n == 0)
    def _():
        s_scr[...] = jnp.zeros_like(s_scr)
        m_scr[...] = jnp.zeros_like(m_scr)

    def step(cur, prev):
        q = q_ref[0, 0]
        m_prev = m_scr[prev]
        mrun = None
        lrun = jnp.zeros((tq, LANES), F32)
        acc = jnp.zeros((tq, V_DIM), F32)
        for c in range(t // kc):
            ks = slice(c * kc, (c + 1) * kc)
            s_c = lax.dot_general(q, k_ref[0, 0, ks, :], (((1,), (1,)), ((), ())),
                                  preferred_element_type=F32)
            s_scr[cur, :, ks] = s_c
            pieces = []
            for j in range(kc // LANES):
                lanes = slice(j * LANES, (j + 1) * LANES)
                col = slice(c * kc + j * LANES, c * kc + (j + 1) * LANES)
                p_j = jnp.exp(s_scr[prev, :, col] - m_prev)
                lrun = lrun + p_j
                pieces.append(p_j.astype(BF))
                mrun = s_c[:, lanes] if mrun is None else jnp.maximum(mrun, s_c[:, lanes])
            acc = acc + _dot(jnp.concatenate(pieces, axis=-1), v_ref[0, 0, ks, :])
        o_ref[0] = (acc / jnp.sum(lrun, axis=-1, keepdims=True)).astype(BF)
        m_scr[cur] = jnp.broadcast_to(jnp.max(mrun, axis=-1, keepdims=True), (tq, LANES))

    @pl.when(n % 2 == 0)
    def _():
        step(0, 1)

    @pl.when(n % 2 == 1)
    def _():
        step(1, 0)


def _attn_call(q, k, v, tq, kc):
    b, h, s, _ = q.shape
    t = k.shape[2]
    nq = s // tq
    total = b * h * nq

    def block(n, lag):
        i = jnp.clip(n - lag, 0, total - 1)
        return i // (h * nq), (i // nq) % h, i % nq

    def q_map(n):
        bi, hi, qi = block(n, 0)
        return bi, hi, qi, 0

    def k_map(n):
        bi, hi, _ = block(n, 0)
        return bi, hi, 0, 0

    def v_map(n):
        bi, hi, _ = block(n, ATTN_LAG)
        return bi, hi, 0, 0

    def o_map(n):
        bi, hi, qi = block(n, ATTN_LAG)
        return bi, qi, hi

    return pl.pallas_call(
        functools.partial(_attn_kernel, kc=kc),
        grid=(total + ATTN_LAG,),
        in_specs=[pl.BlockSpec((1, 1, tq, QK_PAD), q_map),
                  pl.BlockSpec((1, 1, t, QK_PAD), k_map),
                  pl.BlockSpec((1, 1, t, V_DIM), v_map)],
        out_specs=pl.BlockSpec((1, tq, V_DIM), o_map),
        out_shape=jax.ShapeDtypeStruct((b, s, h * V_DIM), BF),
        scratch_shapes=[pltpu.VMEM((2, tq, t), F32), pltpu.VMEM((2, tq, LANES), F32)],
        compiler_params=_params(1),
        name="mla_attention",
    )(q, k, v)


def _pos_dft_kernel(c_ref, s_ref, gc_ref, gs_ref, o_ref):
    o_ref[0] = (_dot(c_ref[...], gc_ref[0]) - _dot(s_ref[...], gs_ref[0])).astype(BF)


def _pos_dft_call(cs, ss, gc, gs, tm, tn):
    b, s, n = gc.shape
    return pl.pallas_call(
        _pos_dft_kernel,
        grid=(b, n // tn, s // tm),
        in_specs=[pl.BlockSpec((tm, s), lambda i, j, m: (m, 0)),
                  pl.BlockSpec((tm, s), lambda i, j, m: (m, 0)),
                  pl.BlockSpec((1, s, tn), lambda i, j, m: (i, 0, j)),
                  pl.BlockSpec((1, s, tn), lambda i, j, m: (i, 0, j))],
        out_specs=pl.BlockSpec((1, tm, tn), lambda i, j, m: (i, m, j)),
        out_shape=jax.ShapeDtypeStruct((b, s, n), BF),
        compiler_params=_params(3),
        name="position_dft",
    )(cs, ss, gc, gs)


def _merge_kernel(a_ref, f_ref, wo_ref, wf_ref, ga_ref, gb_ref, o_ref):
    ya = _dot(a_ref[...], wo_ref[...])
    yb = _dot(f_ref[...], wf_ref[...])
    o_ref[...] = (ga_ref[...].astype(F32) * ya + gb_ref[...].astype(F32) * yb).astype(BF)


def _merge_call(attn, fm, w_o, w_f, gates, tm, tn):
    t, d = attn.shape
    nb = D_MODEL // tn
    return pl.pallas_call(
        _merge_kernel,
        grid=(nb, t // tm),
        in_specs=[pl.BlockSpec((tm, d), lambda j, i: (i, 0)),
                  pl.BlockSpec((tm, FOURIER_DIM), lambda j, i: (i, 0)),
                  pl.BlockSpec((d, tn), lambda j, i: (0, j)),
                  pl.BlockSpec((FOURIER_DIM, tn), lambda j, i: (0, j)),
                  pl.BlockSpec((tm, tn), lambda j, i: (i, j)),
                  pl.BlockSpec((tm, tn), lambda j, i: (i, j + nb))],
        out_specs=pl.BlockSpec((tm, tn), lambda j, i: (i, j)),
        out_shape=jax.ShapeDtypeStruct((t, D_MODEL), BF),
        compiler_params=_params(2),
        name="branch_merge",
    )(attn, fm, w_o, w_f, gates, gates)


def _outproj_kernel(m_ref, w_ref, x_ref, g1_ref, lg_ref, lb_ref, sh_ref, sc_ref,
                    x1_ref, h_ref, ht_ref):
    y = _dot(m_ref[...], w_ref[...])
    z = DEEPNORM_ALPHA * x_ref[0] + g1_ref[0] * y
    x1 = _layer_norm_rows(z) * lg_ref[...] + lb_ref[...]
    x1_ref[0] = x1
    h2 = _layer_norm_rows(x1) * (1.0 + sc_ref[0]) + sh_ref[0]
    h_ref[...] = h2.astype(BF)
    ht_ref[...] = h2.T.astype(BF)


def _outproj_call(merged, w_out, x, g1, ln_g, ln_b, sh2, sc2, tm):
    b, s, d = x.shape
    nb = s // tm
    bmap = lambda i, j: (i, 0, 0)
    const = lambda i, j: (0, 0)
    return pl.pallas_call(
        _outproj_kernel,
        grid=(b, nb),
        in_specs=[pl.BlockSpec((tm, d), lambda i, j: (i * nb + j, 0)),
                  pl.BlockSpec((d, d), const),
                  pl.BlockSpec((1, tm, d), lambda i, j: (i, j, 0)),
                  pl.BlockSpec((1, 1, d), bmap),
                  pl.BlockSpec((1, d), const),
                  pl.BlockSpec((1, d), const),
                  pl.BlockSpec((1, 1, d), bmap),
                  pl.BlockSpec((1, 1, d), bmap)],
        out_specs=[pl.BlockSpec((1, tm, d), lambda i, j: (i, j, 0)),
                   pl.BlockSpec((tm, d), lambda i, j: (i * nb + j, 0)),
                   pl.BlockSpec((d, tm), lambda i, j: (0, i * nb + j))],
        out_shape=[jax.ShapeDtypeStruct((b, s, d), F32),
                   jax.ShapeDtypeStruct((b * s, d), BF),
                   jax.ShapeDtypeStruct((d, b * s), BF)],
        compiler_params=_params(2),
        name="out_proj_deepnorm",
    )(merged, w_out, x, g1, ln_g, ln_b, sh2, sc2)


N_HP = 2 * PEER_HEADS
NOT_TOP = 127.0
HALF_K = PEER_TOPK // 2
BF16_ROWS = 16


def _top_values(s, k):
    tops = []
    for _ in range(k):
        m = jnp.max(s, axis=0, keepdims=True)
        tops.append(m)
        s = jnp.where(s == m, -jnp.inf, s)
    return tops


def _top_values_ranked(s, k):
    tops = []
    rank = jnp.full(s.shape, NOT_TOP, F32)
    for r in range(k):
        m = jnp.max(s, axis=0, keepdims=True)
        tops.append(m)
        hit = s == m
        rank = jnp.where(hit, float(r), rank)
        s = jnp.where(hit, -jnp.inf, s)
    return tops, rank


def _select_kernel(wq_ref, keys_ref, h_ref, rank_ref, cnt_ref, e1_ref, e2_ref,
                   s_scr, top_scr):
    q = _dot(h_ref[...], wq_ref[...]).astype(BF)
    for hp in range(N_HP):
        s_scr[hp] = lax.dot_general(keys_ref[hp], q[:, hp * PEER_HALF:(hp + 1) * PEER_HALF],
                                    (((1,), (1,)), ((), ())), preferred_element_type=F32)

    def head_body(h, carry):
        s1 = s_scr[2 * h]
        s2 = s_scr[2 * h + 1]
        tops1 = _top_values(s1, PEER_TOPK + 1)
        tops2, rank2 = _top_values_ranked(s2, PEER_TOPK + 1)
        for r in range(PEER_TOPK):
            top_scr[0, r:r + 1, :] = tops1[r]
            top_scr[1, r:r + 1, :] = tops2[r]
        t1 = top_scr[0]
        t2 = top_scr[1]
        m1, m2 = tops1[0], tops2[0]
        cand = jnp.concatenate(
            [m1 + t2]
            + [tops1[a] + t2[:HALF_K] for a in range(1, HALF_K)]
            + [t1[HALF_K:] + m2], axis=0)
        best = _top_values(cand, PEER_TOPK + 1)
        outside = jnp.maximum(tops1[PEER_TOPK] + m2, m1 + tops2[PEER_TOPK])
        runner_up = jnp.maximum(best[PEER_TOPK], outside)
        tau = 0.5 * (best[PEER_TOPK - 1] + runner_up)
        sel = cand >= tau
        z = jnp.sum(jnp.where(sel, jnp.exp(cand - (m1 + m2)), 0.0), axis=0, keepdims=True)
        self32 = sel.astype(F32)
        counts = [jnp.sum(self32[:PEER_TOPK], axis=0, keepdims=True)]
        for a in range(1, HALF_K):
            lo = PEER_TOPK + (a - 1) * HALF_K
            counts.append(jnp.sum(self32[lo:lo + HALF_K], axis=0, keepdims=True))
        lo = PEER_TOPK + (HALF_K - 1) * HALF_K
        for a in range(HALF_K, PEER_TOPK):
            counts.append(self32[lo + a - HALF_K:lo + a - HALF_K + 1])
        cnt = jnp.zeros_like(s1)
        for a in range(PEER_TOPK):
            cnt = jnp.where(s1 == tops1[a], counts[a], cnt)
        rank_ref[h] = rank2.astype(BF)
        cnt_ref[h] = cnt
        e1_ref[h] = jnp.exp(s1 - m1) / z
        e2_ref[h] = jnp.exp(s2 - m2).astype(BF)
        return carry
    lax.fori_loop(0, PEER_HEADS, head_body, 0)


def _select_call(wq, keys, h, tn):
    t, d = h.shape
    shape = (PEER_HEADS, N_KEYS, t)
    ospec = pl.BlockSpec((PEER_HEADS, N_KEYS, tn), lambda i: (0, 0, i))
    return pl.pallas_call(
        _select_kernel,
        grid=(t // tn,),
        in_specs=[pl.BlockSpec(wq.shape, lambda i: (0, 0)),
                  pl.BlockSpec(keys.shape, lambda i: (0, 0, 0)),
                  pl.BlockSpec((tn, d), lambda i: (i, 0))],
        out_specs=[ospec, ospec, ospec, ospec],
        out_shape=[jax.ShapeDtypeStruct(shape, BF), jax.ShapeDtypeStruct(shape, F32),
                   jax.ShapeDtypeStruct(shape, F32), jax.ShapeDtypeStruct(shape, BF)],
        scratch_shapes=[pltpu.VMEM((N_HP, N_KEYS, tn), F32),
                        pltpu.VMEM((2, PEER_TOPK, tn), F32)],
        compiler_params=_params(1),
        name="peer_select",
    )(wq, keys, h)


def _gelu(x):
    return 0.5 * x * (1.0 + lax.erf(x * math.sqrt(0.5)))


def _peer_kernel(u_ref, vt_ref, ht_ref, rank_ref, cnt_ref, e1_ref, e2_ref, o_ref,
                 a_scr, c_scr, *, rows, act_slices, mix_slices):
    e = pl.program_id(1)
    tn = ht_ref.shape[1]
    d = vt_ref.shape[0]
    half = rows // 2
    hrows = half * N_KEYS

    @pl.when(e == 0)
    def _():
        o_ref[...] = jnp.zeros_like(o_ref)

    def coef_row(r):
        i1 = e * rows + r
        w = None
        for h in range(PEER_HEADS):
            cnt = jnp.broadcast_to(cnt_ref[h, pl.ds(i1, 1), :], (BF16_ROWS, tn)).astype(BF)
            e1 = jnp.broadcast_to(e1_ref[h, pl.ds(i1, 1), :], (BF16_ROWS, tn)).astype(BF)
            contrib = jnp.where(rank_ref[h] < cnt[None], e2_ref[h] * e1[None],
                                jnp.zeros((), BF))
            w = contrib if w is None else w + contrib
        act = _gelu(a_scr[r * N_KEYS:(r + 1) * N_KEYS, :]).astype(BF)
        c_scr[r * N_KEYS:(r + 1) * N_KEYS, :] = w.reshape(N_KEYS, tn) * act

    ht = ht_ref[...]
    a_scr[0:hrows, :] = _dot(u_ref[0:hrows, :], ht)
    srows = hrows // act_slices
    for j in range(act_slices):
        lo = hrows + j * srows
        a_scr[lo:lo + srows, :] = _dot(u_ref[lo:lo + srows, :], ht)
        for r in range(j * half // act_slices, (j + 1) * half // act_slices):
            coef_row(r)
    mrows = d // mix_slices
    c_a = c_scr[0:hrows, :]
    for j in range(mix_slices):
        o_ref[j * mrows:(j + 1) * mrows, :] += _dot(vt_ref[j * mrows:(j + 1) * mrows, 0:hrows], c_a)
        for r in range(j * half // mix_slices, (j + 1) * half // mix_slices):
            coef_row(half + r)
    o_ref[...] += _dot(vt_ref[:, hrows:], c_scr[hrows:, :])


def _peer_call(u, v_t, h_t, rank2, cnt, e1, e2, tn, te, act_slices, mix_slices):
    d, t = h_t.shape
    rows = te // N_KEYS
    groups = N_KEYS // BF16_ROWS
    rank4 = rank2.reshape(PEER_HEADS, groups, BF16_ROWS, t)
    e24 = e2.reshape(PEER_HEADS, groups, BF16_ROWS, t)
    sel3 = pl.BlockSpec((PEER_HEADS, N_KEYS, tn), lambda i, e: (0, 0, i))
    sel4 = pl.BlockSpec((PEER_HEADS, groups, BF16_ROWS, tn), lambda i, e: (0, 0, 0, i))
    return pl.pallas_call(
        functools.partial(_peer_kernel, rows=rows, act_slices=act_slices,
                          mix_slices=mix_slices),
        grid=(t // tn, N_EXPERTS // te),
        in_specs=[pl.BlockSpec((te, d), lambda i, e: (e, 0)),
                  pl.BlockSpec((d, te), lambda i, e: (0, e)),
                  pl.BlockSpec((d, tn), lambda i, e: (0, i)),
                  sel4, sel3, sel3, sel4],
        out_specs=pl.BlockSpec((d, tn), lambda i, e: (0, i)),
        out_shape=jax.ShapeDtypeStruct((d, t), F32),
        scratch_shapes=[pltpu.VMEM((te, tn), F32), pltpu.VMEM((te, tn), BF)],
        compiler_params=_params(2),
        name="peer_dense",
    )(u, v_t, h_t, rank4, cnt, e1, e24)


def _final_kernel(yt_ref, x_ref, g2_ref, lg_ref, lb_ref, o_ref):
    z = DEEPNORM_ALPHA * x_ref[0] + g2_ref[0] * yt_ref[...].T
    o_ref[0] = _layer_norm_rows(z) * lg_ref[...] + lb_ref[...]


def _final_call(y_t, x1, g2, ln_g, ln_b, tm):
    b, s, d = x1.shape
    nb = s // tm
    return pl.pallas_call(
        _final_kernel,
        grid=(b, nb),
        in_specs=[pl.BlockSpec((d, tm), lambda i, j: (0, i * nb + j)),
                  pl.BlockSpec((1, tm, d), lambda i, j: (i, j, 0)),
                  pl.BlockSpec((1, 1, d), lambda i, j: (i, 0, 0)),
                  pl.BlockSpec((1, d), lambda i, j: (0, 0)),
                  pl.BlockSpec((1, d), lambda i, j: (0, 0))],
        out_specs=pl.BlockSpec((1, tm, d), lambda i, j: (i, j, 0)),
        out_shape=jax.ShapeDtypeStruct((b, s, d), F32),
        compiler_params=_params(2),
        name="final_deepnorm",
    )(y_t, x1, g2, ln_g, ln_b)


def _rope_rotation(w):
    pairs = w.reshape(w.shape[:-1] + (w.shape[-1] // 2, 2))
    return jnp.stack([-pairs[..., 1], pairs[..., 0]], axis=-1).reshape(w.shape)


def _pad_lanes(w):
    return jnp.pad(w, [(0, 0)] * (w.ndim - 1) + [(0, LANES - w.shape[-1])])


def _rope_tables(seq):
    rows = seq // GRID_W
    row = jnp.repeat(jnp.arange(rows, dtype=F32), GRID_W)
    col = jnp.tile(jnp.arange(GRID_W, dtype=F32), rows)
    half = QK_ROPE // 2
    inv = ROPE_THETA ** (-jnp.arange(0, half, 2, dtype=F32) / half)
    ang = jnp.concatenate([row[:, None] * inv, col[:, None] * inv], axis=-1)
    cos = _pad_lanes(jnp.repeat(jnp.cos(ang), 2, axis=-1))
    sin = _pad_lanes(jnp.repeat(jnp.sin(ang), 2, axis=-1))
    return cos, sin


def _dft_matrices(n, scale):
    k = np.arange(n, dtype=np.int64)
    ang = 2.0 * np.pi * ((k[:, None] * k[None, :]) % n).astype(np.float64) / n
    return np.cos(ang) * scale, np.sin(ang) * scale


def kernel(x, c, ctx, c_ctx, w_mod, b_mod, w_in, b_in, q_norm_g, w_uq, kv_norm_g, w_ukv,
           w_o_mla, w_fourier, w_out, ln1_g, ln1_b, peer_wq, peer_keys, peer_u, peer_v,
           ln2_g, ln2_b):
    B, S, D = x.shape
    T = B * S
    CT = ctx.shape[1]
    l = 0

    cmat = jnp.concatenate([c, c_ctx[None, :], jnp.zeros((8 - B - 1, D), F32)], axis=0)
    mod = _mod_call(cmat, w_mod[l], b_mod[l])
    mx = mod[:B].reshape(B, 1, 6, D)
    sh1, sc1, g1, sh2, sc2, g2 = [mx[:, :, i, :] for i in range(6)]
    mc = mod[B].reshape(1, 1, 6, D)
    sh1c, sc1c = mc[:, :, 0, :], mc[:, :, 1, :]

    wi, bi = w_in[l], b_in[l]
    w_kr, b_kr = wi[:, KV_LORA:KV_END], bi[KV_LORA:KV_END]
    q0 = KV_END
    f0 = KV_END + Q_LORA
    g0 = f0 + FOURIER_DIM
    w_lat = jnp.concatenate(
        [wi[:, :KV_LORA], _pad_lanes(w_kr), _pad_lanes(_rope_rotation(w_kr)), wi[:, q0:f0]],
        axis=1).astype(BF)
    b_lat = jnp.concatenate(
        [bi[:KV_LORA], _pad_lanes(b_kr), _pad_lanes(_rope_rotation(b_kr)), bi[q0:f0]])[None, :]
    w_f = wi[:, f0:g0].astype(BF)
    b_f = bi[f0:g0][None, :]
    w_g = wi[:, g0:].astype(BF)
    b_g = bi[g0:][None, :]
    wq3 = w_uq[l].reshape(Q_LORA, N_HEADS, QK_NOPE + QK_ROPE)
    wq_rope = wq3[:, :, QK_NOPE:]
    w_q = jnp.concatenate(
        [wq3[:, :, :QK_NOPE], _pad_lanes(wq_rope), _pad_lanes(_rope_rotation(wq_rope))],
        axis=-1).reshape(Q_LORA, N_HEADS * Q_HEAD_COLS).astype(BF)
    w_kv = w_ukv[l].astype(BF)
    gkv = kv_norm_g[l][None, :]
    gq = q_norm_g[l][None, :]

    cos, sin = _rope_tables(S)
    cos_c = _pad_lanes(jnp.ones((CT, QK_ROPE), F32))
    sin_c = jnp.zeros((CT, LANES), F32)
    dc_c, dc_s = _dft_matrices(FOURIER_GROUP_DIM, FOURIER_GROUP_DIM ** -0.5)
    dc = jnp.asarray(np.concatenate([dc_c, dc_s], axis=1), dtype=F32).astype(BF)
    ds_c, ds_s = _dft_matrices(S, S ** -0.5)
    cs = jnp.asarray(ds_c, dtype=F32).astype(BF)
    ss = jnp.asarray(ds_s, dtype=F32).astype(BF)

    hx = _ln_mod_call(x, sh1, sc1, 512)
    hc = _ln_mod_call(ctx, sh1c, sc1c, CT)
    ckv_x, kr_x, cq_x = _latent_call(hx, w_lat, b_lat, gkv, gq, cos, sin, 512)
    ckv_c, kr_c, _ = _latent_call(hc, w_lat, b_lat, gkv, gq, cos_c, sin_c, CT)
    lat = jnp.concatenate([ckv_c, ckv_x], axis=1)
    kr = jnp.concatenate([kr_c, kr_x], axis=1)
    k_all, v_all = _kv_up_call(lat, kr, w_kv, 768)
    q_all = _q_up_call(cq_x, w_q, cos, sin, 512)
    attn = _attn_call(q_all, k_all, v_all, 1024, 256)

    hx2d = hx.reshape(T, D)
    gc, gs = _fproj_call(hx2d, w_f, b_f, dc, 512)
    fm = _pos_dft_call(cs, ss, gc.reshape(B, S, FOURIER_DIM), gs.reshape(B, S, FOURIER_DIM),
                       512, 512)
    gates = _gate_call(hx2d, w_g, b_g, 512, 1024)
    merged = _merge_call(attn.reshape(T, D), fm.reshape(T, FOURIER_DIM),
                         w_o_mla[l].astype(BF), w_fourier[l].astype(BF), gates, 512, 512)
    x1, h2, h2_t = _outproj_call(merged, w_out[l].astype(BF), x, g1, ln1_g[l][None, :],
                                 ln1_b[l][None, :], sh2, sc2, 512)

    keys = peer_keys[l].reshape(N_HP, N_KEYS, PEER_HALF).astype(BF)
    rank2, cnt, e1, e2 = _select_call(peer_wq[l].astype(BF), keys, h2, 512)
    y_t = _peer_call(peer_u[l].astype(BF), peer_v[l].T.astype(BF), h2_t, rank2, cnt, e1, e2,
                     512, 1024, 2, 4)
    return _final_call(y_t, x1, g2, ln2_g[l][None, :], ln2_b[l][None, :], 256)
```

```python
import functools
import math

import numpy as np
import jax
import jax.numpy as jnp
from jax import lax
from jax.experimental import pallas as pl
from jax.experimental.pallas import tpu as pltpu

D_MODEL = 2048
GRID_W = 64
N_HEADS = 16
QK_NOPE = 128
QK_ROPE = 64
V_DIM = 128
Q_LORA = 512
KV_LORA = 512
ROPE_THETA = 10000.0
N_FOURIER_GROUPS = 4
FOURIER_GROUP_DIM = 256
FOURIER_DIM = N_FOURIER_GROUPS * FOURIER_GROUP_DIM
KV_END = KV_LORA + QK_ROPE
PEER_HEADS = 8
N_KEYS = 128
N_EXPERTS = N_KEYS * N_KEYS
PEER_HALF = 128
PEER_TOPK = 16
DEPTH = 1
DEEPNORM_ALPHA = (2.0 * DEPTH) ** 0.25
EPS = 1e-6

LANES = 128
QK_PAD = 2 * LANES
VMEM_LIMIT = 56 * 1024 * 1024

BF = jnp.bfloat16
F32 = jnp.float32


def _params(n_axes, vmem=VMEM_LIMIT):
    return pltpu.CompilerParams(
        dimension_semantics=("arbitrary",) * n_axes, vmem_limit_bytes=vmem)


def _dot(a, b):
    return jnp.dot(a, b, preferred_element_type=F32)


def _layer_norm_rows(x):
    mu = jnp.mean(x, axis=-1, keepdims=True)
    xc = x - mu
    var = jnp.mean(xc * xc, axis=-1, keepdims=True)
    return xc * lax.rsqrt(var + EPS)


def _mod_kernel(c_ref, w_ref, b_ref, o_ref):
    a = jax.nn.silu(c_ref[...]).astype(BF)
    o_ref[...] = _dot(a, w_ref[...].astype(BF)) + b_ref[...]


def _mod_call(cmat, w_mod, b_mod):
    n = w_mod.shape[1]
    tn = 1024
    return pl.pallas_call(
        _mod_kernel,
        grid=(n // tn,),
        in_specs=[pl.BlockSpec((8, D_MODEL), lambda j: (0, 0)),
                  pl.BlockSpec((D_MODEL, tn), lambda j: (0, j)),
                  pl.BlockSpec((1, tn), lambda j: (0, j))],
        out_specs=pl.BlockSpec((8, tn), lambda j: (0, j)),
        out_shape=jax.ShapeDtypeStruct((8, n), F32),
        compiler_params=_params(1),
        name="adaln_mod",
    )(cmat, w_mod, b_mod.reshape(1, n))


def _ln_mod_kernel(x_ref, sh_ref, sc_ref, o_ref):
    y = _layer_norm_rows(x_ref[0])
    o_ref[0] = (y * (1.0 + sc_ref[0]) + sh_ref[0]).astype(BF)


def _ln_mod_call(x, shift, scale, tm):
    b, s, d = x.shape
    bm = shift.shape[0]
    mod_map = (lambda i, j: (i, 0, 0)) if bm == b else (lambda i, j: (0, 0, 0))
    return pl.pallas_call(
        _ln_mod_kernel,
        grid=(b, s // tm),
        in_specs=[pl.BlockSpec((1, tm, d), lambda i, j: (i, j, 0)),
                  pl.BlockSpec((1, 1, d), mod_map),
                  pl.BlockSpec((1, 1, d), mod_map)],
        out_specs=pl.BlockSpec((1, tm, d), lambda i, j: (i, j, 0)),
        out_shape=jax.ShapeDtypeStruct((b, s, d), BF),
        compiler_params=_params(2),
        name="ln_modulate",
    )(x, shift, scale)


LAT_COLS = KV_LORA + 2 * LANES + Q_LORA
GATE_COLS = 2 * D_MODEL
FOURIER_BLOCK = GATE_COLS // FOURIER_DIM
LAT_BLOCK = (GATE_COLS + FOURIER_DIM) // LAT_COLS
assert GATE_COLS % FOURIER_DIM == 0 and (GATE_COLS + FOURIER_DIM) % LAT_COLS == 0


def _latent_kernel(h_ref, w_ref, b_ref, gkv_ref, gq_ref, cos_ref, sin_ref,
                   ckv_ref, kr_ref, cq_ref):
    acc = _dot(h_ref[0], w_ref[...]) + b_ref[...]
    ckv = acc[:, :KV_LORA]
    ka = acc[:, KV_LORA:KV_LORA + LANES]
    kb = acc[:, KV_LORA + LANES:KV_LORA + 2 * LANES]
    cq = acc[:, KV_LORA + 2 * LANES:]
    ckv_n = ckv * lax.rsqrt(jnp.mean(ckv * ckv, axis=-1, keepdims=True) + EPS)
    cq_n = cq * lax.rsqrt(jnp.mean(cq * cq, axis=-1, keepdims=True) + EPS)
    ckv_ref[0] = (ckv_n * gkv_ref[...]).astype(BF)
    cq_ref[0] = (cq_n * gq_ref[...]).astype(BF)
    kr_ref[0] = (ka * cos_ref[...] + kb * sin_ref[...]).astype(BF)


def _latent_call(h, w_all, b_all, gkv, gq, cos, sin, tm):
    b, s, d = h.shape
    row = lambda i, j: (i, j, 0)
    const = lambda i, j: (0, 0)
    wcol = lambda i, j: (0, LAT_BLOCK)
    return pl.pallas_call(
        _latent_kernel,
        grid=(b, s // tm),
        in_specs=[pl.BlockSpec((1, tm, d), row),
                  pl.BlockSpec((d, LAT_COLS), wcol),
                  pl.BlockSpec((1, LAT_COLS), wcol),
                  pl.BlockSpec((1, KV_LORA), const),
                  pl.BlockSpec((1, Q_LORA), const),
                  pl.BlockSpec((tm, LANES), lambda i, j: (j, 0)),
                  pl.BlockSpec((tm, LANES), lambda i, j: (j, 0))],
        out_specs=[pl.BlockSpec((1, tm, KV_LORA), row),
                   pl.BlockSpec((1, tm, LANES), row),
                   pl.BlockSpec((1, tm, Q_LORA), row)],
        out_shape=[jax.ShapeDtypeStruct((b, s, KV_LORA), BF),
                   jax.ShapeDtypeStruct((b, s, LANES), BF),
                   jax.ShapeDtypeStruct((b, s, Q_LORA), BF)],
        compiler_params=_params(2),
        name="latent_proj",
    )(h, w_all, b_all, gkv, gq, cos, sin)


def _fproj_kernel(h_ref, w_ref, b_ref, dc_ref, gc_ref, gs_ref):
    f = (_dot(h_ref[...], w_ref[...]) + b_ref[...]).astype(BF)
    for g in range(N_FOURIER_GROUPS):
        lo = g * FOURIER_GROUP_DIM
        r = _dot(f[:, lo:lo + FOURIER_GROUP_DIM], dc_ref[...])
        gc_ref[:, lo:lo + FOURIER_GROUP_DIM] = r[:, :FOURIER_GROUP_DIM].astype(BF)
        gs_ref[:, lo:lo + FOURIER_GROUP_DIM] = r[:, FOURIER_GROUP_DIM:].astype(BF)


def _fproj_call(h2d, w_all, b_all, dc, tm):
    t, d = h2d.shape
    const = lambda i: (0, 0)
    wcol = lambda i: (0, FOURIER_BLOCK)
    return pl.pallas_call(
        _fproj_kernel,
        grid=(t // tm,),
        in_specs=[pl.BlockSpec((tm, d), lambda i: (i, 0)),
                  pl.BlockSpec((d, FOURIER_DIM), wcol),
                  pl.BlockSpec((1, FOURIER_DIM), wcol),
                  pl.BlockSpec((FOURIER_GROUP_DIM, 2 * FOURIER_GROUP_DIM), const)],
        out_specs=[pl.BlockSpec((tm, FOURIER_DIM), lambda i: (i, 0)),
                   pl.BlockSpec((tm, FOURIER_DIM), lambda i: (i, 0))],
        out_shape=[jax.ShapeDtypeStruct((t, FOURIER_DIM), BF),
                   jax.ShapeDtypeStruct((t, FOURIER_DIM), BF)],
        compiler_params=_params(1),
        name="fourier_in_proj",
    )(h2d, w_all, b_all, dc)


def _gate_kernel(h_ref, w_ref, b_ref, o_ref):
    o_ref[...] = jax.nn.sigmoid(_dot(h_ref[...], w_ref[...]) + b_ref[...]).astype(BF)


def _gate_call(h2d, w_all, b_all, tm, tn):
    t, d = h2d.shape
    n = GATE_COLS
    return pl.pallas_call(
        _gate_kernel,
        grid=(n // tn, t // tm),
        in_specs=[pl.BlockSpec((tm, d), lambda j, i: (i, 0)),
                  pl.BlockSpec((d, tn), lambda j, i: (0, j)),
                  pl.BlockSpec((1, tn), lambda j, i: (0, j))],
        out_specs=pl.BlockSpec((tm, tn), lambda j, i: (i, j)),
        out_shape=jax.ShapeDtypeStruct((t, n), BF),
        compiler_params=_params(2),
        name="gate_proj",
    )(h2d, w_all, b_all)


def _kv_up_kernel(lat_ref, kr_ref, w_ref, k_ref, v_ref):
    lat = lat_ref[0]
    kr = kr_ref[0]
    for h in range(N_HEADS):
        lo = h * (QK_NOPE + V_DIM)
        kv = _dot(lat, w_ref[:, lo:lo + QK_NOPE + V_DIM])
        k_ref[0, h] = jnp.concatenate([kv[:, :QK_NOPE].astype(BF), kr], axis=-1)
        v_ref[0, h] = kv[:, QK_NOPE:].astype(BF)


def _kv_up_call(lat, kr, w_ukv, tm):
    b, t, _ = lat.shape
    return pl.pallas_call(
        _kv_up_kernel,
        grid=(b, t // tm),
        in_specs=[pl.BlockSpec((1, tm, KV_LORA), lambda i, j: (i, j, 0)),
                  pl.BlockSpec((1, tm, LANES), lambda i, j: (i, j, 0)),
                  pl.BlockSpec(w_ukv.shape, lambda i, j: (0, 0))],
        out_specs=[pl.BlockSpec((1, N_HEADS, tm, QK_PAD), lambda i, j: (i, 0, j, 0)),
                   pl.BlockSpec((1, N_HEADS, tm, V_DIM), lambda i, j: (i, 0, j, 0))],
        out_shape=[jax.ShapeDtypeStruct((b, N_HEADS, t, QK_PAD), BF),
                   jax.ShapeDtypeStruct((b, N_HEADS, t, V_DIM), BF)],
        compiler_params=_params(2),
        name="kv_up_proj",
    )(lat, kr, w_ukv)


Q_HEAD_COLS = 3 * LANES


def _q_up_kernel(cq_ref, w_ref, cos_ref, sin_ref, q_ref):
    cq = cq_ref[0]
    cos = cos_ref[...]
    sin = sin_ref[...]
    scale = (QK_NOPE + QK_ROPE) ** -0.5
    for h in range(N_HEADS):
        lo = h * Q_HEAD_COLS
        acc = _dot(cq, w_ref[:, lo:lo + Q_HEAD_COLS])
        qn = acc[:, :LANES]
        qr = acc[:, LANES:2 * LANES] * cos + acc[:, 2 * LANES:] * sin
        q_ref[0, h] = (jnp.concatenate([qn, qr], axis=-1) * scale).astype(BF)


def _q_up_call(cq, w_q, cos, sin, tm):
    b, s, _ = cq.shape
    return pl.pallas_call(
        _q_up_kernel,
        grid=(b, s // tm),
        in_specs=[pl.BlockSpec((1, tm, Q_LORA), lambda i, j: (i, j, 0)),
                  pl.BlockSpec(w_q.shape, lambda i, j: (0, 0)),
                  pl.BlockSpec((tm, LANES), lambda i, j: (j, 0)),
                  pl.BlockSpec((tm, LANES), lambda i, j: (j, 0))],
        out_specs=pl.BlockSpec((1, N_HEADS, tm, QK_PAD), lambda i, j: (i, 0, j, 0)),
        out_shape=jax.ShapeDtypeStruct((b, N_HEADS, s, QK_PAD), BF),
        compiler_params=_params(2),
        name="q_up_proj",
    )(cq, w_q, cos, sin)


ATTN_LAG = 1


def _attn_kernel(q_ref, k_ref, v_ref, o_ref, s_scr, m_scr, *, kc):
    n = pl.program_id(0)
    tq = q_ref.shape[2]
    t = k_ref.shape[2]

    @pl.when(n == 0)
    def _():
        s_scr[...] = jnp.zeros_like(s_scr)
        m_scr[...] = jnp.zeros_like(m_scr)

    def step(cur, prev):
        q = q_ref[0, 0]
        m_prev = m_scr[prev]
        mrun = None
        lrun = jnp.zeros((tq, LANES), F32)
        acc = jnp.zeros((tq, V_DIM), F32)
        for c in range(t // kc):
            ks = slice(c * kc, (c + 1) * kc)
            s_c = lax.dot_general(q, k_ref[0, 0, ks, :], (((1,), (1,)), ((), ())),
                                  preferred_element_type=F32)
            s_scr[cur, :, ks] = s_c
            pieces = []
            for j in range(kc // LANES):
                lanes = slice(j * LANES, (j + 1) * LANES)
                col = slice(c * kc + j * LANES, c * kc + (j + 1) * LANES)
                p_j = jnp.exp(s_scr[prev, :, col] - m_prev)
                lrun = lrun + p_j
                pieces.append(p_j.astype(BF))
                mrun = s_c[:, lanes] if mrun is None else jnp.maximum(mrun, s_c[:, lanes])
            acc = acc + _dot(jnp.concatenate(pieces, axis=-1), v_ref[0, 0, ks, :])
        o_ref[0] = (acc / jnp.sum(lrun, axis=-1, keepdims=True)).astype(BF)
        m_scr[cur] = jnp.broadcast_to(jnp.max(mrun, axis=-1, keepdims=True), (tq, LANES))

    @pl.when(n % 2 == 0)
    def _():
        step(0, 1)

    @pl.when(n % 2 == 1)
    def _():
        step(1, 0)


def _attn_call(q, k, v, tq, kc):
    b, h, s, _ = q.shape
    t = k.shape[2]
    nq = s // tq
    total = b * h * nq

    def block(n, lag):
        i = jnp.clip(n - lag, 0, total - 1)
        return i // (h * nq), (i // nq) % h, i % nq

    def q_map(n):
        bi, hi, qi = block(n, 0)
        return bi, hi, qi, 0

    def k_map(n):
        bi, hi, _ = block(n, 0)
        return bi, hi, 0, 0

    def v_map(n):
        bi, hi, _ = block(n, ATTN_LAG)
        return bi, hi, 0, 0

    def o_map(n):
        bi, hi, qi = block(n, ATTN_LAG)
        return bi, qi, hi

    return pl.pallas_call(
        functools.partial(_attn_kernel, kc=kc),
        grid=(total + ATTN_LAG,),
        in_specs=[pl.BlockSpec((1, 1, tq, QK_PAD), q_map),
                  pl.BlockSpec((1, 1, t, QK_PAD), k_map),
                  pl.BlockSpec((1, 1, t, V_DIM), v_map)],
        out_specs=pl.BlockSpec((1, tq, V_DIM), o_map),
        out_shape=jax.ShapeDtypeStruct((b, s, h * V_DIM), BF),
        scratch_shapes=[pltpu.VMEM((2, tq, t), F32), pltpu.VMEM((2, tq, LANES), F32)],
        compiler_params=_params(1),
        name="mla_attention",
    )(q, k, v)


def _pos_dft_kernel(c_ref, s_ref, gc_ref, gs_ref, o_ref):
    o_ref[0] = (_dot(c_ref[...], gc_ref[0]) - _dot(s_ref[...], gs_ref[0])).astype(BF)


def _pos_dft_call(cs, ss, gc, gs, tm, tn):
    b, s, n = gc.shape
    return pl.pallas_call(
        _pos_dft_kernel,
        grid=(b, n // tn, s // tm),
        in_specs=[pl.BlockSpec((tm, s), lambda i, j, m: (m, 0)),
                  pl.BlockSpec((tm, s), lambda i, j, m: (m, 0)),
                  pl.BlockSpec((1, s, tn), lambda i, j, m: (i, 0, j)),
                  pl.BlockSpec((1, s, tn), lambda i, j, m: (i, 0, j))],
        out_specs=pl.BlockSpec((1, tm, tn), lambda i, j, m: (i, m, j)),
        out_shape=jax.ShapeDtypeStruct((b, s, n), BF),
        compiler_params=_params(3),
        name="position_dft",
    )(cs, ss, gc, gs)


def _merge_kernel(a_ref, f_ref, wo_ref, wf_ref, ga_ref, gb_ref, o_ref):
    ya = _dot(a_ref[...], wo_ref[...])
    yb = _dot(f_ref[...], wf_ref[...])
    o_ref[...] = (ga_ref[...].astype(F32) * ya + gb_ref[...].astype(F32) * yb).astype(BF)


def _merge_call(attn, fm, w_o, w_f, gates, tm, tn):
    t, d = attn.shape
    nb = D_MODEL // tn
    return pl.pallas_call(
        _merge_kernel,
        grid=(nb, t // tm),
        in_specs=[pl.BlockSpec((tm, d), lambda j, i: (i, 0)),
                  pl.BlockSpec((tm, FOURIER_DIM), lambda j, i: (i, 0)),
                  pl.BlockSpec((d, tn), lambda j, i: (0, j)),
                  pl.BlockSpec((FOURIER_DIM, tn), lambda j, i: (0, j)),
                  pl.BlockSpec((tm, tn), lambda j, i: (i, j)),
                  pl.BlockSpec((tm, tn), lambda j, i: (i, j + nb))],
        out_specs=pl.BlockSpec((tm, tn), lambda j, i: (i, j)),
        out_shape=jax.ShapeDtypeStruct((t, D_MODEL), BF),
        compiler_params=_params(2),
        name="branch_merge",
    )(attn, fm, w_o, w_f, gates, gates)


def _outproj_kernel(m_ref, w_ref, x_ref, g1_ref, lg_ref, lb_ref, sh_ref, sc_ref,
                    x1_ref, h_ref, ht_ref):
    y = _dot(m_ref[...], w_ref[...])
    z = DEEPNORM_ALPHA * x_ref[0] + g1_ref[0] * y
    x1 = _layer_norm_rows(z) * lg_ref[...] + lb_ref[...]
    x1_ref[0] = x1
    h2 = _layer_norm_rows(x1) * (1.0 + sc_ref[0]) + sh_ref[0]
    h_ref[...] = h2.astype(BF)
    ht_ref[...] = h2.T.astype(BF)


def _outproj_call(merged, w_out, x, g1, ln_g, ln_b, sh2, sc2, tm):
    b, s, d = x.shape
    nb = s // tm
    bmap = lambda i, j: (i, 0, 0)
    const = lambda i, j: (0, 0)
    return pl.pallas_call(
        _outproj_kernel,
        grid=(b, nb),
        in_specs=[pl.BlockSpec((tm, d), lambda i, j: (i * nb + j, 0)),
                  pl.BlockSpec((d, d), const),
                  pl.BlockSpec((1, tm, d), lambda i, j: (i, j, 0)),
                  pl.BlockSpec((1, 1, d), bmap),
                  pl.BlockSpec((1, d), const),
                  pl.BlockSpec((1, d), const),
                  pl.BlockSpec((1, 1, d), bmap),
                  pl.BlockSpec((1, 1, d), bmap)],
        out_specs=[pl.BlockSpec((1, tm, d), lambda i, j: (i, j, 0)),
                   pl.BlockSpec((tm, d), lambda i, j: (i * nb + j, 0)),
                   pl.BlockSpec((d, tm), lambda i, j: (0, i * nb + j))],
        out_shape=[jax.ShapeDtypeStruct((b, s, d), F32),
                   jax.ShapeDtypeStruct((b * s, d), BF),
                   jax.ShapeDtypeStruct((d, b * s), BF)],
        compiler_params=_params(2),
        name="out_proj_deepnorm",
    )(merged, w_out, x, g1, ln_g, ln_b, sh2, sc2)


N_HP = 2 * PEER_HEADS
NOT_TOP = 127.0
HALF_K = PEER_TOPK // 2
BF16_ROWS = 16


def _top_values(s, k):
    tops = []
    for _ in range(k):
        m = jnp.max(s, axis=0, keepdims=True)
        tops.append(m)
        s = jnp.where(s == m, -jnp.inf, s)
    return tops


def _top_values_ranked(s, k):
    tops = []
    rank = jnp.full(s.shape, NOT_TOP, F32)
    for r in range(k):
        m = jnp.max(s, axis=0, keepdims=True)
        tops.append(m)
        hit = s == m
        rank = jnp.where(hit, float(r), rank)
        s = jnp.where(hit, -jnp.inf, s)
    return tops, rank


def _select_kernel(wq_ref, keys_ref, h_ref, rank_ref, cnt_ref, e1_ref, e2_ref,
                   s_scr, top_scr):
    q = _dot(h_ref[...], wq_ref[...]).astype(BF)
    for hp in range(N_HP):
        s_scr[hp] = lax.dot_general(keys_ref[hp], q[:, hp * PEER_HALF:(hp + 1) * PEER_HALF],
                                    (((1,), (1,)), ((), ())), preferred_element_type=F32)

    def head_body(h, carry):
        s1 = s_scr[2 * h]
        s2 = s_scr[2 * h + 1]
        tops1 = _top_values(s1, PEER_TOPK + 1)
        tops2, rank2 = _top_values_ranked(s2, PEER_TOPK + 1)
        for r in range(PEER_TOPK):
            top_scr[0, r:r + 1, :] = tops1[r]
            top_scr[1, r:r + 1, :] = tops2[r]
        t1 = top_scr[0]
        t2 = top_scr[1]
        m1, m2 = tops1[0], tops2[0]
        cand = jnp.concatenate(
            [m1 + t2]
            + [tops1[a] + t2[:HALF_K] for a in range(1, HALF_K)]
            + [t1[HALF_K:] + m2], axis=0)
        best = _top_values(cand, PEER_TOPK + 1)
        outside = jnp.maximum(tops1[PEER_TOPK] + m2, m1 + tops2[PEER_TOPK])
        runner_up = jnp.maximum(best[PEER_TOPK], outside)
        tau = 0.5 * (best[PEER_TOPK - 1] + runner_up)
        sel = cand >= tau
        z = jnp.sum(jnp.where(sel, jnp.exp(cand - (m1 + m2)), 0.0), axis=0, keepdims=True)
        self32 = sel.astype(F32)
        counts = [jnp.sum(self32[:PEER_TOPK], axis=0, keepdims=True)]
        for a in range(1, HALF_K):
            lo = PEER_TOPK + (a - 1) * HALF_K
            counts.append(jnp.sum(self32[lo:lo + HALF_K], axis=0, keepdims=True))
        lo = PEER_TOPK + (HALF_K - 1) * HALF_K
        for a in range(HALF_K, PEER_TOPK):
            counts.append(self32[lo + a - HALF_K:lo + a - HALF_K + 1])
        cnt = jnp.zeros_like(s1)
        for a in range(PEER_TOPK):
            cnt = jnp.where(s1 == tops1[a], counts[a], cnt)
        rank_ref[h] = rank2.astype(BF)
        cnt_ref[h] = cnt
        e1_ref[h] = jnp.exp(s1 - m1) / z
        e2_ref[h] = jnp.exp(s2 - m2).astype(BF)
        return carry
    lax.fori_loop(0, PEER_HEADS, head_body, 0)


def _select_call(wq, keys, h, tn):
    t, d = h.shape
    shape = (PEER_HEADS, N_KEYS, t)
    ospec = pl.BlockSpec((PEER_HEADS, N_KEYS, tn), lambda i: (0, 0, i))
    return pl.pallas_call(
        _select_kernel,
        grid=(t // tn,),
        in_specs=[pl.BlockSpec(wq.shape, lambda i: (0, 0)),
                  pl.BlockSpec(keys.shape, lambda i: (0, 0, 0)),
                  pl.BlockSpec((tn, d), lambda i: (i, 0))],
        out_specs=[ospec, ospec, ospec, ospec],
        out_shape=[jax.ShapeDtypeStruct(shape, BF), jax.ShapeDtypeStruct(shape, F32),
                   jax.ShapeDtypeStruct(shape, F32), jax.ShapeDtypeStruct(shape, BF)],
        scratch_shapes=[pltpu.VMEM((N_HP, N_KEYS, tn), F32),
                        pltpu.VMEM((2, PEER_TOPK, tn), F32)],
        compiler_params=_params(1),
        name="peer_select",
    )(wq, keys, h)


def _expert_prep_kernel(u_ref, v_ref, ub_ref, vt_ref):
    ub_ref[...] = u_ref[...].astype(BF)
    vt_ref[...] = v_ref[...].T.astype(BF)


def _expert_prep_call(u, v, te):
    e, d = u.shape
    return pl.pallas_call(
        _expert_prep_kernel,
        grid=(e // te,),
        in_specs=[pl.BlockSpec((te, d), lambda i: (i, 0)),
                  pl.BlockSpec((te, d), lambda i: (i, 0))],
        out_specs=[pl.BlockSpec((te, d), lambda i: (i, 0)),
                   pl.BlockSpec((d, te), lambda i: (0, i))],
        out_shape=[jax.ShapeDtypeStruct((e, d), BF), jax.ShapeDtypeStruct((d, e), BF)],
        compiler_params=_params(1),
        name="expert_table_layout",
    )(u, v)


def _gelu(x):
    return 0.5 * x * (1.0 + lax.erf(x * math.sqrt(0.5)))


def _peer_kernel(u_ref, vt_ref, ht_ref, rank_ref, cnt_ref, e1_ref, e2_ref, o_ref,
                 a_scr, c_scr, *, rows, act_slices, mix_slices):
    e = pl.program_id(1)
    tn = ht_ref.shape[1]
    d = vt_ref.shape[0]
    half = rows // 2
    hrows = half * N_KEYS

    @pl.when(e == 0)
    def _():
        o_ref[...] = jnp.zeros_like(o_ref)

    def coef_row(r):
        i1 = e * rows + r
        w = None
        for h in range(PEER_HEADS):
            cnt = jnp.broadcast_to(cnt_ref[h, pl.ds(i1, 1), :], (BF16_ROWS, tn)).astype(BF)
            e1 = jnp.broadcast_to(e1_ref[h, pl.ds(i1, 1), :], (BF16_ROWS, tn)).astype(BF)
            contrib = jnp.where(rank_ref[h] < cnt[None], e2_ref[h] * e1[None],
                                jnp.zeros((), BF))
            w = contrib if w is None else w + contrib
        act = _gelu(a_scr[r * N_KEYS:(r + 1) * N_KEYS, :]).astype(BF)
        c_scr[r * N_KEYS:(r + 1) * N_KEYS, :] = w.reshape(N_KEYS, tn) * act

    ht = ht_ref[...]
    a_scr[0:hrows, :] = _dot(u_ref[0:hrows, :], ht)
    srows = hrows // act_slices
    for j in range(act_slices):
        lo = hrows + j * srows
        a_scr[lo:lo + srows, :] = _dot(u_ref[lo:lo + srows, :], ht)
        for r in range(j * half // act_slices, (j + 1) * half // act_slices):
            coef_row(r)
    mrows = d // mix_slices
    c_a = c_scr[0:hrows, :]
    for j in range(mix_slices):
        o_ref[j * mrows:(j + 1) * mrows, :] += _dot(vt_ref[j * mrows:(j + 1) * mrows, 0:hrows], c_a)
        for r in range(j * half // mix_slices, (j + 1) * half // mix_slices):
            coef_row(half + r)
    o_ref[...] += _dot(vt_ref[:, hrows:], c_scr[hrows:, :])


def _peer_call(u, v_t, h_t, rank2, cnt, e1, e2, tn, te, act_slices, mix_slices):
    d, t = h_t.shape
    rows = te // N_KEYS
    groups = N_KEYS // BF16_ROWS
    rank4 = rank2.reshape(PEER_HEADS, groups, BF16_ROWS, t)
    e24 = e2.reshape(PEER_HEADS, groups, BF16_ROWS, t)
    sel3 = pl.BlockSpec((PEER_HEADS, N_KEYS, tn), lambda i, e: (0, 0, i))
    sel4 = pl.BlockSpec((PEER_HEADS, groups, BF16_ROWS, tn), lambda i, e: (0, 0, 0, i))
    return pl.pallas_call(
        functools.partial(_peer_kernel, rows=rows, act_slices=act_slices,
                          mix_slices=mix_slices),
        grid=(t // tn, N_EXPERTS // te),
        in_specs=[pl.BlockSpec((te, d), lambda i, e: (e, 0)),
                  pl.BlockSpec((d, te), lambda i, e: (0, e)),
                  pl.BlockSpec((d, tn), lambda i, e: (0, i)),
                  sel4, sel3, sel3, sel4],
        out_specs=pl.BlockSpec((d, tn), lambda i, e: (0, i)),
        out_shape=jax.ShapeDtypeStruct((d, t), F32),
        scratch_shapes=[pltpu.VMEM((te, tn), F32), pltpu.VMEM((te, tn), BF)],
        compiler_params=_params(2),
        name="peer_dense",
    )(u, v_t, h_t, rank4, cnt, e1, e24)


def _final_kernel(yt_ref, x_ref, g2_ref, lg_ref, lb_ref, o_ref):
    z = DEEPNORM_ALPHA * x_ref[0] + g2_ref[0] * yt_ref[...].T
    o_ref[0] = _layer_norm_rows(z) * lg_ref[...] + lb_ref[...]


def _final_call(y_t, x1, g2, ln_g, ln_b, tm):
    b, s, d = x1.shape
    nb = s // tm
    return pl.pallas_call(
        _final_kernel,
        grid=(b, nb),
        in_specs=[pl.BlockSpec((d, tm), lambda i, j: (0, i * nb + j)),
                  pl.BlockSpec((1, tm, d), lambda i, j: (i, j, 0)),
                  pl.BlockSpec((1, 1, d), lambda i, j: (i, 0, 0)),
                  pl.BlockSpec((1, d), lambda i, j: (0, 0)),
                  pl.BlockSpec((1, d), lambda i, j: (0, 0))],
        out_specs=pl.BlockSpec((1, tm, d), lambda i, j: (i, j, 0)),
        out_shape=jax.ShapeDtypeStruct((b, s, d), F32),
        compiler_params=_params(2),
        name="final_deepnorm",
    )(y_t, x1, g2, ln_g, ln_b)


def _rope_rotation(w):
    pairs = w.reshape(w.shape[:-1] + (w.shape[-1] // 2, 2))
    return jnp.stack([-pairs[..., 1], pairs[..., 0]], axis=-1).reshape(w.shape)


def _pad_lanes(w):
    return jnp.pad(w, [(0, 0)] * (w.ndim - 1) + [(0, LANES - w.shape[-1])])


def _rope_tables(seq):
    rows = seq // GRID_W
    row = jnp.repeat(jnp.arange(rows, dtype=F32), GRID_W)
    col = jnp.tile(jnp.arange(GRID_W, dtype=F32), rows)
    half = QK_ROPE // 2
    inv = ROPE_THETA ** (-jnp.arange(0, half, 2, dtype=F32) / half)
    ang = jnp.concatenate([row[:, None] * inv, col[:, None] * inv], axis=-1)
    cos = _pad_lanes(jnp.repeat(jnp.cos(ang), 2, axis=-1))
    sin = _pad_lanes(jnp.repeat(jnp.sin(ang), 2, axis=-1))
    return cos, sin


def _dft_matrices(n, scale):
    k = np.arange(n, dtype=np.int64)
    ang = 2.0 * np.pi * ((k[:, None] * k[None, :]) % n).astype(np.float64) / n
    return np.cos(ang) * scale, np.sin(ang) * scale


def kernel(x, c, ctx, c_ctx, w_mod, b_mod, w_in, b_in, q_norm_g, w_uq, kv_norm_g, w_ukv,
           w_o_mla, w_fourier, w_out, ln1_g, ln1_b, peer_wq, peer_keys, peer_u, peer_v,
           ln2_g, ln2_b):
    B, S, D = x.shape
    T = B * S
    CT = ctx.shape[1]
    l = 0

    cmat = jnp.concatenate([c, c_ctx[None, :], jnp.zeros((8 - B - 1, D), F32)], axis=0)
    mod = _mod_call(cmat, w_mod[l], b_mod[l])
    mx = mod[:B].reshape(B, 1, 6, D)
    sh1, sc1, g1, sh2, sc2, g2 = [mx[:, :, i, :] for i in range(6)]
    mc = mod[B].reshape(1, 1, 6, D)
    sh1c, sc1c = mc[:, :, 0, :], mc[:, :, 1, :]

    wi, bi = w_in[l], b_in[l]
    w_kr, b_kr = wi[:, KV_LORA:KV_END], bi[KV_LORA:KV_END]
    q0 = KV_END
    f0 = KV_END + Q_LORA
    g0 = f0 + FOURIER_DIM
    w_all = jnp.concatenate(
        [wi[:, g0:], wi[:, f0:g0],
         wi[:, :KV_LORA], _pad_lanes(w_kr), _pad_lanes(_rope_rotation(w_kr)), wi[:, q0:f0]],
        axis=1).astype(BF)
    b_all = jnp.concatenate(
        [bi[g0:], bi[f0:g0],
         bi[:KV_LORA], _pad_lanes(b_kr), _pad_lanes(_rope_rotation(b_kr)), bi[q0:f0]])[None, :]
    wq3 = w_uq[l].reshape(Q_LORA, N_HEADS, QK_NOPE + QK_ROPE)
    wq_rope = wq3[:, :, QK_NOPE:]
    w_q = jnp.concatenate(
        [wq3[:, :, :QK_NOPE], _pad_lanes(wq_rope), _pad_lanes(_rope_rotation(wq_rope))],
        axis=-1).reshape(Q_LORA, N_HEADS * Q_HEAD_COLS).astype(BF)
    w_kv = w_ukv[l].astype(BF)
    gkv = kv_norm_g[l][None, :]
    gq = q_norm_g[l][None, :]

    cos, sin = _rope_tables(S)
    cos_c = _pad_lanes(jnp.ones((CT, QK_ROPE), F32))
    sin_c = jnp.zeros((CT, LANES), F32)
    dc_c, dc_s = _dft_matrices(FOURIER_GROUP_DIM, FOURIER_GROUP_DIM ** -0.5)
    dc = jnp.asarray(np.concatenate([dc_c, dc_s], axis=1), dtype=F32).astype(BF)
    ds_c, ds_s = _dft_matrices(S, S ** -0.5)
    cs = jnp.asarray(ds_c, dtype=F32).astype(BF)
    ss = jnp.asarray(ds_s, dtype=F32).astype(BF)

    hx = _ln_mod_call(x, sh1, sc1, 512)
    hc = _ln_mod_call(ctx, sh1c, sc1c, CT)
    ckv_x, kr_x, cq_x = _latent_call(hx, w_all, b_all, gkv, gq, cos, sin, 512)
    ckv_c, kr_c, _ = _latent_call(hc, w_all, b_all, gkv, gq, cos_c, sin_c, CT)
    lat = jnp.concatenate([ckv_c, ckv_x], axis=1)
    kr = jnp.concatenate([kr_c, kr_x], axis=1)
    k_all, v_all = _kv_up_call(lat, kr, w_kv, 768)
    q_all = _q_up_call(cq_x, w_q, cos, sin, 512)
    attn = _attn_call(q_all, k_all, v_all, 1024, 256)

    hx2d = hx.reshape(T, D)
    gc, gs = _fproj_call(hx2d, w_all, b_all, dc, 512)
    fm = _pos_dft_call(cs, ss, gc.reshape(B, S, FOURIER_DIM), gs.reshape(B, S, FOURIER_DIM),
                       512, 512)
    gates = _gate_call(hx2d, w_all, b_all, 512, 1024)
    merged = _merge_call(attn.reshape(T, D), fm.reshape(T, FOURIER_DIM),
                         w_o_mla[l].astype(BF), w_fourier[l].astype(BF), gates, 512, 512)
    x1, h2, h2_t = _outproj_call(merged, w_out[l].astype(BF), x, g1, ln1_g[l][None, :],
                                 ln1_b[l][None, :], sh2, sc2, 512)

    keys = peer_keys[l].reshape(N_HP, N_KEYS, PEER_HALF).astype(BF)
    rank2, cnt, e1, e2 = _select_call(peer_wq[l].astype(BF), keys, h2, 512)
    u_b, v_t = _expert_prep_call(peer_u[l], peer_v[l], 512)
    y_t = _peer_call(u_b, v_t, h2_t, rank2, cnt, e1, e2, 512, 1024, 2, 4)
    return _final_call(y_t, x1, g2, ln2_g[l][None, :], ln2_b[l][None, :], 256)
```

```python
import functools
import math

import numpy as np
import jax
import jax.numpy as jnp
from jax import lax
from jax.experimental import pallas as pl
from jax.experimental.pallas import tpu as pltpu

D_MODEL = 2048
GRID_W = 64
N_HEADS = 16
QK_NOPE = 128
QK_ROPE = 64
V_DIM = 128
Q_LORA = 512
KV_LORA = 512
ROPE_THETA = 10000.0
N_FOURIER_GROUPS = 4
FOURIER_GROUP_DIM = 256
FOURIER_DIM = N_FOURIER_GROUPS * FOURIER_GROUP_DIM
KV_END = KV_LORA + QK_ROPE
PEER_HEADS = 8
N_KEYS = 128
N_EXPERTS = N_KEYS * N_KEYS
PEER_HALF = 128
PEER_TOPK = 16
DEPTH = 1
DEEPNORM_ALPHA = (2.0 * DEPTH) ** 0.25
EPS = 1e-6

LANES = 128
QK_PAD = 2 * LANES
VMEM_LIMIT = 56 * 1024 * 1024

BF = jnp.bfloat16
F32 = jnp.float32


def _params(n_axes, vmem=VMEM_LIMIT):
    return pltpu.CompilerParams(
        dimension_semantics=("arbitrary",) * n_axes, vmem_limit_bytes=vmem)


def _dot(a, b):
    return jnp.dot(a, b, preferred_element_type=F32)


def _dot_nt(a, b):
    return lax.dot_general(a, b, (((1,), (1,)), ((), ())), preferred_element_type=F32)


def _layer_norm_rows(x):
    mu = jnp.mean(x, axis=-1, keepdims=True)
    xc = x - mu
    var = jnp.mean(xc * xc, axis=-1, keepdims=True)
    return xc * lax.rsqrt(var + EPS)


def _mod_kernel(c_ref, w_ref, b_ref, o_ref):
    a = jax.nn.silu(c_ref[...]).astype(BF)
    o_ref[...] = _dot(a, w_ref[...].astype(BF)) + b_ref[...]


def _mod_call(cmat, w_mod, b_mod):
    n = w_mod.shape[1]
    tn = 1024
    return pl.pallas_call(
        _mod_kernel,
        grid=(n // tn,),
        in_specs=[pl.BlockSpec((8, D_MODEL), lambda j: (0, 0)),
                  pl.BlockSpec((D_MODEL, tn), lambda j: (0, j)),
                  pl.BlockSpec((1, tn), lambda j: (0, j))],
        out_specs=pl.BlockSpec((8, tn), lambda j: (0, j)),
        out_shape=jax.ShapeDtypeStruct((8, n), F32),
        compiler_params=_params(1),
        name="adaln_mod",
    )(cmat, w_mod, b_mod.reshape(1, n))


def _ln_mod_kernel(x_ref, sh_ref, sc_ref, o_ref):
    y = _layer_norm_rows(x_ref[0])
    o_ref[0] = (y * (1.0 + sc_ref[0]) + sh_ref[0]).astype(BF)


def _ln_mod_call(x, shift, scale, tm):
    b, s, d = x.shape
    bm = shift.shape[0]
    mod_map = (lambda i, j: (i, 0, 0)) if bm == b else (lambda i, j: (0, 0, 0))
    return pl.pallas_call(
        _ln_mod_kernel,
        grid=(b, s // tm),
        in_specs=[pl.BlockSpec((1, tm, d), lambda i, j: (i, j, 0)),
                  pl.BlockSpec((1, 1, d), mod_map),
                  pl.BlockSpec((1, 1, d), mod_map)],
        out_specs=pl.BlockSpec((1, tm, d), lambda i, j: (i, j, 0)),
        out_shape=jax.ShapeDtypeStruct((b, s, d), BF),
        compiler_params=_params(2),
        name="ln_modulate",
    )(x, shift, scale)


LAT_COLS = KV_LORA + 2 * LANES + Q_LORA
GATE_COLS = 2 * D_MODEL
FOURIER_BLOCK = GATE_COLS // FOURIER_DIM
LAT_BLOCK = (GATE_COLS + FOURIER_DIM) // LAT_COLS
assert GATE_COLS % FOURIER_DIM == 0 and (GATE_COLS + FOURIER_DIM) % LAT_COLS == 0


def _latent_kernel(h_ref, w_ref, b_ref, gkv_ref, gq_ref, cos_ref, sin_ref,
                   ckv_ref, kr_ref, cq_ref):
    acc = _dot_nt(h_ref[0], w_ref[...]) + b_ref[...]
    ckv = acc[:, :KV_LORA]
    ka = acc[:, KV_LORA:KV_LORA + LANES]
    kb = acc[:, KV_LORA + LANES:KV_LORA + 2 * LANES]
    cq = acc[:, KV_LORA + 2 * LANES:]
    ckv_n = ckv * lax.rsqrt(jnp.mean(ckv * ckv, axis=-1, keepdims=True) + EPS)
    cq_n = cq * lax.rsqrt(jnp.mean(cq * cq, axis=-1, keepdims=True) + EPS)
    ckv_ref[0] = (ckv_n * gkv_ref[...]).astype(BF)
    cq_ref[0] = (cq_n * gq_ref[...]).astype(BF)
    kr_ref[0] = (ka * cos_ref[...] + kb * sin_ref[...]).astype(BF)


def _latent_call(h, w_all, b_all, gkv, gq, cos, sin, tm):
    b, s, d = h.shape
    row = lambda i, j: (i, j, 0)
    const = lambda i, j: (0, 0)
    return pl.pallas_call(
        _latent_kernel,
        grid=(b, s // tm),
        in_specs=[pl.BlockSpec((1, tm, d), row),
                  pl.BlockSpec((LAT_COLS, d), lambda i, j: (LAT_BLOCK, 0)),
                  pl.BlockSpec((1, LAT_COLS), lambda i, j: (0, LAT_BLOCK)),
                  pl.BlockSpec((1, KV_LORA), const),
                  pl.BlockSpec((1, Q_LORA), const),
                  pl.BlockSpec((tm, LANES), lambda i, j: (j, 0)),
                  pl.BlockSpec((tm, LANES), lambda i, j: (j, 0))],
        out_specs=[pl.BlockSpec((1, tm, KV_LORA), row),
                   pl.BlockSpec((1, tm, LANES), row),
                   pl.BlockSpec((1, tm, Q_LORA), row)],
        out_shape=[jax.ShapeDtypeStruct((b, s, KV_LORA), BF),
                   jax.ShapeDtypeStruct((b, s, LANES), BF),
                   jax.ShapeDtypeStruct((b, s, Q_LORA), BF)],
        compiler_params=_params(2),
        name="latent_proj",
    )(h, w_all, b_all, gkv, gq, cos, sin)


def _fproj_kernel(h_ref, w_ref, b_ref, dc_ref, gc_ref, gs_ref):
    f = (_dot_nt(h_ref[...], w_ref[...]) + b_ref[...]).astype(BF)
    for g in range(N_FOURIER_GROUPS):
        lo = g * FOURIER_GROUP_DIM
        r = _dot(f[:, lo:lo + FOURIER_GROUP_DIM], dc_ref[...])
        gc_ref[:, lo:lo + FOURIER_GROUP_DIM] = r[:, :FOURIER_GROUP_DIM].astype(BF)
        gs_ref[:, lo:lo + FOURIER_GROUP_DIM] = r[:, FOURIER_GROUP_DIM:].astype(BF)


def _fproj_call(h2d, w_all, b_all, dc, tm):
    t, d = h2d.shape
    const = lambda i: (0, 0)
    return pl.pallas_call(
        _fproj_kernel,
        grid=(t // tm,),
        in_specs=[pl.BlockSpec((tm, d), lambda i: (i, 0)),
                  pl.BlockSpec((FOURIER_DIM, d), lambda i: (FOURIER_BLOCK, 0)),
                  pl.BlockSpec((1, FOURIER_DIM), lambda i: (0, FOURIER_BLOCK)),
                  pl.BlockSpec((FOURIER_GROUP_DIM, 2 * FOURIER_GROUP_DIM), const)],
        out_specs=[pl.BlockSpec((tm, FOURIER_DIM), lambda i: (i, 0)),
                   pl.BlockSpec((tm, FOURIER_DIM), lambda i: (i, 0))],
        out_shape=[jax.ShapeDtypeStruct((t, FOURIER_DIM), BF),
                   jax.ShapeDtypeStruct((t, FOURIER_DIM), BF)],
        compiler_params=_params(1),
        name="fourier_in_proj",
    )(h2d, w_all, b_all, dc)


def _gate_kernel(h_ref, w_ref, b_ref, o_ref):
    o_ref[...] = jax.nn.sigmoid(_dot_nt(h_ref[...], w_ref[...]) + b_ref[...]).astype(BF)


def _gate_call(h2d, w_all, b_all, tm, tn):
    t, d = h2d.shape
    n = GATE_COLS
    return pl.pallas_call(
        _gate_kernel,
        grid=(n // tn, t // tm),
        in_specs=[pl.BlockSpec((tm, d), lambda j, i: (i, 0)),
                  pl.BlockSpec((tn, d), lambda j, i: (j, 0)),
                  pl.BlockSpec((1, tn), lambda j, i: (0, j))],
        out_specs=pl.BlockSpec((tm, tn), lambda j, i: (i, j)),
        out_shape=jax.ShapeDtypeStruct((t, n), BF),
        compiler_params=_params(2),
        name="gate_proj",
    )(h2d, w_all, b_all)


def _kv_up_kernel(lat_ref, kr_ref, w_ref, k_ref, v_ref):
    lat = lat_ref[0]
    kr = kr_ref[0]
    for h in range(N_HEADS):
        lo = h * (QK_NOPE + V_DIM)
        kv = _dot(lat, w_ref[:, lo:lo + QK_NOPE + V_DIM])
        k_ref[0, h] = jnp.concatenate([kv[:, :QK_NOPE].astype(BF), kr], axis=-1)
        v_ref[0, h] = kv[:, QK_NOPE:].astype(BF)


def _kv_up_call(lat, kr, w_ukv, tm):
    b, t, _ = lat.shape
    return pl.pallas_call(
        _kv_up_kernel,
        grid=(b, t // tm),
        in_specs=[pl.BlockSpec((1, tm, KV_LORA), lambda i, j: (i, j, 0)),
                  pl.BlockSpec((1, tm, LANES), lambda i, j: (i, j, 0)),
                  pl.BlockSpec(w_ukv.shape, lambda i, j: (0, 0))],
        out_specs=[pl.BlockSpec((1, N_HEADS, tm, QK_PAD), lambda i, j: (i, 0, j, 0)),
                   pl.BlockSpec((1, N_HEADS, tm, V_DIM), lambda i, j: (i, 0, j, 0))],
        out_shape=[jax.ShapeDtypeStruct((b, N_HEADS, t, QK_PAD), BF),
                   jax.ShapeDtypeStruct((b, N_HEADS, t, V_DIM), BF)],
        compiler_params=_params(2),
        name="kv_up_proj",
    )(lat, kr, w_ukv)


Q_HEAD_COLS = 3 * LANES


def _q_up_kernel(cq_ref, w_ref, cos_ref, sin_ref, q_ref):
    cq = cq_ref[0]
    cos = cos_ref[...]
    sin = sin_ref[...]
    scale = (QK_NOPE + QK_ROPE) ** -0.5
    for h in range(N_HEADS):
        lo = h * Q_HEAD_COLS
        acc = _dot(cq, w_ref[:, lo:lo + Q_HEAD_COLS])
        qn = acc[:, :LANES]
        qr = acc[:, LANES:2 * LANES] * cos + acc[:, 2 * LANES:] * sin
        q_ref[0, h] = (jnp.concatenate([qn, qr], axis=-1) * scale).astype(BF)


def _q_up_call(cq, w_q, cos, sin, tm):
    b, s, _ = cq.shape
    return pl.pallas_call(
        _q_up_kernel,
        grid=(b, s // tm),
        in_specs=[pl.BlockSpec((1, tm, Q_LORA), lambda i, j: (i, j, 0)),
                  pl.BlockSpec(w_q.shape, lambda i, j: (0, 0)),
                  pl.BlockSpec((tm, LANES), lambda i, j: (j, 0)),
                  pl.BlockSpec((tm, LANES), lambda i, j: (j, 0))],
        out_specs=pl.BlockSpec((1, N_HEADS, tm, QK_PAD), lambda i, j: (i, 0, j, 0)),
        out_shape=jax.ShapeDtypeStruct((b, N_HEADS, s, QK_PAD), BF),
        compiler_params=_params(2),
        name="q_up_proj",
    )(cq, w_q, cos, sin)


ATTN_LAG = 1


def _attn_kernel(q_ref, k_ref, v_ref, o_ref, s_scr, m_scr, *, kc):
    n = pl.program_id(0)
    tq = q_ref.shape[2]
    t = k_ref.shape[2]

    @pl.when(n == 0)
    def _():
        s_scr[...] = jnp.zeros_like(s_scr)
        m_scr[...] = jnp.zeros_like(m_scr)

    def step(cur, prev):
        q = q_ref[0, 0]
        m_prev = m_scr[prev]
        mrun = None
        lrun = jnp.zeros((tq, LANES), F32)
        acc = jnp.zeros((tq, V_DIM), F32)
        for c in range(t // kc):
            ks = slice(c * kc, (c + 1) * kc)
            s_c = lax.dot_general(q, k_ref[0, 0, ks, :], (((1,), (1,)), ((), ())),
                                  preferred_element_type=F32)
            s_scr[cur, :, ks] = s_c
            pieces = []
            for j in range(kc // LANES):
                lanes = slice(j * LANES, (j + 1) * LANES)
                col = slice(c * kc + j * LANES, c * kc + (j + 1) * LANES)
                p_j = jnp.exp(s_scr[prev, :, col] - m_prev)
                lrun = lrun + p_j
                pieces.append(p_j.astype(BF))
                mrun = s_c[:, lanes] if mrun is None else jnp.maximum(mrun, s_c[:, lanes])
            acc = acc + _dot(jnp.concatenate(pieces, axis=-1), v_ref[0, 0, ks, :])
        o_ref[0] = (acc / jnp.sum(lrun, axis=-1, keepdims=True)).astype(BF)
        m_scr[cur] = jnp.broadcast_to(jnp.max(mrun, axis=-1, keepdims=True), (tq, LANES))

    @pl.when(n % 2 == 0)
    def _():
        step(0, 1)

    @pl.when(n % 2 == 1)
    def _():
        step(1, 0)


def _attn_call(q, k, v, tq, kc):
    b, h, s, _ = q.shape
    t = k.shape[2]
    nq = s // tq
    total = b * h * nq

    def block(n, lag):
        i = jnp.clip(n - lag, 0, total - 1)
        return i // (h * nq), (i // nq) % h, i % nq

    def q_map(n):
        bi, hi, qi = block(n, 0)
        return bi, hi, qi, 0

    def k_map(n):
        bi, hi, _ = block(n, 0)
        return bi, hi, 0, 0

    def v_map(n):
        bi, hi, _ = block(n, ATTN_LAG)
        return bi, hi, 0, 0

    def o_map(n):
        bi, hi, qi = block(n, ATTN_LAG)
        return bi, qi, hi

    return pl.pallas_call(
        functools.partial(_attn_kernel, kc=kc),
        grid=(total + ATTN_LAG,),
        in_specs=[pl.BlockSpec((1, 1, tq, QK_PAD), q_map),
                  pl.BlockSpec((1, 1, t, QK_PAD), k_map),
                  pl.BlockSpec((1, 1, t, V_DIM), v_map)],
        out_specs=pl.BlockSpec((1, tq, V_DIM), o_map),
        out_shape=jax.ShapeDtypeStruct((b, s, h * V_DIM), BF),
        scratch_shapes=[pltpu.VMEM((2, tq, t), F32), pltpu.VMEM((2, tq, LANES), F32)],
        compiler_params=_params(1),
        name="mla_attention",
    )(q, k, v)


def _pos_dft_kernel(c_ref, s_ref, gc_ref, gs_ref, o_ref):
    o_ref[0] = (_dot(c_ref[...], gc_ref[0]) - _dot(s_ref[...], gs_ref[0])).astype(BF)


def _pos_dft_call(cs, ss, gc, gs, tm, tn):
    b, s, n = gc.shape
    return pl.pallas_call(
        _pos_dft_kernel,
        grid=(b, n // tn, s // tm),
        in_specs=[pl.BlockSpec((tm, s), lambda i, j, m: (m, 0)),
                  pl.BlockSpec((tm, s), lambda i, j, m: (m, 0)),
                  pl.BlockSpec((1, s, tn), lambda i, j, m: (i, 0, j)),
                  pl.BlockSpec((1, s, tn), lambda i, j, m: (i, 0, j))],
        out_specs=pl.BlockSpec((1, tm, tn), lambda i, j, m: (i, m, j)),
        out_shape=jax.ShapeDtypeStruct((b, s, n), BF),
        compiler_params=_params(3),
        name="position_dft",
    )(cs, ss, gc, gs)


def _merge_kernel(a_ref, f_ref, wo_ref, wf_ref, ga_ref, gb_ref, o_ref):
    ya = _dot(a_ref[...], wo_ref[...])
    yb = _dot(f_ref[...], wf_ref[...])
    o_ref[...] = (ga_ref[...].astype(F32) * ya + gb_ref[...].astype(F32) * yb).astype(BF)


def _merge_call(attn, fm, w_o, w_f, gates, tm, tn):
    t, d = attn.shape
    nb = D_MODEL // tn
    return pl.pallas_call(
        _merge_kernel,
        grid=(nb, t // tm),
        in_specs=[pl.BlockSpec((tm, d), lambda j, i: (i, 0)),
                  pl.BlockSpec((tm, FOURIER_DIM), lambda j, i: (i, 0)),
                  pl.BlockSpec((d, tn), lambda j, i: (0, j)),
                  pl.BlockSpec((FOURIER_DIM, tn), lambda j, i: (0, j)),
                  pl.BlockSpec((tm, tn), lambda j, i: (i, j)),
                  pl.BlockSpec((tm, tn), lambda j, i: (i, j + nb))],
        out_specs=pl.BlockSpec((tm, tn), lambda j, i: (i, j)),
        out_shape=jax.ShapeDtypeStruct((t, D_MODEL), BF),
        compiler_params=_params(2),
        name="branch_merge",
    )(attn, fm, w_o, w_f, gates, gates)


def _outproj_kernel(m_ref, w_ref, x_ref, g1_ref, lg_ref, lb_ref, sh_ref, sc_ref,
                    x1_ref, h_ref, ht_ref):
    y = _dot(m_ref[...], w_ref[...])
    z = DEEPNORM_ALPHA * x_ref[0] + g1_ref[0] * y
    x1 = _layer_norm_rows(z) * lg_ref[...] + lb_ref[...]
    x1_ref[0] = x1
    h2 = _layer_norm_rows(x1) * (1.0 + sc_ref[0]) + sh_ref[0]
    h_ref[...] = h2.astype(BF)
    ht_ref[...] = h2.T.astype(BF)


def _outproj_call(merged, w_out, x, g1, ln_g, ln_b, sh2, sc2, tm):
    b, s, d = x.shape
    nb = s // tm
    bmap = lambda i, j: (i, 0, 0)
    const = lambda i, j: (0, 0)
    return pl.pallas_call(
        _outproj_kernel,
        grid=(b, nb),
        in_specs=[pl.BlockSpec((tm, d), lambda i, j: (i * nb + j, 0)),
                  pl.BlockSpec((d, d), const),
                  pl.BlockSpec((1, tm, d), lambda i, j: (i, j, 0)),
                  pl.BlockSpec((1, 1, d), bmap),
                  pl.BlockSpec((1, d), const),
                  pl.BlockSpec((1, d), const),
                  pl.BlockSpec((1, 1, d), bmap),
                  pl.BlockSpec((1, 1, d), bmap)],
        out_specs=[pl.BlockSpec((1, tm, d), lambda i, j: (i, j, 0)),
                   pl.BlockSpec((tm, d), lambda i, j: (i * nb + j, 0)),
                   pl.BlockSpec((d, tm), lambda i, j: (0, i * nb + j))],
        out_shape=[jax.ShapeDtypeStruct((b, s, d), F32),
                   jax.ShapeDtypeStruct((b * s, d), BF),
                   jax.ShapeDtypeStruct((d, b * s), BF)],
        compiler_params=_params(2),
        name="out_proj_deepnorm",
    )(merged, w_out, x, g1, ln_g, ln_b, sh2, sc2)


N_HP = 2 * PEER_HEADS
NOT_TOP = 127.0
HALF_K = PEER_TOPK // 2
BF16_ROWS = 16


def _top_values(s, k):
    tops = []
    for _ in range(k):
        m = jnp.max(s, axis=0, keepdims=True)
        tops.append(m)
        s = jnp.where(s == m, -jnp.inf, s)
    return tops


def _top_values_ranked(s, k):
    tops = []
    rank = jnp.full(s.shape, NOT_TOP, F32)
    for r in range(k):
        m = jnp.max(s, axis=0, keepdims=True)
        tops.append(m)
        hit = s == m
        rank = jnp.where(hit, float(r), rank)
        s = jnp.where(hit, -jnp.inf, s)
    return tops, rank


def _select_kernel(wq_ref, keys_ref, h_ref, rank_ref, cnt_ref, e1_ref, e2_ref,
                   s_scr, top_scr):
    q = _dot(h_ref[...], wq_ref[...]).astype(BF)
    for hp in range(N_HP):
        s_scr[hp] = lax.dot_general(keys_ref[hp], q[:, hp * PEER_HALF:(hp + 1) * PEER_HALF],
                                    (((1,), (1,)), ((), ())), preferred_element_type=F32)

    def head_body(h, carry):
        s1 = s_scr[2 * h]
        s2 = s_scr[2 * h + 1]
        tops1 = _top_values(s1, PEER_TOPK + 1)
        tops2, rank2 = _top_values_ranked(s2, PEER_TOPK + 1)
        for r in range(PEER_TOPK):
            top_scr[0, r:r + 1, :] = tops1[r]
            top_scr[1, r:r + 1, :] = tops2[r]
        t1 = top_scr[0]
        t2 = top_scr[1]
        m1, m2 = tops1[0], tops2[0]
        cand = jnp.concatenate(
            [m1 + t2]
            + [tops1[a] + t2[:HALF_K] for a in range(1, HALF_K)]
            + [t1[HALF_K:] + m2], axis=0)
        best = _top_values(cand, PEER_TOPK + 1)
        outside = jnp.maximum(tops1[PEER_TOPK] + m2, m1 + tops2[PEER_TOPK])
        runner_up = jnp.maximum(best[PEER_TOPK], outside)
        tau = 0.5 * (best[PEER_TOPK - 1] + runner_up)
        sel = cand >= tau
        z = jnp.sum(jnp.where(sel, jnp.exp(cand - (m1 + m2)), 0.0), axis=0, keepdims=True)
        self32 = sel.astype(F32)
        counts = [jnp.sum(self32[:PEER_TOPK], axis=0, keepdims=True)]
        for a in range(1, HALF_K):
            lo = PEER_TOPK + (a - 1) * HALF_K
            counts.append(jnp.sum(self32[lo:lo + HALF_K], axis=0, keepdims=True))
        lo = PEER_TOPK + (HALF_K - 1) * HALF_K
        for a in range(HALF_K, PEER_TOPK):
            counts.append(self32[lo + a - HALF_K:lo + a - HALF_K + 1])
        cnt = jnp.zeros_like(s1)
        for a in range(PEER_TOPK):
            cnt = jnp.where(s1 == tops1[a], counts[a], cnt)
        rank_ref[h] = rank2.astype(BF)
        cnt_ref[h] = cnt
        e1_ref[h] = jnp.exp(s1 - m1) / z
        e2_ref[h] = jnp.exp(s2 - m2).astype(BF)
        return carry
    lax.fori_loop(0, PEER_HEADS, head_body, 0)


def _select_call(wq, keys, h, tn):
    t, d = h.shape
    shape = (PEER_HEADS, N_KEYS, t)
    ospec = pl.BlockSpec((PEER_HEADS, N_KEYS, tn), lambda i: (0, 0, i))
    return pl.pallas_call(
        _select_kernel,
        grid=(t // tn,),
        in_specs=[pl.BlockSpec(wq.shape, lambda i: (0, 0)),
                  pl.BlockSpec(keys.shape, lambda i: (0, 0, 0)),
                  pl.BlockSpec((tn, d), lambda i: (i, 0))],
        out_specs=[ospec, ospec, ospec, ospec],
        out_shape=[jax.ShapeDtypeStruct(shape, BF), jax.ShapeDtypeStruct(shape, F32),
                   jax.ShapeDtypeStruct(shape, F32), jax.ShapeDtypeStruct(shape, BF)],
        scratch_shapes=[pltpu.VMEM((N_HP, N_KEYS, tn), F32),
                        pltpu.VMEM((2, PEER_TOPK, tn), F32)],
        compiler_params=_params(1),
        name="peer_select",
    )(wq, keys, h)


def _expert_prep_kernel(u_ref, v_ref, ub_ref, vt_ref):
    ub_ref[...] = u_ref[...].astype(BF)
    vt_ref[...] = v_ref[...].T.astype(BF)


def _expert_prep_call(u, v, te):
    e, d = u.shape
    return pl.pallas_call(
        _expert_prep_kernel,
        grid=(e // te,),
        in_specs=[pl.BlockSpec((te, d), lambda i: (i, 0)),
                  pl.BlockSpec((te, d), lambda i: (i, 0))],
        out_specs=[pl.BlockSpec((te, d), lambda i: (i, 0)),
                   pl.BlockSpec((d, te), lambda i: (0, i))],
        out_shape=[jax.ShapeDtypeStruct((e, d), BF), jax.ShapeDtypeStruct((d, e), BF)],
        compiler_params=_params(1),
        name="expert_table_layout",
    )(u, v)


def _gelu(x):
    return 0.5 * x * (1.0 + lax.erf(x * math.sqrt(0.5)))


def _peer_kernel(u_ref, vt_ref, ht_ref, rank_ref, cnt_ref, e1_ref, e2_ref, o_ref,
                 a_scr, c_scr, *, rows, act_slices, mix_slices):
    e = pl.program_id(1)
    tn = ht_ref.shape[1]
    d = vt_ref.shape[0]
    half = rows // 2
    hrows = half * N_KEYS

    @pl.when(e == 0)
    def _():
        o_ref[...] = jnp.zeros_like(o_ref)

    def coef_row(r):
        i1 = e * rows + r
        w = None
        for h in range(PEER_HEADS):
            cnt = jnp.broadcast_to(cnt_ref[h, pl.ds(i1, 1), :], (BF16_ROWS, tn)).astype(BF)
            e1 = jnp.broadcast_to(e1_ref[h, pl.ds(i1, 1), :], (BF16_ROWS, tn)).astype(BF)
            contrib = jnp.where(rank_ref[h] < cnt[None], e2_ref[h] * e1[None],
                                jnp.zeros((), BF))
            w = contrib if w is None else w + contrib
        act = _gelu(a_scr[r * N_KEYS:(r + 1) * N_KEYS, :]).astype(BF)
        c_scr[r * N_KEYS:(r + 1) * N_KEYS, :] = w.reshape(N_KEYS, tn) * act

    ht = ht_ref[...]
    a_scr[0:hrows, :] = _dot(u_ref[0:hrows, :], ht)
    srows = hrows // act_slices
    for j in range(act_slices):
        lo = hrows + j * srows
        a_scr[lo:lo + srows, :] = _dot(u_ref[lo:lo + srows, :], ht)
        for r in range(j * half // act_slices, (j + 1) * half // act_slices):
            coef_row(r)
    mrows = d // mix_slices
    c_a = c_scr[0:hrows, :]
    for j in range(mix_slices):
        o_ref[j * mrows:(j + 1) * mrows, :] += _dot(vt_ref[j * mrows:(j + 1) * mrows, 0:hrows], c_a)
        for r in range(j * half // mix_slices, (j + 1) * half // mix_slices):
            coef_row(half + r)
    o_ref[...] += _dot(vt_ref[:, hrows:], c_scr[hrows:, :])


def _peer_call(u, v_t, h_t, rank2, cnt, e1, e2, tn, te, act_slices, mix_slices):
    d, t = h_t.shape
    rows = te // N_KEYS
    groups = N_KEYS // BF16_ROWS
    rank4 = rank2.reshape(PEER_HEADS, groups, BF16_ROWS, t)
    e24 = e2.reshape(PEER_HEADS, groups, BF16_ROWS, t)
    sel3 = pl.BlockSpec((PEER_HEADS, N_KEYS, tn), lambda i, e: (0, 0, i))
    sel4 = pl.BlockSpec((PEER_HEADS, groups, BF16_ROWS, tn), lambda i, e: (0, 0, 0, i))
    return pl.pallas_call(
        functools.partial(_peer_kernel, rows=rows, act_slices=act_slices,
                          mix_slices=mix_slices),
        grid=(t // tn, N_EXPERTS // te),
        in_specs=[pl.BlockSpec((te, d), lambda i, e: (e, 0)),
                  pl.BlockSpec((d, te), lambda i, e: (0, e)),
                  pl.BlockSpec((d, tn), lambda i, e: (0, i)),
                  sel4, sel3, sel3, sel4],
        out_specs=pl.BlockSpec((d, tn), lambda i, e: (0, i)),
        out_shape=jax.ShapeDtypeStruct((d, t), F32),
        scratch_shapes=[pltpu.VMEM((te, tn), F32), pltpu.VMEM((te, tn), BF)],
        compiler_params=_params(2),
        name="peer_dense",
    )(u, v_t, h_t, rank4, cnt, e1, e24)


def _final_kernel(yt_ref, x_ref, g2_ref, lg_ref, lb_ref, o_ref):
    z = DEEPNORM_ALPHA * x_ref[0] + g2_ref[0] * yt_ref[...].T
    o_ref[0] = _layer_norm_rows(z) * lg_ref[...] + lb_ref[...]


def _final_call(y_t, x1, g2, ln_g, ln_b, tm):
    b, s, d = x1.shape
    nb = s // tm
    return pl.pallas_call(
        _final_kernel,
        grid=(b, nb),
        in_specs=[pl.BlockSpec((d, tm), lambda i, j: (0, i * nb + j)),
                  pl.BlockSpec((1, tm, d), lambda i, j: (i, j, 0)),
                  pl.BlockSpec((1, 1, d), lambda i, j: (i, 0, 0)),
                  pl.BlockSpec((1, d), lambda i, j: (0, 0)),
                  pl.BlockSpec((1, d), lambda i, j: (0, 0))],
        out_specs=pl.BlockSpec((1, tm, d), lambda i, j: (i, j, 0)),
        out_shape=jax.ShapeDtypeStruct((b, s, d), F32),
        compiler_params=_params(2),
        name="final_deepnorm",
    )(y_t, x1, g2, ln_g, ln_b)


def _rope_rotation(w):
    pairs = w.reshape(w.shape[:-1] + (w.shape[-1] // 2, 2))
    return jnp.stack([-pairs[..., 1], pairs[..., 0]], axis=-1).reshape(w.shape)


def _pad_lanes(w):
    return jnp.pad(w, [(0, 0)] * (w.ndim - 1) + [(0, LANES - w.shape[-1])])


def _rope_tables(seq):
    rows = seq // GRID_W
    row = jnp.repeat(jnp.arange(rows, dtype=F32), GRID_W)
    col = jnp.tile(jnp.arange(GRID_W, dtype=F32), rows)
    half = QK_ROPE // 2
    inv = ROPE_THETA ** (-jnp.arange(0, half, 2, dtype=F32) / half)
    ang = jnp.concatenate([row[:, None] * inv, col[:, None] * inv], axis=-1)
    cos = _pad_lanes(jnp.repeat(jnp.cos(ang), 2, axis=-1))
    sin = _pad_lanes(jnp.repeat(jnp.sin(ang), 2, axis=-1))
    return cos, sin


def _dft_matrices(n, scale):
    k = np.arange(n, dtype=np.int64)
    ang = 2.0 * np.pi * ((k[:, None] * k[None, :]) % n).astype(np.float64) / n
    return np.cos(ang) * scale, np.sin(ang) * scale


def kernel(x, c, ctx, c_ctx, w_mod, b_mod, w_in, b_in, q_norm_g, w_uq, kv_norm_g, w_ukv,
           w_o_mla, w_fourier, w_out, ln1_g, ln1_b, peer_wq, peer_keys, peer_u, peer_v,
           ln2_g, ln2_b):
    B, S, D = x.shape
    T = B * S
    CT = ctx.shape[1]
    l = 0

    cmat = jnp.concatenate([c, c_ctx[None, :], jnp.zeros((8 - B - 1, D), F32)], axis=0)
    mod = _mod_call(cmat, w_mod[l], b_mod[l])
    mx = mod[:B].reshape(B, 1, 6, D)
    sh1, sc1, g1, sh2, sc2, g2 = [mx[:, :, i, :] for i in range(6)]
    mc = mod[B].reshape(1, 1, 6, D)
    sh1c, sc1c = mc[:, :, 0, :], mc[:, :, 1, :]

    wt, bi = w_in[l].T, b_in[l]
    w_kr, b_kr = wt[KV_LORA:KV_END].T, bi[KV_LORA:KV_END]
    q0 = KV_END
    f0 = KV_END + Q_LORA
    g0 = f0 + FOURIER_DIM
    w_all = jnp.concatenate(
        [wt[g0:], wt[f0:g0],
         wt[:KV_LORA], _pad_lanes(w_kr).T, _pad_lanes(_rope_rotation(w_kr)).T, wt[q0:f0]],
        axis=0).astype(BF)
    b_all = jnp.concatenate(
        [bi[g0:], bi[f0:g0],
         bi[:KV_LORA], _pad_lanes(b_kr), _pad_lanes(_rope_rotation(b_kr)), bi[q0:f0]])[None, :]
    wq3 = w_uq[l].reshape(Q_LORA, N_HEADS, QK_NOPE + QK_ROPE)
    wq_rope = wq3[:, :, QK_NOPE:]
    w_q = jnp.concatenate(
        [wq3[:, :, :QK_NOPE], _pad_lanes(wq_rope), _pad_lanes(_rope_rotation(wq_rope))],
        axis=-1).reshape(Q_LORA, N_HEADS * Q_HEAD_COLS).astype(BF)
    w_kv = w_ukv[l].astype(BF)
    gkv = kv_norm_g[l][None, :]
    gq = q_norm_g[l][None, :]

    cos, sin = _rope_tables(S)
    cos_c = _pad_lanes(jnp.ones((CT, QK_ROPE), F32))
    sin_c = jnp.zeros((CT, LANES), F32)
    dc_c, dc_s = _dft_matrices(FOURIER_GROUP_DIM, FOURIER_GROUP_DIM ** -0.5)
    dc = jnp.asarray(np.concatenate([dc_c, dc_s], axis=1), dtype=F32).astype(BF)
    ds_c, ds_s = _dft_matrices(S, S ** -0.5)
    cs = jnp.asarray(ds_c, dtype=F32).astype(BF)
    ss = jnp.asarray(ds_s, dtype=F32).astype(BF)

    hx = _ln_mod_call(x, sh1, sc1, 512)
    hc = _ln_mod_call(ctx, sh1c, sc1c, CT)
    ckv_x, kr_x, cq_x = _latent_call(hx, w_all, b_all, gkv, gq, cos, sin, 512)
    ckv_c, kr_c, _ = _latent_call(hc, w_all, b_all, gkv, gq, cos_c, sin_c, CT)
    lat = jnp.concatenate([ckv_c, ckv_x], axis=1)
    kr = jnp.concatenate([kr_c, kr_x], axis=1)
    k_all, v_all = _kv_up_call(lat, kr, w_kv, 768)
    q_all = _q_up_call(cq_x, w_q, cos, sin, 512)
    attn = _attn_call(q_all, k_all, v_all, 1024, 256)

    hx2d = hx.reshape(T, D)
    gc, gs = _fproj_call(hx2d, w_all, b_all, dc, 512)
    fm = _pos_dft_call(cs, ss, gc.reshape(B, S, FOURIER_DIM), gs.reshape(B, S, FOURIER_DIM),
                       512, 512)
    gates = _gate_call(hx2d, w_all, b_all, 512, 1024)
    merged = _merge_call(attn.reshape(T, D), fm.reshape(T, FOURIER_DIM),
                         w_o_mla[l].astype(BF), w_fourier[l].astype(BF), gates, 512, 512)
    x1, h2, h2_t = _outproj_call(merged, w_out[l].astype(BF), x, g1, ln1_g[l][None, :],
                                 ln1_b[l][None, :], sh2, sc2, 512)

    keys = peer_keys[l].reshape(N_HP, N_KEYS, PEER_HALF).astype(BF)
    rank2, cnt, e1, e2 = _select_call(peer_wq[l].astype(BF), keys, h2, 512)
    u_b, v_t = _expert_prep_call(peer_u[l], peer_v[l], 512)
    y_t = _peer_call(u_b, v_t, h2_t, rank2, cnt, e1, e2, 512, 1024, 2, 4)
    return _final_call(y_t, x1, g2, ln2_g[l][None, :], ln2_b[l][None, :], 256)
```

```python
import functools
import math

import numpy as np
import jax
import jax.numpy as jnp
from jax import lax
from jax.experimental import pallas as pl
from jax.experimental.pallas import tpu as pltpu

D_MODEL = 2048
GRID_W = 64
N_HEADS = 16
QK_NOPE = 128
QK_ROPE = 64
V_DIM = 128
Q_LORA = 512
KV_LORA = 512
ROPE_THETA = 10000.0
N_FOURIER_GROUPS = 4
FOURIER_GROUP_DIM = 256
FOURIER_DIM = N_FOURIER_GROUPS * FOURIER_GROUP_DIM
KV_END = KV_LORA + QK_ROPE
PEER_HEADS = 8
N_KEYS = 128
N_EXPERTS = N_KEYS * N_KEYS
PEER_HALF = 128
PEER_TOPK = 16
DEPTH = 1
DEEPNORM_ALPHA = (2.0 * DEPTH) ** 0.25
EPS = 1e-6

LANES = 128
QK_PAD = 2 * LANES
VMEM_LIMIT = 56 * 1024 * 1024

BF = jnp.bfloat16
F32 = jnp.float32


def _params(n_axes, vmem=VMEM_LIMIT):
    return pltpu.CompilerParams(
        dimension_semantics=("arbitrary",) * n_axes, vmem_limit_bytes=vmem)


def _dot(a, b):
    return jnp.dot(a, b, preferred_element_type=F32)


def _dot_nt(a, b):
    return lax.dot_general(a, b, (((1,), (1,)), ((), ())), preferred_element_type=F32)


def _layer_norm_rows(x):
    mu = jnp.mean(x, axis=-1, keepdims=True)
    xc = x - mu
    var = jnp.mean(xc * xc, axis=-1, keepdims=True)
    return xc * lax.rsqrt(var + EPS)


def _mod_kernel(c_ref, w_ref, b_ref, o_ref):
    a = jax.nn.silu(c_ref[...]).astype(BF)
    o_ref[...] = _dot(a, w_ref[...].astype(BF)) + b_ref[...]


def _mod_call(cmat, w_mod, b_mod):
    n = w_mod.shape[1]
    tn = 1024
    return pl.pallas_call(
        _mod_kernel,
        grid=(n // tn,),
        in_specs=[pl.BlockSpec((8, D_MODEL), lambda j: (0, 0)),
                  pl.BlockSpec((D_MODEL, tn), lambda j: (0, j)),
                  pl.BlockSpec((1, tn), lambda j: (0, j))],
        out_specs=pl.BlockSpec((8, tn), lambda j: (0, j)),
        out_shape=jax.ShapeDtypeStruct((8, n), F32),
        compiler_params=_params(1),
        name="adaln_mod",
    )(cmat, w_mod, b_mod.reshape(1, n))


def _ln_mod_kernel(x_ref, sh_ref, sc_ref, o_ref):
    y = _layer_norm_rows(x_ref[0])
    o_ref[0] = (y * (1.0 + sc_ref[0]) + sh_ref[0]).astype(BF)


def _ln_mod_call(x, shift, scale, tm):
    b, s, d = x.shape
    bm = shift.shape[0]
    mod_map = (lambda i, j: (i, 0, 0)) if bm == b else (lambda i, j: (0, 0, 0))
    return pl.pallas_call(
        _ln_mod_kernel,
        grid=(b, s // tm),
        in_specs=[pl.BlockSpec((1, tm, d), lambda i, j: (i, j, 0)),
                  pl.BlockSpec((1, 1, d), mod_map),
                  pl.BlockSpec((1, 1, d), mod_map)],
        out_specs=pl.BlockSpec((1, tm, d), lambda i, j: (i, j, 0)),
        out_shape=jax.ShapeDtypeStruct((b, s, d), BF),
        compiler_params=_params(2),
        name="ln_modulate",
    )(x, shift, scale)


LAT_COLS = KV_LORA + 2 * LANES + Q_LORA
GATE_COLS = 2 * D_MODEL
FOURIER_BLOCK = GATE_COLS // FOURIER_DIM
LAT_BLOCK = (GATE_COLS + FOURIER_DIM) // LAT_COLS
assert GATE_COLS % FOURIER_DIM == 0 and (GATE_COLS + FOURIER_DIM) % LAT_COLS == 0


def _latent_kernel(h_ref, w_ref, b_ref, gkv_ref, gq_ref, cos_ref, sin_ref,
                   ckv_ref, kr_ref, cq_ref):
    acc = _dot_nt(h_ref[0], w_ref[...]) + b_ref[...]
    ckv = acc[:, :KV_LORA]
    ka = acc[:, KV_LORA:KV_LORA + LANES]
    kb = acc[:, KV_LORA + LANES:KV_LORA + 2 * LANES]
    cq = acc[:, KV_LORA + 2 * LANES:]
    ckv_n = ckv * lax.rsqrt(jnp.mean(ckv * ckv, axis=-1, keepdims=True) + EPS)
    cq_n = cq * lax.rsqrt(jnp.mean(cq * cq, axis=-1, keepdims=True) + EPS)
    ckv_ref[0] = (ckv_n * gkv_ref[...]).astype(BF)
    cq_ref[0] = (cq_n * gq_ref[...]).astype(BF)
    kr_ref[0] = (ka * cos_ref[...] + kb * sin_ref[...]).astype(BF)


def _latent_call(h, w_all, b_all, gkv, gq, cos, sin, tm):
    b, s, d = h.shape
    row = lambda i, j: (i, j, 0)
    const = lambda i, j: (0, 0)
    return pl.pallas_call(
        _latent_kernel,
        grid=(b, s // tm),
        in_specs=[pl.BlockSpec((1, tm, d), row),
                  pl.BlockSpec((LAT_COLS, d), lambda i, j: (LAT_BLOCK, 0)),
                  pl.BlockSpec((1, LAT_COLS), lambda i, j: (0, LAT_BLOCK)),
                  pl.BlockSpec((1, KV_LORA), const),
                  pl.BlockSpec((1, Q_LORA), const),
                  pl.BlockSpec((tm, LANES), lambda i, j: (j, 0)),
                  pl.BlockSpec((tm, LANES), lambda i, j: (j, 0))],
        out_specs=[pl.BlockSpec((1, tm, KV_LORA), row),
                   pl.BlockSpec((1, tm, LANES), row),
                   pl.BlockSpec((1, tm, Q_LORA), row)],
        out_shape=[jax.ShapeDtypeStruct((b, s, KV_LORA), BF),
                   jax.ShapeDtypeStruct((b, s, LANES), BF),
                   jax.ShapeDtypeStruct((b, s, Q_LORA), BF)],
        compiler_params=_params(2),
        name="latent_proj",
    )(h, w_all, b_all, gkv, gq, cos, sin)


def _fproj_kernel(h_ref, w_ref, b_ref, dc_ref, gc_ref, gs_ref):
    f = (_dot_nt(h_ref[...], w_ref[...]) + b_ref[...]).astype(BF)
    for g in range(N_FOURIER_GROUPS):
        lo = g * FOURIER_GROUP_DIM
        r = _dot(f[:, lo:lo + FOURIER_GROUP_DIM], dc_ref[...])
        gc_ref[:, lo:lo + FOURIER_GROUP_DIM] = r[:, :FOURIER_GROUP_DIM].astype(BF)
        gs_ref[:, lo:lo + FOURIER_GROUP_DIM] = r[:, FOURIER_GROUP_DIM:].astype(BF)


def _fproj_call(h2d, w_all, b_all, dc, tm):
    t, d = h2d.shape
    const = lambda i: (0, 0)
    return pl.pallas_call(
        _fproj_kernel,
        grid=(t // tm,),
        in_specs=[pl.BlockSpec((tm, d), lambda i: (i, 0)),
                  pl.BlockSpec((FOURIER_DIM, d), lambda i: (FOURIER_BLOCK, 0)),
                  pl.BlockSpec((1, FOURIER_DIM), lambda i: (0, FOURIER_BLOCK)),
                  pl.BlockSpec((FOURIER_GROUP_DIM, 2 * FOURIER_GROUP_DIM), const)],
        out_specs=[pl.BlockSpec((tm, FOURIER_DIM), lambda i: (i, 0)),
                   pl.BlockSpec((tm, FOURIER_DIM), lambda i: (i, 0))],
        out_shape=[jax.ShapeDtypeStruct((t, FOURIER_DIM), BF),
                   jax.ShapeDtypeStruct((t, FOURIER_DIM), BF)],
        compiler_params=_params(1),
        name="fourier_in_proj",
    )(h2d, w_all, b_all, dc)


def _gate_kernel(h_ref, w_ref, b_ref, u_ref, o_ref, ub_ref):
    o_ref[...] = jax.nn.sigmoid(_dot_nt(h_ref[...], w_ref[...]) + b_ref[...]).astype(BF)
    ub_ref[...] = u_ref[...].astype(BF)


def _gate_call(h2d, w_all, b_all, u, tm, tn):
    t, d = h2d.shape
    n = GATE_COLS
    mb = t // tm
    ue = u.shape[0] // ((n // tn) * mb)
    slab = lambda j, i: (j * mb + i, 0)
    return pl.pallas_call(
        _gate_kernel,
        grid=(n // tn, mb),
        in_specs=[pl.BlockSpec((tm, d), lambda j, i: (i, 0)),
                  pl.BlockSpec((tn, d), lambda j, i: (j, 0)),
                  pl.BlockSpec((1, tn), lambda j, i: (0, j)),
                  pl.BlockSpec((ue, d), slab)],
        out_specs=[pl.BlockSpec((tm, tn), lambda j, i: (i, j)),
                   pl.BlockSpec((ue, d), slab)],
        out_shape=[jax.ShapeDtypeStruct((t, n), BF), jax.ShapeDtypeStruct(u.shape, BF)],
        compiler_params=_params(2),
        name="gate_proj",
    )(h2d, w_all, b_all, u)


def _kv_up_kernel(lat_ref, kr_ref, w_ref, k_ref, v_ref):
    lat = lat_ref[0]
    kr = kr_ref[0]
    for h in range(N_HEADS):
        lo = h * (QK_NOPE + V_DIM)
        kv = _dot(lat, w_ref[:, lo:lo + QK_NOPE + V_DIM])
        k_ref[0, h] = jnp.concatenate([kv[:, :QK_NOPE].astype(BF), kr], axis=-1)
        v_ref[0, h] = kv[:, QK_NOPE:].astype(BF)


def _kv_up_call(lat, kr, w_ukv, tm):
    b, t, _ = lat.shape
    return pl.pallas_call(
        _kv_up_kernel,
        grid=(b, t // tm),
        in_specs=[pl.BlockSpec((1, tm, KV_LORA), lambda i, j: (i, j, 0)),
                  pl.BlockSpec((1, tm, LANES), lambda i, j: (i, j, 0)),
                  pl.BlockSpec(w_ukv.shape, lambda i, j: (0, 0))],
        out_specs=[pl.BlockSpec((1, N_HEADS, tm, QK_PAD), lambda i, j: (i, 0, j, 0)),
                   pl.BlockSpec((1, N_HEADS, tm, V_DIM), lambda i, j: (i, 0, j, 0))],
        out_shape=[jax.ShapeDtypeStruct((b, N_HEADS, t, QK_PAD), BF),
                   jax.ShapeDtypeStruct((b, N_HEADS, t, V_DIM), BF)],
        compiler_params=_params(2),
        name="kv_up_proj",
    )(lat, kr, w_ukv)


Q_HEAD_COLS = 3 * LANES


def _q_up_kernel(cq_ref, w_ref, cos_ref, sin_ref, q_ref):
    cq = cq_ref[0]
    cos = cos_ref[...]
    sin = sin_ref[...]
    scale = (QK_NOPE + QK_ROPE) ** -0.5
    for h in range(N_HEADS):
        lo = h * Q_HEAD_COLS
        acc = _dot(cq, w_ref[:, lo:lo + Q_HEAD_COLS])
        qn = acc[:, :LANES]
        qr = acc[:, LANES:2 * LANES] * cos + acc[:, 2 * LANES:] * sin
        q_ref[0, h] = (jnp.concatenate([qn, qr], axis=-1) * scale).astype(BF)


def _q_up_call(cq, w_q, cos, sin, tm):
    b, s, _ = cq.shape
    return pl.pallas_call(
        _q_up_kernel,
        grid=(b, s // tm),
        in_specs=[pl.BlockSpec((1, tm, Q_LORA), lambda i, j: (i, j, 0)),
                  pl.BlockSpec(w_q.shape, lambda i, j: (0, 0)),
                  pl.BlockSpec((tm, LANES), lambda i, j: (j, 0)),
                  pl.BlockSpec((tm, LANES), lambda i, j: (j, 0))],
        out_specs=pl.BlockSpec((1, N_HEADS, tm, QK_PAD), lambda i, j: (i, 0, j, 0)),
        out_shape=jax.ShapeDtypeStruct((b, N_HEADS, s, QK_PAD), BF),
        compiler_params=_params(2),
        name="q_up_proj",
    )(cq, w_q, cos, sin)


ATTN_LAG = 1


def _attn_kernel(q_ref, k_ref, v_ref, o_ref, s_scr, m_scr, *, kc):
    n = pl.program_id(0)
    tq = q_ref.shape[2]
    t = k_ref.shape[2]

    @pl.when(n == 0)
    def _():
        s_scr[...] = jnp.zeros_like(s_scr)
        m_scr[...] = jnp.zeros_like(m_scr)

    def step(cur, prev):
        q = q_ref[0, 0]
        m_prev = m_scr[prev]
        mrun = None
        lrun = jnp.zeros((tq, LANES), F32)
        acc = jnp.zeros((tq, V_DIM), F32)
        for c in range(t // kc):
            ks = slice(c * kc, (c + 1) * kc)
            s_c = lax.dot_general(q, k_ref[0, 0, ks, :], (((1,), (1,)), ((), ())),
                                  preferred_element_type=F32)
            s_scr[cur, :, ks] = s_c
            pieces = []
            for j in range(kc // LANES):
                lanes = slice(j * LANES, (j + 1) * LANES)
                col = slice(c * kc + j * LANES, c * kc + (j + 1) * LANES)
                p_j = jnp.exp(s_scr[prev, :, col] - m_prev)
                lrun = lrun + p_j
                pieces.append(p_j.astype(BF))
                mrun = s_c[:, lanes] if mrun is None else jnp.maximum(mrun, s_c[:, lanes])
            acc = acc + _dot(jnp.concatenate(pieces, axis=-1), v_ref[0, 0, ks, :])
        o_ref[0] = (acc / jnp.sum(lrun, axis=-1, keepdims=True)).astype(BF)
        m_scr[cur] = jnp.broadcast_to(jnp.max(mrun, axis=-1, keepdims=True), (tq, LANES))

    @pl.when(n % 2 == 0)
    def _():
        step(0, 1)

    @pl.when(n % 2 == 1)
    def _():
        step(1, 0)


def _attn_call(q, k, v, tq, kc):
    b, h, s, _ = q.shape
    t = k.shape[2]
    nq = s // tq
    total = b * h * nq

    def block(n, lag):
        i = jnp.clip(n - lag, 0, total - 1)
        return i // (h * nq), (i // nq) % h, i % nq

    def q_map(n):
        bi, hi, qi = block(n, 0)
        return bi, hi, qi, 0

    def k_map(n):
        bi, hi, _ = block(n, 0)
        return bi, hi, 0, 0

    def v_map(n):
        bi, hi, _ = block(n, ATTN_LAG)
        return bi, hi, 0, 0

    def o_map(n):
        bi, hi, qi = block(n, ATTN_LAG)
        return bi, qi, hi

    return pl.pallas_call(
        functools.partial(_attn_kernel, kc=kc),
        grid=(total + ATTN_LAG,),
        in_specs=[pl.BlockSpec((1, 1, tq, QK_PAD), q_map),
                  pl.BlockSpec((1, 1, t, QK_PAD), k_map),
                  pl.BlockSpec((1, 1, t, V_DIM), v_map)],
        out_specs=pl.BlockSpec((1, tq, V_DIM), o_map),
        out_shape=jax.ShapeDtypeStruct((b, s, h * V_DIM), BF),
        scratch_shapes=[pltpu.VMEM((2, tq, t), F32), pltpu.VMEM((2, tq, LANES), F32)],
        compiler_params=_params(1),
        name="mla_attention",
    )(q, k, v)


def _pos_dft_kernel(c_ref, s_ref, gc_ref, gs_ref, o_ref):
    o_ref[0] = (_dot(c_ref[...], gc_ref[0]) - _dot(s_ref[...], gs_ref[0])).astype(BF)


def _pos_dft_call(cs, ss, gc, gs, tm, tn):
    b, s, n = gc.shape
    return pl.pallas_call(
        _pos_dft_kernel,
        grid=(b, n // tn, s // tm),
        in_specs=[pl.BlockSpec((tm, s), lambda i, j, m: (m, 0)),
                  pl.BlockSpec((tm, s), lambda i, j, m: (m, 0)),
                  pl.BlockSpec((1, s, tn), lambda i, j, m: (i, 0, j)),
                  pl.BlockSpec((1, s, tn), lambda i, j, m: (i, 0, j))],
        out_specs=pl.BlockSpec((1, tm, tn), lambda i, j, m: (i, m, j)),
        out_shape=jax.ShapeDtypeStruct((b, s, n), BF),
        compiler_params=_params(3),
        name="position_dft",
    )(cs, ss, gc, gs)


def _merge_kernel(a_ref, f_ref, wo_ref, wf_ref, ga_ref, gb_ref, v_ref, o_ref, vt_ref):
    ya = _dot(a_ref[...], wo_ref[...])
    yb = _dot(f_ref[...], wf_ref[...])
    o_ref[...] = (ga_ref[...].astype(F32) * ya + gb_ref[...].astype(F32) * yb).astype(BF)
    vt_ref[...] = v_ref[...].T.astype(BF)


def _merge_call(attn, fm, w_o, w_f, gates, v, tm, tn):
    t, d = attn.shape
    nb = D_MODEL // tn
    mb = t // tm
    ve = v.shape[0] // (nb * mb)
    return pl.pallas_call(
        _merge_kernel,
        grid=(nb, mb),
        in_specs=[pl.BlockSpec((tm, d), lambda j, i: (i, 0)),
                  pl.BlockSpec((tm, FOURIER_DIM), lambda j, i: (i, 0)),
                  pl.BlockSpec((d, tn), lambda j, i: (0, j)),
                  pl.BlockSpec((FOURIER_DIM, tn), lambda j, i: (0, j)),
                  pl.BlockSpec((tm, tn), lambda j, i: (i, j)),
                  pl.BlockSpec((tm, tn), lambda j, i: (i, j + nb)),
                  pl.BlockSpec((ve, v.shape[1]), lambda j, i: (j * mb + i, 0))],
        out_specs=[pl.BlockSpec((tm, tn), lambda j, i: (i, j)),
                   pl.BlockSpec((v.shape[1], ve), lambda j, i: (0, j * mb + i))],
        out_shape=[jax.ShapeDtypeStruct((t, D_MODEL), BF),
                   jax.ShapeDtypeStruct((v.shape[1], v.shape[0]), BF)],
        compiler_params=_params(2),
        name="branch_merge",
    )(attn, fm, w_o, w_f, gates, gates, v)


def _outproj_kernel(m_ref, w_ref, x_ref, g1_ref, lg_ref, lb_ref, sh_ref, sc_ref,
                    x1_ref, h_ref, ht_ref):
    y = _dot(m_ref[...], w_ref[...])
    z = DEEPNORM_ALPHA * x_ref[0] + g1_ref[0] * y
    x1 = _layer_norm_rows(z) * lg_ref[...] + lb_ref[...]
    x1_ref[0] = x1
    h2 = _layer_norm_rows(x1) * (1.0 + sc_ref[0]) + sh_ref[0]
    h_ref[...] = h2.astype(BF)
    ht_ref[...] = h2.T.astype(BF)


def _outproj_call(merged, w_out, x, g1, ln_g, ln_b, sh2, sc2, tm):
    b, s, d = x.shape
    nb = s // tm
    bmap = lambda i, j: (i, 0, 0)
    const = lambda i, j: (0, 0)
    return pl.pallas_call(
        _outproj_kernel,
        grid=(b, nb),
        in_specs=[pl.BlockSpec((tm, d), lambda i, j: (i * nb + j, 0)),
                  pl.BlockSpec((d, d), const),
                  pl.BlockSpec((1, tm, d), lambda i, j: (i, j, 0)),
                  pl.BlockSpec((1, 1, d), bmap),
                  pl.BlockSpec((1, d), const),
                  pl.BlockSpec((1, d), const),
                  pl.BlockSpec((1, 1, d), bmap),
                  pl.BlockSpec((1, 1, d), bmap)],
        out_specs=[pl.BlockSpec((1, tm, d), lambda i, j: (i, j, 0)),
                   pl.BlockSpec((tm, d), lambda i, j: (i * nb + j, 0)),
                   pl.BlockSpec((d, tm), lambda i, j: (0, i * nb + j))],
        out_shape=[jax.ShapeDtypeStruct((b, s, d), F32),
                   jax.ShapeDtypeStruct((b * s, d), BF),
                   jax.ShapeDtypeStruct((d, b * s), BF)],
        compiler_params=_params(2),
        name="out_proj_deepnorm",
    )(merged, w_out, x, g1, ln_g, ln_b, sh2, sc2)


N_HP = 2 * PEER_HEADS
NOT_TOP = 127.0
HALF_K = PEER_TOPK // 2
BF16_ROWS = 16


def _top_values(s, k):
    tops = []
    for _ in range(k):
        m = jnp.max(s, axis=0, keepdims=True)
        tops.append(m)
        s = jnp.where(s == m, -jnp.inf, s)
    return tops


def _top_values_ranked(s, k):
    tops = []
    rank = jnp.full(s.shape, NOT_TOP, F32)
    for r in range(k):
        m = jnp.max(s, axis=0, keepdims=True)
        tops.append(m)
        hit = s == m
        rank = jnp.where(hit, float(r), rank)
        s = jnp.where(hit, -jnp.inf, s)
    return tops, rank


def _select_kernel(wq_ref, keys_ref, h_ref, rank_ref, cnt_ref, e1_ref, e2_ref,
                   s_scr, top_scr):
    q = _dot(h_ref[...], wq_ref[...]).astype(BF)
    for hp in range(N_HP):
        s_scr[hp] = lax.dot_general(keys_ref[hp], q[:, hp * PEER_HALF:(hp + 1) * PEER_HALF],
                                    (((1,), (1,)), ((), ())), preferred_element_type=F32)

    def head_body(h, carry):
        s1 = s_scr[2 * h]
        s2 = s_scr[2 * h + 1]
        tops1 = _top_values(s1, PEER_TOPK + 1)
        tops2, rank2 = _top_values_ranked(s2, PEER_TOPK + 1)
        for r in range(PEER_TOPK):
            top_scr[0, r:r + 1, :] = tops1[r]
            top_scr[1, r:r + 1, :] = tops2[r]
        t1 = top_scr[0]
        t2 = top_scr[1]
        m1, m2 = tops1[0], tops2[0]
        cand = jnp.concatenate(
            [m1 + t2]
            + [tops1[a] + t2[:HALF_K] for a in range(1, HALF_K)]
            + [t1[HALF_K:] + m2], axis=0)
        best = _top_values(cand, PEER_TOPK + 1)
        outside = jnp.maximum(tops1[PEER_TOPK] + m2, m1 + tops2[PEER_TOPK])
        runner_up = jnp.maximum(best[PEER_TOPK], outside)
        tau = 0.5 * (best[PEER_TOPK - 1] + runner_up)
        sel = cand >= tau
        z = jnp.sum(jnp.where(sel, jnp.exp(cand - (m1 + m2)), 0.0), axis=0, keepdims=True)
        self32 = sel.astype(F32)
        counts = [jnp.sum(self32[:PEER_TOPK], axis=0, keepdims=True)]
        for a in range(1, HALF_K):
            lo = PEER_TOPK + (a - 1) * HALF_K
            counts.append(jnp.sum(self32[lo:lo + HALF_K], axis=0, keepdims=True))
        lo = PEER_TOPK + (HALF_K - 1) * HALF_K
        for a in range(HALF_K, PEER_TOPK):
            counts.append(self32[lo + a - HALF_K:lo + a - HALF_K + 1])
        cnt = jnp.zeros_like(s1)
        for a in range(PEER_TOPK):
            cnt = jnp.where(s1 == tops1[a], counts[a], cnt)
        rank_ref[h] = rank2.astype(BF)
        cnt_ref[h] = cnt
        e1_ref[h] = jnp.exp(s1 - m1) * (0.5 / z)
        e2_ref[h] = jnp.exp(s2 - m2).astype(BF)
        return carry
    lax.fori_loop(0, PEER_HEADS, head_body, 0)


def _select_call(wq, keys, h, tn):
    t, d = h.shape
    shape = (PEER_HEADS, N_KEYS, t)
    ospec = pl.BlockSpec((PEER_HEADS, N_KEYS, tn), lambda i: (0, 0, i))
    return pl.pallas_call(
        _select_kernel,
        grid=(t // tn,),
        in_specs=[pl.BlockSpec(wq.shape, lambda i: (0, 0)),
                  pl.BlockSpec(keys.shape, lambda i: (0, 0, 0)),
                  pl.BlockSpec((tn, d), lambda i: (i, 0))],
        out_specs=[ospec, ospec, ospec, ospec],
        out_shape=[jax.ShapeDtypeStruct(shape, BF), jax.ShapeDtypeStruct(shape, F32),
                   jax.ShapeDtypeStruct(shape, F32), jax.ShapeDtypeStruct(shape, BF)],
        scratch_shapes=[pltpu.VMEM((N_HP, N_KEYS, tn), F32),
                        pltpu.VMEM((2, PEER_TOPK, tn), F32)],
        compiler_params=_params(1),
        name="peer_select",
    )(wq, keys, h)


def _gelu_times_two(x):
    return x * (1.0 + lax.erf(x * math.sqrt(0.5)))


def _peer_kernel(u_ref, vt_ref, ht_ref, rank_ref, cnt_ref, e1_ref, e2_ref, o_ref,
                 a_scr, c_scr, *, rows, act_slices, mix_slices):
    e = pl.program_id(1)
    tn = ht_ref.shape[1]
    d = vt_ref.shape[0]
    half = rows // 2
    hrows = half * N_KEYS

    @pl.when(e == 0)
    def _():
        o_ref[...] = jnp.zeros_like(o_ref)

    def coef_row(r):
        w = None
        for h in range(PEER_HEADS):
            cnt = jnp.broadcast_to(cnt_ref[h, r:r + 1, :], (BF16_ROWS, tn)).astype(BF)
            e1 = jnp.broadcast_to(e1_ref[h, r:r + 1, :], (BF16_ROWS, tn)).astype(BF)
            gate = jnp.where(rank_ref[h] < cnt[None], e1[None], jnp.zeros((), BF))
            w = e2_ref[h] * gate if w is None else w + e2_ref[h] * gate
        act = _gelu_times_two(a_scr[r * N_KEYS:(r + 1) * N_KEYS, :]).astype(BF)
        c_scr[r * N_KEYS:(r + 1) * N_KEYS, :] = w.reshape(N_KEYS, tn) * act

    ht = ht_ref[...]
    a_scr[0:hrows, :] = _dot(u_ref[0:hrows, :], ht)
    srows = hrows // act_slices
    for j in range(act_slices):
        lo = hrows + j * srows
        a_scr[lo:lo + srows, :] = _dot(u_ref[lo:lo + srows, :], ht)
        for r in range(j * half // act_slices, (j + 1) * half // act_slices):
            coef_row(r)
    mrows = d // mix_slices
    c_a = c_scr[0:hrows, :]
    for j in range(mix_slices):
        o_ref[j * mrows:(j + 1) * mrows, :] += _dot(vt_ref[j * mrows:(j + 1) * mrows, 0:hrows], c_a)
        for r in range(j * half // mix_slices, (j + 1) * half // mix_slices):
            coef_row(half + r)
    o_ref[...] += _dot(vt_ref[:, hrows:], c_scr[hrows:, :])


def _peer_call(u, v_t, h_t, rank2, cnt, e1, e2, tn, te, act_slices, mix_slices):
    d, t = h_t.shape
    rows = te // N_KEYS
    groups = N_KEYS // BF16_ROWS
    rank4 = rank2.reshape(PEER_HEADS, groups, BF16_ROWS, t)
    e24 = e2.reshape(PEER_HEADS, groups, BF16_ROWS, t)
    sel3 = pl.BlockSpec((PEER_HEADS, rows, tn), lambda i, e: (0, e, i))
    sel4 = pl.BlockSpec((PEER_HEADS, groups, BF16_ROWS, tn), lambda i, e: (0, 0, 0, i))
    return pl.pallas_call(
        functools.partial(_peer_kernel, rows=rows, act_slices=act_slices,
                          mix_slices=mix_slices),
        grid=(t // tn, N_EXPERTS // te),
        in_specs=[pl.BlockSpec((te, d), lambda i, e: (e, 0)),
                  pl.BlockSpec((d, te), lambda i, e: (0, e)),
                  pl.BlockSpec((d, tn), lambda i, e: (0, i)),
                  sel4, sel3, sel3, sel4],
        out_specs=pl.BlockSpec((d, tn), lambda i, e: (0, i)),
        out_shape=jax.ShapeDtypeStruct((d, t), F32),
        scratch_shapes=[pltpu.VMEM((te, tn), F32), pltpu.VMEM((te, tn), BF)],
        compiler_params=_params(2),
        name="peer_dense",
    )(u, v_t, h_t, rank4, cnt, e1, e24)


def _final_kernel(yt_ref, x_ref, g2_ref, lg_ref, lb_ref, o_ref):
    z = DEEPNORM_ALPHA * x_ref[0] + g2_ref[0] * yt_ref[...].T
    o_ref[0] = _layer_norm_rows(z) * lg_ref[...] + lb_ref[...]


def _final_call(y_t, x1, g2, ln_g, ln_b, tm):
    b, s, d = x1.shape
    nb = s // tm
    return pl.pallas_call(
        _final_kernel,
        grid=(b, nb),
        in_specs=[pl.BlockSpec((d, tm), lambda i, j: (0, i * nb + j)),
                  pl.BlockSpec((1, tm, d), lambda i, j: (i, j, 0)),
                  pl.BlockSpec((1, 1, d), lambda i, j: (i, 0, 0)),
                  pl.BlockSpec((1, d), lambda i, j: (0, 0)),
                  pl.BlockSpec((1, d), lambda i, j: (0, 0))],
        out_specs=pl.BlockSpec((1, tm, d), lambda i, j: (i, j, 0)),
        out_shape=jax.ShapeDtypeStruct((b, s, d), F32),
        compiler_params=_params(2),
        name="final_deepnorm",
    )(y_t, x1, g2, ln_g, ln_b)


def _rope_rotation(w):
    pairs = w.reshape(w.shape[:-1] + (w.shape[-1] // 2, 2))
    return jnp.stack([-pairs[..., 1], pairs[..., 0]], axis=-1).reshape(w.shape)


def _pad_lanes(w):
    return jnp.pad(w, [(0, 0)] * (w.ndim - 1) + [(0, LANES - w.shape[-1])])


def _rope_tables(seq):
    rows = seq // GRID_W
    row = jnp.repeat(jnp.arange(rows, dtype=F32), GRID_W)
    col = jnp.tile(jnp.arange(GRID_W, dtype=F32), rows)
    half = QK_ROPE // 2
    inv = ROPE_THETA ** (-jnp.arange(0, half, 2, dtype=F32) / half)
    ang = jnp.concatenate([row[:, None] * inv, col[:, None] * inv], axis=-1)
    cos = _pad_lanes(jnp.repeat(jnp.cos(ang), 2, axis=-1))
    sin = _pad_lanes(jnp.repeat(jnp.sin(ang), 2, axis=-1))
    return cos, sin


def _dft_matrices(n, scale):
    k = np.arange(n, dtype=np.int64)
    ang = 2.0 * np.pi * ((k[:, None] * k[None, :]) % n).astype(np.float64) / n
    return np.cos(ang) * scale, np.sin(ang) * scale


def kernel(x, c, ctx, c_ctx, w_mod, b_mod, w_in, b_in, q_norm_g, w_uq, kv_norm_g, w_ukv,
           w_o_mla, w_fourier, w_out, ln1_g, ln1_b, peer_wq, peer_keys, peer_u, peer_v,
           ln2_g, ln2_b):
    B, S, D = x.shape
    T = B * S
    CT = ctx.shape[1]
    l = 0

    cmat = jnp.concatenate([c, c_ctx[None, :], jnp.zeros((8 - B - 1, D), F32)], axis=0)
    mod = _mod_call(cmat, w_mod[l], b_mod[l])
    mx = mod[:B].reshape(B, 1, 6, D)
    sh1, sc1, g1, sh2, sc2, g2 = [mx[:, :, i, :] for i in range(6)]
    mc = mod[B].reshape(1, 1, 6, D)
    sh1c, sc1c = mc[:, :, 0, :], mc[:, :, 1, :]

    wt, bi = w_in[l].T, b_in[l]
    w_kr, b_kr = wt[KV_LORA:KV_END].T, bi[KV_LORA:KV_END]
    q0 = KV_END
    f0 = KV_END + Q_LORA
    g0 = f0 + FOURIER_DIM
    w_all = jnp.concatenate(
        [wt[g0:], wt[f0:g0],
         wt[:KV_LORA], _pad_lanes(w_kr).T, _pad_lanes(_rope_rotation(w_kr)).T, wt[q0:f0]],
        axis=0).astype(BF)
    b_all = jnp.concatenate(
        [bi[g0:], bi[f0:g0],
         bi[:KV_LORA], _pad_lanes(b_kr), _pad_lanes(_rope_rotation(b_kr)), bi[q0:f0]])[None, :]
    wq3 = w_uq[l].reshape(Q_LORA, N_HEADS, QK_NOPE + QK_ROPE)
    wq_rope = wq3[:, :, QK_NOPE:]
    w_q = jnp.concatenate(
        [wq3[:, :, :QK_NOPE], _pad_lanes(wq_rope), _pad_lanes(_rope_rotation(wq_rope))],
        axis=-1).reshape(Q_LORA, N_HEADS * Q_HEAD_COLS).astype(BF)
    w_kv = w_ukv[l].astype(BF)
    gkv = kv_norm_g[l][None, :]
    gq = q_norm_g[l][None, :]

    cos, sin = _rope_tables(S)
    cos_c = _pad_lanes(jnp.ones((CT, QK_ROPE), F32))
    sin_c = jnp.zeros((CT, LANES), F32)
    dc_c, dc_s = _dft_matrices(FOURIER_GROUP_DIM, FOURIER_GROUP_DIM ** -0.5)
    dc = jnp.asarray(np.concatenate([dc_c, dc_s], axis=1), dtype=F32).astype(BF)
    ds_c, ds_s = _dft_matrices(S, S ** -0.5)
    cs = jnp.asarray(ds_c, dtype=F32).astype(BF)
    ss = jnp.asarray(ds_s, dtype=F32).astype(BF)

    hx = _ln_mod_call(x, sh1, sc1, 512)
    hc = _ln_mod_call(ctx, sh1c, sc1c, CT)
    ckv_x, kr_x, cq_x = _latent_call(hx, w_all, b_all, gkv, gq, cos, sin, 512)
    ckv_c, kr_c, _ = _latent_call(hc, w_all, b_all, gkv, gq, cos_c, sin_c, CT)
    lat = jnp.concatenate([ckv_c, ckv_x], axis=1)
    kr = jnp.concatenate([kr_c, kr_x], axis=1)
    k_all, v_all = _kv_up_call(lat, kr, w_kv, 768)
    q_all = _q_up_call(cq_x, w_q, cos, sin, 512)
    attn = _attn_call(q_all, k_all, v_all, 1024, 256)

    hx2d = hx.reshape(T, D)
    gc, gs = _fproj_call(hx2d, w_all, b_all, dc, 512)
    fm = _pos_dft_call(cs, ss, gc.reshape(B, S, FOURIER_DIM), gs.reshape(B, S, FOURIER_DIM),
                       512, 512)
    gates, u_b = _gate_call(hx2d, w_all, b_all, peer_u[l], 512, 1024)
    merged, v_t = _merge_call(attn.reshape(T, D), fm.reshape(T, FOURIER_DIM),
                              w_o_mla[l].astype(BF), w_fourier[l].astype(BF), gates,
                              peer_v[l], 512, 512)
    x1, h2, h2_t = _outproj_call(merged, w_out[l].astype(BF), x, g1, ln1_g[l][None, :],
                                 ln1_b[l][None, :], sh2, sc2, 512)

    keys = peer_keys[l].reshape(N_HP, N_KEYS, PEER_HALF).astype(BF)
    rank2, cnt, e1, e2 = _select_call(peer_wq[l].astype(BF), keys, h2, 512)
    y_t = _peer_call(u_b, v_t, h2_t, rank2, cnt, e1, e2, 512, 1024, 2, 4)
    return _final_call(y_t, x1, g2, ln2_g[l][None, :], ln2_b[l][None, :], 256)
```

```python
import functools
import math

import numpy as np
import jax
import jax.numpy as jnp
from jax import lax
from jax.experimental import pallas as pl
from jax.experimental.pallas import tpu as pltpu

D_MODEL = 2048
GRID_W = 64
N_HEADS = 16
QK_NOPE = 128
QK_ROPE = 64
V_DIM = 128
Q_LORA = 512
KV_LORA = 512
ROPE_THETA = 10000.0
N_FOURIER_GROUPS = 4
FOURIER_GROUP_DIM = 256
FOURIER_DIM = N_FOURIER_GROUPS * FOURIER_GROUP_DIM
KV_END = KV_LORA + QK_ROPE
PEER_HEADS = 8
N_KEYS = 128
N_EXPERTS = N_KEYS * N_KEYS
PEER_HALF = 128
PEER_TOPK = 16
DEPTH = 1
DEEPNORM_ALPHA = (2.0 * DEPTH) ** 0.25
EPS = 1e-6

LANES = 128
QK_PAD = 2 * LANES
VMEM_LIMIT = 56 * 1024 * 1024

BF = jnp.bfloat16
F32 = jnp.float32


def _params(n_axes, vmem=VMEM_LIMIT):
    return pltpu.CompilerParams(
        dimension_semantics=("arbitrary",) * n_axes, vmem_limit_bytes=vmem)


def _dot(a, b):
    return jnp.dot(a, b, preferred_element_type=F32)


def _dot_nt(a, b):
    return lax.dot_general(a, b, (((1,), (1,)), ((), ())), preferred_element_type=F32)


def _layer_norm_rows(x):
    mu = jnp.mean(x, axis=-1, keepdims=True)
    xc = x - mu
    var = jnp.mean(xc * xc, axis=-1, keepdims=True)
    return xc * lax.rsqrt(var + EPS)


def _mod_kernel(c_ref, w_ref, b_ref, o_ref):
    a = jax.nn.silu(c_ref[...]).astype(BF)
    o_ref[...] = _dot(a, w_ref[...].astype(BF)) + b_ref[...]


def _mod_call(cmat, w_mod, b_mod):
    n = w_mod.shape[1]
    tn = 1024
    return pl.pallas_call(
        _mod_kernel,
        grid=(n // tn,),
        in_specs=[pl.BlockSpec((8, D_MODEL), lambda j: (0, 0)),
                  pl.BlockSpec((D_MODEL, tn), lambda j: (0, j)),
                  pl.BlockSpec((1, tn), lambda j: (0, j))],
        out_specs=pl.BlockSpec((8, tn), lambda j: (0, j)),
        out_shape=jax.ShapeDtypeStruct((8, n), F32),
        compiler_params=_params(1),
        name="adaln_mod",
    )(cmat, w_mod, b_mod.reshape(1, n))


def _ln_mod_kernel(x_ref, sh_ref, sc_ref, o_ref):
    y = _layer_norm_rows(x_ref[0])
    o_ref[0] = (y * (1.0 + sc_ref[0]) + sh_ref[0]).astype(BF)


def _ln_mod_call(x, shift, scale, tm):
    b, s, d = x.shape
    bm = shift.shape[0]
    mod_map = (lambda i, j: (i, 0, 0)) if bm == b else (lambda i, j: (0, 0, 0))
    return pl.pallas_call(
        _ln_mod_kernel,
        grid=(b, s // tm),
        in_specs=[pl.BlockSpec((1, tm, d), lambda i, j: (i, j, 0)),
                  pl.BlockSpec((1, 1, d), mod_map),
                  pl.BlockSpec((1, 1, d), mod_map)],
        out_specs=pl.BlockSpec((1, tm, d), lambda i, j: (i, j, 0)),
        out_shape=jax.ShapeDtypeStruct((b, s, d), BF),
        compiler_params=_params(2),
        name="ln_modulate",
    )(x, shift, scale)


LAT_COLS = KV_LORA + 2 * LANES + Q_LORA
GATE_COLS = 2 * D_MODEL
FOURIER_BLOCK = GATE_COLS // FOURIER_DIM
LAT_BLOCK = (GATE_COLS + FOURIER_DIM) // LAT_COLS
assert GATE_COLS % FOURIER_DIM == 0 and (GATE_COLS + FOURIER_DIM) % LAT_COLS == 0


def _latent_kernel(h_ref, w_ref, b_ref, gkv_ref, gq_ref, cos_ref, sin_ref,
                   ckv_ref, kr_ref, cq_ref):
    acc = _dot_nt(h_ref[0], w_ref[...]) + b_ref[...]
    ckv = acc[:, :KV_LORA]
    ka = acc[:, KV_LORA:KV_LORA + LANES]
    kb = acc[:, KV_LORA + LANES:KV_LORA + 2 * LANES]
    cq = acc[:, KV_LORA + 2 * LANES:]
    ckv_n = ckv * lax.rsqrt(jnp.mean(ckv * ckv, axis=-1, keepdims=True) + EPS)
    cq_n = cq * lax.rsqrt(jnp.mean(cq * cq, axis=-1, keepdims=True) + EPS)
    ckv_ref[0] = (ckv_n * gkv_ref[...]).astype(BF)
    cq_ref[0] = (cq_n * gq_ref[...]).astype(BF)
    kr_ref[0] = (ka * cos_ref[...] + kb * sin_ref[...]).astype(BF)


def _latent_call(h, w_all, b_all, gkv, gq, cos, sin, tm):
    b, s, d = h.shape
    row = lambda i, j: (i, j, 0)
    const = lambda i, j: (0, 0)
    return pl.pallas_call(
        _latent_kernel,
        grid=(b, s // tm),
        in_specs=[pl.BlockSpec((1, tm, d), row),
                  pl.BlockSpec((LAT_COLS, d), lambda i, j: (LAT_BLOCK, 0)),
                  pl.BlockSpec((1, LAT_COLS), lambda i, j: (0, LAT_BLOCK)),
                  pl.BlockSpec((1, KV_LORA), const),
                  pl.BlockSpec((1, Q_LORA), const),
                  pl.BlockSpec((tm, LANES), lambda i, j: (j, 0)),
                  pl.BlockSpec((tm, LANES), lambda i, j: (j, 0))],
        out_specs=[pl.BlockSpec((1, tm, KV_LORA), row),
                   pl.BlockSpec((1, tm, LANES), row),
                   pl.BlockSpec((1, tm, Q_LORA), row)],
        out_shape=[jax.ShapeDtypeStruct((b, s, KV_LORA), BF),
                   jax.ShapeDtypeStruct((b, s, LANES), BF),
                   jax.ShapeDtypeStruct((b, s, Q_LORA), BF)],
        compiler_params=_params(2),
        name="latent_proj",
    )(h, w_all, b_all, gkv, gq, cos, sin)


def _fproj_kernel(h_ref, w_ref, b_ref, dc_ref, gc_ref, gs_ref):
    f = (_dot_nt(h_ref[...], w_ref[...]) + b_ref[...]).astype(BF)
    for g in range(N_FOURIER_GROUPS):
        lo = g * FOURIER_GROUP_DIM
        r = _dot(f[:, lo:lo + FOURIER_GROUP_DIM], dc_ref[...])
        gc_ref[:, lo:lo + FOURIER_GROUP_DIM] = r[:, :FOURIER_GROUP_DIM].astype(BF)
        gs_ref[:, lo:lo + FOURIER_GROUP_DIM] = r[:, FOURIER_GROUP_DIM:].astype(BF)


def _fproj_call(h2d, w_all, b_all, dc, tm):
    t, d = h2d.shape
    const = lambda i: (0, 0)
    return pl.pallas_call(
        _fproj_kernel,
        grid=(t // tm,),
        in_specs=[pl.BlockSpec((tm, d), lambda i: (i, 0)),
                  pl.BlockSpec((FOURIER_DIM, d), lambda i: (FOURIER_BLOCK, 0)),
                  pl.BlockSpec((1, FOURIER_DIM), lambda i: (0, FOURIER_BLOCK)),
                  pl.BlockSpec((FOURIER_GROUP_DIM, 2 * FOURIER_GROUP_DIM), const)],
        out_specs=[pl.BlockSpec((tm, FOURIER_DIM), lambda i: (i, 0)),
                   pl.BlockSpec((tm, FOURIER_DIM), lambda i: (i, 0))],
        out_shape=[jax.ShapeDtypeStruct((t, FOURIER_DIM), BF),
                   jax.ShapeDtypeStruct((t, FOURIER_DIM), BF)],
        compiler_params=_params(1),
        name="fourier_in_proj",
    )(h2d, w_all, b_all, dc)


def _gate_kernel(h_ref, w_ref, b_ref, o_ref):
    o_ref[...] = jax.nn.sigmoid(_dot_nt(h_ref[...], w_ref[...]) + b_ref[...]).astype(BF)


def _gate_call(h2d, w_all, b_all, tm, tn):
    t, d = h2d.shape
    n = GATE_COLS
    return pl.pallas_call(
        _gate_kernel,
        grid=(n // tn, t // tm),
        in_specs=[pl.BlockSpec((tm, d), lambda j, i: (i, 0)),
                  pl.BlockSpec((tn, d), lambda j, i: (j, 0)),
                  pl.BlockSpec((1, tn), lambda j, i: (0, j))],
        out_specs=pl.BlockSpec((tm, tn), lambda j, i: (i, j)),
        out_shape=jax.ShapeDtypeStruct((t, n), BF),
        compiler_params=_params(2),
        name="gate_proj",
    )(h2d, w_all, b_all)


def _kv_up_kernel(lat_ref, kr_ref, w_ref, k_ref, v_ref):
    lat = lat_ref[0]
    kr = kr_ref[0]
    for h in range(N_HEADS):
        lo = h * (QK_NOPE + V_DIM)
        kv = _dot(lat, w_ref[:, lo:lo + QK_NOPE + V_DIM])
        k_ref[0, h] = jnp.concatenate([kv[:, :QK_NOPE].astype(BF), kr], axis=-1)
        v_ref[0, h] = kv[:, QK_NOPE:].astype(BF)


def _kv_up_call(lat, kr, w_ukv, tm):
    b, t, _ = lat.shape
    return pl.pallas_call(
        _kv_up_kernel,
        grid=(b, t // tm),
        in_specs=[pl.BlockSpec((1, tm, KV_LORA), lambda i, j: (i, j, 0)),
                  pl.BlockSpec((1, tm, LANES), lambda i, j: (i, j, 0)),
                  pl.BlockSpec(w_ukv.shape, lambda i, j: (0, 0))],
        out_specs=[pl.BlockSpec((1, N_HEADS, tm, QK_PAD), lambda i, j: (i, 0, j, 0)),
                   pl.BlockSpec((1, N_HEADS, tm, V_DIM), lambda i, j: (i, 0, j, 0))],
        out_shape=[jax.ShapeDtypeStruct((b, N_HEADS, t, QK_PAD), BF),
                   jax.ShapeDtypeStruct((b, N_HEADS, t, V_DIM), BF)],
        compiler_params=_params(2),
        name="kv_up_proj",
    )(lat, kr, w_ukv)


Q_HEAD_COLS = 3 * LANES


def _q_up_kernel(cq_ref, w_ref, cos_ref, sin_ref, q_ref):
    cq = cq_ref[0]
    cos = cos_ref[...]
    sin = sin_ref[...]
    scale = (QK_NOPE + QK_ROPE) ** -0.5
    for h in range(N_HEADS):
        lo = h * Q_HEAD_COLS
        acc = _dot(cq, w_ref[:, lo:lo + Q_HEAD_COLS])
        qn = acc[:, :LANES]
        qr = acc[:, LANES:2 * LANES] * cos + acc[:, 2 * LANES:] * sin
        q_ref[0, h] = (jnp.concatenate([qn, qr], axis=-1) * scale).astype(BF)


def _q_up_call(cq, w_q, cos, sin, tm):
    b, s, _ = cq.shape
    return pl.pallas_call(
        _q_up_kernel,
        grid=(b, s // tm),
        in_specs=[pl.BlockSpec((1, tm, Q_LORA), lambda i, j: (i, j, 0)),
                  pl.BlockSpec(w_q.shape, lambda i, j: (0, 0)),
                  pl.BlockSpec((tm, LANES), lambda i, j: (j, 0)),
                  pl.BlockSpec((tm, LANES), lambda i, j: (j, 0))],
        out_specs=pl.BlockSpec((1, N_HEADS, tm, QK_PAD), lambda i, j: (i, 0, j, 0)),
        out_shape=jax.ShapeDtypeStruct((b, N_HEADS, s, QK_PAD), BF),
        compiler_params=_params(2),
        name="q_up_proj",
    )(cq, w_q, cos, sin)


ATTN_LAG = 1


def _attn_kernel(q_ref, k_ref, v_ref, eu_ref, ev_ref, o_ref, ub_ref, vt_ref, s_scr, m_scr,
                 *, kc):
    n = pl.program_id(0)
    tq = q_ref.shape[2]
    t = k_ref.shape[2]

    @pl.when(n == 0)
    def _():
        s_scr[...] = jnp.zeros_like(s_scr)
        m_scr[...] = jnp.zeros_like(m_scr)

    def step(cur, prev):
        q = q_ref[0, 0]
        m_prev = m_scr[prev]
        mrun = None
        lrun = jnp.zeros((tq, LANES), F32)
        acc = jnp.zeros((tq, V_DIM), F32)
        for c in range(t // kc):
            ks = slice(c * kc, (c + 1) * kc)
            s_c = lax.dot_general(q, k_ref[0, 0, ks, :], (((1,), (1,)), ((), ())),
                                  preferred_element_type=F32)
            s_scr[cur, :, ks] = s_c
            pieces = []
            for j in range(kc // LANES):
                lanes = slice(j * LANES, (j + 1) * LANES)
                col = slice(c * kc + j * LANES, c * kc + (j + 1) * LANES)
                p_j = jnp.exp(s_scr[prev, :, col] - m_prev)
                lrun = lrun + p_j
                pieces.append(p_j.astype(BF))
                mrun = s_c[:, lanes] if mrun is None else jnp.maximum(mrun, s_c[:, lanes])
            acc = acc + _dot(jnp.concatenate(pieces, axis=-1), v_ref[0, 0, ks, :])
        o_ref[0] = (acc / jnp.sum(lrun, axis=-1, keepdims=True)).astype(BF)
        m_scr[cur] = jnp.broadcast_to(jnp.max(mrun, axis=-1, keepdims=True), (tq, LANES))
        ub_ref[...] = eu_ref[...].astype(BF)
        vt_ref[...] = ev_ref[...].T.astype(BF)

    @pl.when(n % 2 == 0)
    def _():
        step(0, 1)

    @pl.when(n % 2 == 1)
    def _():
        step(1, 0)


def _attn_call(q, k, v, eu, ev, tq, kc):
    b, h, s, _ = q.shape
    t = k.shape[2]
    nq = s // tq
    total = b * h * nq
    ne, ed = eu.shape
    slab = ne // total
    assert slab * total == ne and slab % LANES == 0
    slab_idx = lambda n: jnp.minimum(n, total - 1)

    def block(n, lag):
        i = jnp.clip(n - lag, 0, total - 1)
        return i // (h * nq), (i // nq) % h, i % nq

    def q_map(n):
        bi, hi, qi = block(n, 0)
        return bi, hi, qi, 0

    def k_map(n):
        bi, hi, _ = block(n, 0)
        return bi, hi, 0, 0

    def v_map(n):
        bi, hi, _ = block(n, ATTN_LAG)
        return bi, hi, 0, 0

    def o_map(n):
        bi, hi, qi = block(n, ATTN_LAG)
        return bi, qi, hi

    return pl.pallas_call(
        functools.partial(_attn_kernel, kc=kc),
        grid=(total + ATTN_LAG,),
        in_specs=[pl.BlockSpec((1, 1, tq, QK_PAD), q_map),
                  pl.BlockSpec((1, 1, t, QK_PAD), k_map),
                  pl.BlockSpec((1, 1, t, V_DIM), v_map),
                  pl.BlockSpec((slab, ed), lambda n: (slab_idx(n), 0)),
                  pl.BlockSpec((slab, ed), lambda n: (slab_idx(n), 0))],
        out_specs=[pl.BlockSpec((1, tq, V_DIM), o_map),
                   pl.BlockSpec((slab, ed), lambda n: (slab_idx(n), 0)),
                   pl.BlockSpec((ed, slab), lambda n: (0, slab_idx(n)))],
        out_shape=[jax.ShapeDtypeStruct((b, s, h * V_DIM), BF),
                   jax.ShapeDtypeStruct((ne, ed), BF),
                   jax.ShapeDtypeStruct((ed, ne), BF)],
        scratch_shapes=[pltpu.VMEM((2, tq, t), F32), pltpu.VMEM((2, tq, LANES), F32)],
        compiler_params=_params(1),
        name="mla_attention",
    )(q, k, v, eu, ev)


def _pos_dft_kernel(c_ref, s_ref, gc_ref, gs_ref, o_ref):
    o_ref[0] = (_dot(c_ref[...], gc_ref[0]) - _dot(s_ref[...], gs_ref[0])).astype(BF)


def _pos_dft_call(cs, ss, gc, gs, tm, tn):
    b, s, n = gc.shape
    return pl.pallas_call(
        _pos_dft_kernel,
        grid=(b, n // tn, s // tm),
        in_specs=[pl.BlockSpec((tm, s), lambda i, j, m: (m, 0)),
                  pl.BlockSpec((tm, s), lambda i, j, m: (m, 0)),
                  pl.BlockSpec((1, s, tn), lambda i, j, m: (i, 0, j)),
                  pl.BlockSpec((1, s, tn), lambda i, j, m: (i, 0, j))],
        out_specs=pl.BlockSpec((1, tm, tn), lambda i, j, m: (i, m, j)),
        out_shape=jax.ShapeDtypeStruct((b, s, n), BF),
        compiler_params=_params(3),
        name="position_dft",
    )(cs, ss, gc, gs)


def _merge_kernel(a_ref, f_ref, wo_ref, wf_ref, ga_ref, gb_ref, o_ref):
    ya = _dot(a_ref[...], wo_ref[...])
    yb = _dot(f_ref[...], wf_ref[...])
    o_ref[...] = (ga_ref[...].astype(F32) * ya + gb_ref[...].astype(F32) * yb).astype(BF)


def _merge_call(attn, fm, w_o, w_f, gates, tm, tn):
    t, d = attn.shape
    nb = D_MODEL // tn
    return pl.pallas_call(
        _merge_kernel,
        grid=(nb, t // tm),
        in_specs=[pl.BlockSpec((tm, d), lambda j, i: (i, 0)),
                  pl.BlockSpec((tm, FOURIER_DIM), lambda j, i: (i, 0)),
                  pl.BlockSpec((d, tn), lambda j, i: (0, j)),
                  pl.BlockSpec((FOURIER_DIM, tn), lambda j, i: (0, j)),
                  pl.BlockSpec((tm, tn), lambda j, i: (i, j)),
                  pl.BlockSpec((tm, tn), lambda j, i: (i, j + nb))],
        out_specs=pl.BlockSpec((tm, tn), lambda j, i: (i, j)),
        out_shape=jax.ShapeDtypeStruct((t, D_MODEL), BF),
        compiler_params=_params(2),
        name="branch_merge",
    )(attn, fm, w_o, w_f, gates, gates)


def _outproj_kernel(m_ref, w_ref, x_ref, g1_ref, lg_ref, lb_ref, sh_ref, sc_ref,
                    x1_ref, h_ref, ht_ref):
    y = _dot(m_ref[...], w_ref[...])
    z = DEEPNORM_ALPHA * x_ref[0] + g1_ref[0] * y
    x1 = _layer_norm_rows(z) * lg_ref[...] + lb_ref[...]
    x1_ref[0] = x1
    h2 = _layer_norm_rows(x1) * (1.0 + sc_ref[0]) + sh_ref[0]
    h_ref[...] = h2.astype(BF)
    ht_ref[...] = h2.T.astype(BF)


def _outproj_call(merged, w_out, x, g1, ln_g, ln_b, sh2, sc2, tm):
    b, s, d = x.shape
    nb = s // tm
    bmap = lambda i, j: (i, 0, 0)
    const = lambda i, j: (0, 0)
    return pl.pallas_call(
        _outproj_kernel,
        grid=(b, nb),
        in_specs=[pl.BlockSpec((tm, d), lambda i, j: (i * nb + j, 0)),
                  pl.BlockSpec((d, d), const),
                  pl.BlockSpec((1, tm, d), lambda i, j: (i, j, 0)),
                  pl.BlockSpec((1, 1, d), bmap),
                  pl.BlockSpec((1, d), const),
                  pl.BlockSpec((1, d), const),
                  pl.BlockSpec((1, 1, d), bmap),
                  pl.BlockSpec((1, 1, d), bmap)],
        out_specs=[pl.BlockSpec((1, tm, d), lambda i, j: (i, j, 0)),
                   pl.BlockSpec((tm, d), lambda i, j: (i * nb + j, 0)),
                   pl.BlockSpec((d, tm), lambda i, j: (0, i * nb + j))],
        out_shape=[jax.ShapeDtypeStruct((b, s, d), F32),
                   jax.ShapeDtypeStruct((b * s, d), BF),
                   jax.ShapeDtypeStruct((d, b * s), BF)],
        compiler_params=_params(2),
        name="out_proj_deepnorm",
    )(merged, w_out, x, g1, ln_g, ln_b, sh2, sc2)


N_HP = 2 * PEER_HEADS
NOT_TOP = 127.0
HALF_K = PEER_TOPK // 2
BF16_ROWS = 16


def _top_values(s, k):
    tops = []
    for _ in range(k):
        m = jnp.max(s, axis=0, keepdims=True)
        tops.append(m)
        s = jnp.where(s == m, -jnp.inf, s)
    return tops


def _top_values_ranked(s, k):
    tops = []
    rank = jnp.full(s.shape, NOT_TOP, F32)
    for r in range(k):
        m = jnp.max(s, axis=0, keepdims=True)
        tops.append(m)
        hit = s == m
        rank = jnp.where(hit, float(r), rank)
        s = jnp.where(hit, -jnp.inf, s)
    return tops, rank


def _select_kernel(wq_ref, keys_ref, h_ref, rank_ref, cnt_ref, e1_ref, e2_ref,
                   s_scr, top_scr):
    q = _dot(h_ref[...], wq_ref[...]).astype(BF)
    for hp in range(N_HP):
        s_scr[hp] = lax.dot_general(keys_ref[hp], q[:, hp * PEER_HALF:(hp + 1) * PEER_HALF],
                                    (((1,), (1,)), ((), ())), preferred_element_type=F32)

    def head_body(h, carry):
        s1 = s_scr[2 * h]
        s2 = s_scr[2 * h + 1]
        tops1 = _top_values(s1, PEER_TOPK + 1)
        tops2, rank2 = _top_values_ranked(s2, PEER_TOPK + 1)
        for r in range(PEER_TOPK):
            top_scr[0, r:r + 1, :] = tops1[r]
            top_scr[1, r:r + 1, :] = tops2[r]
        t1 = top_scr[0]
        t2 = top_scr[1]
        m1, m2 = tops1[0], tops2[0]
        cand = jnp.concatenate(
            [m1 + t2]
            + [tops1[a] + t2[:HALF_K] for a in range(1, HALF_K)]
            + [t1[HALF_K:] + m2], axis=0)
        best = _top_values(cand, PEER_TOPK + 1)
        outside = jnp.maximum(tops1[PEER_TOPK] + m2, m1 + tops2[PEER_TOPK])
        runner_up = jnp.maximum(best[PEER_TOPK], outside)
        tau = 0.5 * (best[PEER_TOPK - 1] + runner_up)
        sel = cand >= tau
        z = jnp.sum(jnp.where(sel, jnp.exp(cand - (m1 + m2)), 0.0), axis=0, keepdims=True)
        self32 = sel.astype(F32)
        counts = [jnp.sum(self32[:PEER_TOPK], axis=0, keepdims=True)]
        for a in range(1, HALF_K):
            lo = PEER_TOPK + (a - 1) * HALF_K
            counts.append(jnp.sum(self32[lo:lo + HALF_K], axis=0, keepdims=True))
        lo = PEER_TOPK + (HALF_K - 1) * HALF_K
        for a in range(HALF_K, PEER_TOPK):
            counts.append(self32[lo + a - HALF_K:lo + a - HALF_K + 1])
        cnt = jnp.zeros_like(s1)
        for a in range(PEER_TOPK):
            cnt = jnp.where(s1 == tops1[a], counts[a], cnt)
        rank_ref[h] = rank2.astype(BF)
        cnt_ref[h] = cnt
        e1_ref[h] = jnp.exp(s1 - m1) * (0.5 / z)
        e2_ref[h] = jnp.exp(s2 - m2).astype(BF)
        return carry
    lax.fori_loop(0, PEER_HEADS, head_body, 0)


def _select_call(wq, keys, h, tn):
    t, d = h.shape
    shape = (PEER_HEADS, N_KEYS, t)
    ospec = pl.BlockSpec((PEER_HEADS, N_KEYS, tn), lambda i: (0, 0, i))
    return pl.pallas_call(
        _select_kernel,
        grid=(t // tn,),
        in_specs=[pl.BlockSpec(wq.shape, lambda i: (0, 0)),
                  pl.BlockSpec(keys.shape, lambda i: (0, 0, 0)),
                  pl.BlockSpec((tn, d), lambda i: (i, 0))],
        out_specs=[ospec, ospec, ospec, ospec],
        out_shape=[jax.ShapeDtypeStruct(shape, BF), jax.ShapeDtypeStruct(shape, F32),
                   jax.ShapeDtypeStruct(shape, F32), jax.ShapeDtypeStruct(shape, BF)],
        scratch_shapes=[pltpu.VMEM((N_HP, N_KEYS, tn), F32),
                        pltpu.VMEM((2, PEER_TOPK, tn), F32)],
        compiler_params=_params(1),
        name="peer_select",
    )(wq, keys, h)


def _gelu_times_two(x):
    return x * (1.0 + lax.erf(x * math.sqrt(0.5)))


def _peer_kernel(u_ref, vt_ref, ht_ref, rank_ref, cnt_ref, e1_ref, e2_ref, o_ref,
                 a_scr, c_scr, *, rows, act_slices, mix_slices):
    e = pl.program_id(1)
    tn = ht_ref.shape[1]
    d = vt_ref.shape[0]
    half = rows // 2
    hrows = half * N_KEYS

    @pl.when(e == 0)
    def _():
        o_ref[...] = jnp.zeros_like(o_ref)

    def coef_row(r):
        w = None
        for h in range(PEER_HEADS):
            cnt = jnp.broadcast_to(cnt_ref[h, r:r + 1, :], (BF16_ROWS, tn)).astype(BF)
            e1 = jnp.broadcast_to(e1_ref[h, r:r + 1, :], (BF16_ROWS, tn)).astype(BF)
            gate = jnp.where(rank_ref[h] < cnt[None], e1[None], jnp.zeros((), BF))
            w = e2_ref[h] * gate if w is None else w + e2_ref[h] * gate
        act = _gelu_times_two(a_scr[r * N_KEYS:(r + 1) * N_KEYS, :]).astype(BF)
        c_scr[r * N_KEYS:(r + 1) * N_KEYS, :] = w.reshape(N_KEYS, tn) * act

    ht = ht_ref[...]
    a_scr[0:hrows, :] = _dot(u_ref[0:hrows, :], ht)
    srows = hrows // act_slices
    for j in range(act_slices):
        lo = hrows + j * srows
        a_scr[lo:lo + srows, :] = _dot(u_ref[lo:lo + srows, :], ht)
        for r in range(j * half // act_slices, (j + 1) * half // act_slices):
            coef_row(r)
    mrows = d // mix_slices
    c_a = c_scr[0:hrows, :]
    for j in range(mix_slices):
        o_ref[j * mrows:(j + 1) * mrows, :] += _dot(vt_ref[j * mrows:(j + 1) * mrows, 0:hrows], c_a)
        for r in range(j * half // mix_slices, (j + 1) * half // mix_slices):
            coef_row(half + r)
    o_ref[...] += _dot(vt_ref[:, hrows:], c_scr[hrows:, :])


def _peer_call(u, v_t, h_t, rank2, cnt, e1, e2, tn, te, act_slices, mix_slices):
    d, t = h_t.shape
    rows = te // N_KEYS
    groups = N_KEYS // BF16_ROWS
    rank4 = rank2.reshape(PEER_HEADS, groups, BF16_ROWS, t)
    e24 = e2.reshape(PEER_HEADS, groups, BF16_ROWS, t)
    sel3 = pl.BlockSpec((PEER_HEADS, rows, tn), lambda i, e: (0, e, i))
    sel4 = pl.BlockSpec((PEER_HEADS, groups, BF16_ROWS, tn), lambda i, e: (0, 0, 0, i))
    return pl.pallas_call(
        functools.partial(_peer_kernel, rows=rows, act_slices=act_slices,
                          mix_slices=mix_slices),
        grid=(t // tn, N_EXPERTS // te),
        in_specs=[pl.BlockSpec((te, d), lambda i, e: (e, 0)),
                  pl.BlockSpec((d, te), lambda i, e: (0, e)),
                  pl.BlockSpec((d, tn), lambda i, e: (0, i)),
                  sel4, sel3, sel3, sel4],
        out_specs=pl.BlockSpec((d, tn), lambda i, e: (0, i)),
        out_shape=jax.ShapeDtypeStruct((d, t), F32),
        scratch_shapes=[pltpu.VMEM((te, tn), F32), pltpu.VMEM((te, tn), BF)],
        compiler_params=_params(2),
        name="peer_dense",
    )(u, v_t, h_t, rank4, cnt, e1, e24)


def _final_kernel(yt_ref, x_ref, g2_ref, lg_ref, lb_ref, o_ref):
    z = DEEPNORM_ALPHA * x_ref[0] + g2_ref[0] * yt_ref[...].T
    o_ref[0] = _layer_norm_rows(z) * lg_ref[...] + lb_ref[...]


def _final_call(y_t, x1, g2, ln_g, ln_b, tm):
    b, s, d = x1.shape
    nb = s // tm
    return pl.pallas_call(
        _final_kernel,
        grid=(b, nb),
        in_specs=[pl.BlockSpec((d, tm), lambda i, j: (0, i * nb + j)),
                  pl.BlockSpec((1, tm, d), lambda i, j: (i, j, 0)),
                  pl.BlockSpec((1, 1, d), lambda i, j: (i, 0, 0)),
                  pl.BlockSpec((1, d), lambda i, j: (0, 0)),
                  pl.BlockSpec((1, d), lambda i, j: (0, 0))],
        out_specs=pl.BlockSpec((1, tm, d), lambda i, j: (i, j, 0)),
        out_shape=jax.ShapeDtypeStruct((b, s, d), F32),
        compiler_params=_params(2),
        name="final_deepnorm",
    )(y_t, x1, g2, ln_g, ln_b)


def _rope_rotation(w):
    pairs = w.reshape(w.shape[:-1] + (w.shape[-1] // 2, 2))
    return jnp.stack([-pairs[..., 1], pairs[..., 0]], axis=-1).reshape(w.shape)


def _pad_lanes(w):
    return jnp.pad(w, [(0, 0)] * (w.ndim - 1) + [(0, LANES - w.shape[-1])])


def _rope_tables(seq):
    rows = seq // GRID_W
    row = jnp.repeat(jnp.arange(rows, dtype=F32), GRID_W)
    col = jnp.tile(jnp.arange(GRID_W, dtype=F32), rows)
    half = QK_ROPE // 2
    inv = ROPE_THETA ** (-jnp.arange(0, half, 2, dtype=F32) / half)
    ang = jnp.concatenate([row[:, None] * inv, col[:, None] * inv], axis=-1)
    cos = _pad_lanes(jnp.repeat(jnp.cos(ang), 2, axis=-1))
    sin = _pad_lanes(jnp.repeat(jnp.sin(ang), 2, axis=-1))
    return cos, sin


def _dft_matrices(n, scale):
    k = np.arange(n, dtype=np.int64)
    ang = 2.0 * np.pi * ((k[:, None] * k[None, :]) % n).astype(np.float64) / n
    return np.cos(ang) * scale, np.sin(ang) * scale


def kernel(x, c, ctx, c_ctx, w_mod, b_mod, w_in, b_in, q_norm_g, w_uq, kv_norm_g, w_ukv,
           w_o_mla, w_fourier, w_out, ln1_g, ln1_b, peer_wq, peer_keys, peer_u, peer_v,
           ln2_g, ln2_b):
    B, S, D = x.shape
    T = B * S
    CT = ctx.shape[1]
    l = 0

    cmat = jnp.concatenate([c, c_ctx[None, :], jnp.zeros((8 - B - 1, D), F32)], axis=0)
    mod = _mod_call(cmat, w_mod[l], b_mod[l])
    mx = mod[:B].reshape(B, 1, 6, D)
    sh1, sc1, g1, sh2, sc2, g2 = [mx[:, :, i, :] for i in range(6)]
    mc = mod[B].reshape(1, 1, 6, D)
    sh1c, sc1c = mc[:, :, 0, :], mc[:, :, 1, :]

    wt, bi = w_in[l].T, b_in[l]
    w_kr, b_kr = wt[KV_LORA:KV_END].T, bi[KV_LORA:KV_END]
    q0 = KV_END
    f0 = KV_END + Q_LORA
    g0 = f0 + FOURIER_DIM
    w_all = jnp.concatenate(
        [wt[g0:], wt[f0:g0],
         wt[:KV_LORA], _pad_lanes(w_kr).T, _pad_lanes(_rope_rotation(w_kr)).T, wt[q0:f0]],
        axis=0).astype(BF)
    b_all = jnp.concatenate(
        [bi[g0:], bi[f0:g0],
         bi[:KV_LORA], _pad_lanes(b_kr), _pad_lanes(_rope_rotation(b_kr)), bi[q0:f0]])[None, :]
    wq3 = w_uq[l].reshape(Q_LORA, N_HEADS, QK_NOPE + QK_ROPE)
    wq_rope = wq3[:, :, QK_NOPE:]
    w_q = jnp.concatenate(
        [wq3[:, :, :QK_NOPE], _pad_lanes(wq_rope), _pad_lanes(_rope_rotation(wq_rope))],
        axis=-1).reshape(Q_LORA, N_HEADS * Q_HEAD_COLS).astype(BF)
    w_kv = w_ukv[l].astype(BF)
    gkv = kv_norm_g[l][None, :]
    gq = q_norm_g[l][None, :]

    cos, sin = _rope_tables(S)
    cos_c = _pad_lanes(jnp.ones((CT, QK_ROPE), F32))
    sin_c = jnp.zeros((CT, LANES), F32)
    dc_c, dc_s = _dft_matrices(FOURIER_GROUP_DIM, FOURIER_GROUP_DIM ** -0.5)
    dc = jnp.asarray(np.concatenate([dc_c, dc_s], axis=1), dtype=F32).astype(BF)
    ds_c, ds_s = _dft_matrices(S, S ** -0.5)
    cs = jnp.asarray(ds_c, dtype=F32).astype(BF)
    ss = jnp.asarray(ds_s, dtype=F32).astype(BF)

    hx = _ln_mod_call(x, sh1, sc1, 512)
    hc = _ln_mod_call(ctx, sh1c, sc1c, CT)
    ckv_x, kr_x, cq_x = _latent_call(hx, w_all, b_all, gkv, gq, cos, sin, 512)
    ckv_c, kr_c, _ = _latent_call(hc, w_all, b_all, gkv, gq, cos_c, sin_c, CT)
    lat = jnp.concatenate([ckv_c, ckv_x], axis=1)
    kr = jnp.concatenate([kr_c, kr_x], axis=1)
    k_all, v_all = _kv_up_call(lat, kr, w_kv, 768)
    q_all = _q_up_call(cq_x, w_q, cos, sin, 512)
    attn, u_b, v_t = _attn_call(q_all, k_all, v_all, peer_u[l], peer_v[l], 1024, 256)

    hx2d = hx.reshape(T, D)
    gc, gs = _fproj_call(hx2d, w_all, b_all, dc, 512)
    fm = _pos_dft_call(cs, ss, gc.reshape(B, S, FOURIER_DIM), gs.reshape(B, S, FOURIER_DIM),
                       512, 512)
    gates = _gate_call(hx2d, w_all, b_all, 512, 1024)
    merged = _merge_call(attn.reshape(T, D), fm.reshape(T, FOURIER_DIM),
                         w_o_mla[l].astype(BF), w_fourier[l].astype(BF), gates, 512, 512)
    x1, h2, h2_t = _outproj_call(merged, w_out[l].astype(BF), x, g1, ln1_g[l][None, :],
                                 ln1_b[l][None, :], sh2, sc2, 512)

    keys = peer_keys[l].reshape(N_HP, N_KEYS, PEER_HALF).astype(BF)
    rank2, cnt, e1, e2 = _select_call(peer_wq[l].astype(BF), keys, h2, 512)
    y_t = _peer_call(u_b, v_t, h2_t, rank2, cnt, e1, e2, 512, 1024, 2, 4)
    return _final_call(y_t, x1, g2, ln2_g[l][None, :], ln2_b[l][None, :], 256)
```

```python
import functools
import math

import numpy as np
import jax
import jax.numpy as jnp
from jax import lax
from jax.experimental import pallas as pl
from jax.experimental.pallas import tpu as pltpu

D_MODEL = 2048
GRID_W = 64
N_HEADS = 16
QK_NOPE = 128
QK_ROPE = 64
V_DIM = 128
Q_LORA = 512
KV_LORA = 512
ROPE_THETA = 10000.0
N_FOURIER_GROUPS = 4
FOURIER_GROUP_DIM = 256
FOURIER_DIM = N_FOURIER_GROUPS * FOURIER_GROUP_DIM
KV_END = KV_LORA + QK_ROPE
PEER_HEADS = 8
N_KEYS = 128
N_EXPERTS = N_KEYS * N_KEYS
PEER_HALF = 128
PEER_TOPK = 16
DEPTH = 1
DEEPNORM_ALPHA = (2.0 * DEPTH) ** 0.25
EPS = 1e-6

LANES = 128
QK_PAD = 2 * LANES
VMEM_LIMIT = 56 * 1024 * 1024

BF = jnp.bfloat16
F32 = jnp.float32


def _params(n_axes, vmem=VMEM_LIMIT):
    return pltpu.CompilerParams(
        dimension_semantics=("arbitrary",) * n_axes, vmem_limit_bytes=vmem)


def _dot(a, b):
    return jnp.dot(a, b, preferred_element_type=F32)


def _dot_nt(a, b):
    return lax.dot_general(a, b, (((1,), (1,)), ((), ())), preferred_element_type=F32)


def _layer_norm_rows(x):
    mu = jnp.mean(x, axis=-1, keepdims=True)
    xc = x - mu
    var = jnp.mean(xc * xc, axis=-1, keepdims=True)
    return xc * lax.rsqrt(var + EPS)


def _mod_kernel(c_ref, w_ref, b_ref, o_ref):
    a = jax.nn.silu(c_ref[...]).astype(BF)
    o_ref[...] = _dot(a, w_ref[...].astype(BF)) + b_ref[...]


def _mod_call(cmat, w_mod, b_mod):
    n = w_mod.shape[1]
    tn = 1024
    return pl.pallas_call(
        _mod_kernel,
        grid=(n // tn,),
        in_specs=[pl.BlockSpec((8, D_MODEL), lambda j: (0, 0)),
                  pl.BlockSpec((D_MODEL, tn), lambda j: (0, j)),
                  pl.BlockSpec((1, tn), lambda j: (0, j))],
        out_specs=pl.BlockSpec((8, tn), lambda j: (0, j)),
        out_shape=jax.ShapeDtypeStruct((8, n), F32),
        compiler_params=_params(1),
        name="adaln_mod",
    )(cmat, w_mod, b_mod.reshape(1, n))


def _ln_mod_kernel(x_ref, sh_ref, sc_ref, o_ref):
    y = _layer_norm_rows(x_ref[0])
    o_ref[0] = (y * (1.0 + sc_ref[0]) + sh_ref[0]).astype(BF)


def _ln_mod_call(x, shift, scale, tm):
    b, s, d = x.shape
    bm = shift.shape[0]
    mod_map = (lambda i, j: (i, 0, 0)) if bm == b else (lambda i, j: (0, 0, 0))
    return pl.pallas_call(
        _ln_mod_kernel,
        grid=(b, s // tm),
        in_specs=[pl.BlockSpec((1, tm, d), lambda i, j: (i, j, 0)),
                  pl.BlockSpec((1, 1, d), mod_map),
                  pl.BlockSpec((1, 1, d), mod_map)],
        out_specs=pl.BlockSpec((1, tm, d), lambda i, j: (i, j, 0)),
        out_shape=jax.ShapeDtypeStruct((b, s, d), BF),
        compiler_params=_params(2),
        name="ln_modulate",
    )(x, shift, scale)


LAT_COLS = KV_LORA + 2 * LANES + Q_LORA
GATE_COLS = 2 * D_MODEL
FOURIER_BLOCK = GATE_COLS // FOURIER_DIM
LAT_BLOCK = (GATE_COLS + FOURIER_DIM) // LAT_COLS
assert GATE_COLS % FOURIER_DIM == 0 and (GATE_COLS + FOURIER_DIM) % LAT_COLS == 0


def _latent_kernel(h_ref, w_ref, b_ref, gkv_ref, gq_ref, cos_ref, sin_ref,
                   ckv_ref, kr_ref, cq_ref):
    acc = _dot_nt(h_ref[0], w_ref[...]) + b_ref[...]
    ckv = acc[:, :KV_LORA]
    ka = acc[:, KV_LORA:KV_LORA + LANES]
    kb = acc[:, KV_LORA + LANES:KV_LORA + 2 * LANES]
    cq = acc[:, KV_LORA + 2 * LANES:]
    ckv_n = ckv * lax.rsqrt(jnp.mean(ckv * ckv, axis=-1, keepdims=True) + EPS)
    cq_n = cq * lax.rsqrt(jnp.mean(cq * cq, axis=-1, keepdims=True) + EPS)
    ckv_ref[0] = (ckv_n * gkv_ref[...]).astype(BF)
    cq_ref[0] = (cq_n * gq_ref[...]).astype(BF)
    kr_ref[0] = (ka * cos_ref[...] + kb * sin_ref[...]).astype(BF)


def _latent_call(h, w_all, b_all, gkv, gq, cos, sin, tm):
    b, s, d = h.shape
    row = lambda i, j: (i, j, 0)
    const = lambda i, j: (0, 0)
    return pl.pallas_call(
        _latent_kernel,
        grid=(b, s // tm),
        in_specs=[pl.BlockSpec((1, tm, d), row),
                  pl.BlockSpec((LAT_COLS, d), lambda i, j: (LAT_BLOCK, 0)),
                  pl.BlockSpec((1, LAT_COLS), lambda i, j: (0, LAT_BLOCK)),
                  pl.BlockSpec((1, KV_LORA), const),
                  pl.BlockSpec((1, Q_LORA), const),
                  pl.BlockSpec((tm, LANES), lambda i, j: (j, 0)),
                  pl.BlockSpec((tm, LANES), lambda i, j: (j, 0))],
        out_specs=[pl.BlockSpec((1, tm, KV_LORA), row),
                   pl.BlockSpec((1, tm, LANES), row),
                   pl.BlockSpec((1, tm, Q_LORA), row)],
        out_shape=[jax.ShapeDtypeStruct((b, s, KV_LORA), BF),
                   jax.ShapeDtypeStruct((b, s, LANES), BF),
                   jax.ShapeDtypeStruct((b, s, Q_LORA), BF)],
        compiler_params=_params(2),
        name="latent_proj",
    )(h, w_all, b_all, gkv, gq, cos, sin)


def _fproj_kernel(h_ref, w_ref, b_ref, dc_ref, gc_ref, gs_ref):
    f = (_dot_nt(h_ref[...], w_ref[...]) + b_ref[...]).astype(BF)
    for g in range(N_FOURIER_GROUPS):
        lo = g * FOURIER_GROUP_DIM
        r = _dot(f[:, lo:lo + FOURIER_GROUP_DIM], dc_ref[...])
        gc_ref[:, lo:lo + FOURIER_GROUP_DIM] = r[:, :FOURIER_GROUP_DIM].astype(BF)
        gs_ref[:, lo:lo + FOURIER_GROUP_DIM] = r[:, FOURIER_GROUP_DIM:].astype(BF)


def _fproj_call(h2d, w_all, b_all, dc, tm):
    t, d = h2d.shape
    const = lambda i: (0, 0)
    return pl.pallas_call(
        _fproj_kernel,
        grid=(t // tm,),
        in_specs=[pl.BlockSpec((tm, d), lambda i: (i, 0)),
                  pl.BlockSpec((FOURIER_DIM, d), lambda i: (FOURIER_BLOCK, 0)),
                  pl.BlockSpec((1, FOURIER_DIM), lambda i: (0, FOURIER_BLOCK)),
                  pl.BlockSpec((FOURIER_GROUP_DIM, 2 * FOURIER_GROUP_DIM), const)],
        out_specs=[pl.BlockSpec((tm, FOURIER_DIM), lambda i: (i, 0)),
                   pl.BlockSpec((tm, FOURIER_DIM), lambda i: (i, 0))],
        out_shape=[jax.ShapeDtypeStruct((t, FOURIER_DIM), BF),
                   jax.ShapeDtypeStruct((t, FOURIER_DIM), BF)],
        compiler_params=_params(1),
        name="fourier_in_proj",
    )(h2d, w_all, b_all, dc)


def _gate_kernel(h_ref, w_ref, b_ref, o_ref):
    o_ref[...] = jax.nn.sigmoid(_dot_nt(h_ref[...], w_ref[...]) + b_ref[...]).astype(BF)


def _gate_call(h2d, w_all, b_all, tm, tn):
    t, d = h2d.shape
    n = GATE_COLS
    return pl.pallas_call(
        _gate_kernel,
        grid=(n // tn, t // tm),
        in_specs=[pl.BlockSpec((tm, d), lambda j, i: (i, 0)),
                  pl.BlockSpec((tn, d), lambda j, i: (j, 0)),
                  pl.BlockSpec((1, tn), lambda j, i: (0, j))],
        out_specs=pl.BlockSpec((tm, tn), lambda j, i: (i, j)),
        out_shape=jax.ShapeDtypeStruct((t, n), BF),
        compiler_params=_params(2),
        name="gate_proj",
    )(h2d, w_all, b_all)


def _kv_up_kernel(lat_ref, kr_ref, w_ref, k_ref, v_ref):
    lat = lat_ref[0]
    kr = kr_ref[0]
    for h in range(N_HEADS):
        lo = h * (QK_NOPE + V_DIM)
        kv = _dot(lat, w_ref[:, lo:lo + QK_NOPE + V_DIM])
        k_ref[0, h] = jnp.concatenate([kv[:, :QK_NOPE].astype(BF), kr], axis=-1)
        v_ref[0, h] = kv[:, QK_NOPE:].astype(BF)


def _kv_up_call(lat, kr, w_ukv, tm):
    b, t, _ = lat.shape
    return pl.pallas_call(
        _kv_up_kernel,
        grid=(b, t // tm),
        in_specs=[pl.BlockSpec((1, tm, KV_LORA), lambda i, j: (i, j, 0)),
                  pl.BlockSpec((1, tm, LANES), lambda i, j: (i, j, 0)),
                  pl.BlockSpec(w_ukv.shape, lambda i, j: (0, 0))],
        out_specs=[pl.BlockSpec((1, N_HEADS, tm, QK_PAD), lambda i, j: (i, 0, j, 0)),
                   pl.BlockSpec((1, N_HEADS, tm, V_DIM), lambda i, j: (i, 0, j, 0))],
        out_shape=[jax.ShapeDtypeStruct((b, N_HEADS, t, QK_PAD), BF),
                   jax.ShapeDtypeStruct((b, N_HEADS, t, V_DIM), BF)],
        compiler_params=_params(2),
        name="kv_up_proj",
    )(lat, kr, w_ukv)


Q_HEAD_COLS = 3 * LANES


def _q_up_kernel(cq_ref, w_ref, cos_ref, sin_ref, q_ref):
    cq = cq_ref[0]
    cos = cos_ref[...]
    sin = sin_ref[...]
    scale = (QK_NOPE + QK_ROPE) ** -0.5
    for h in range(N_HEADS):
        lo = h * Q_HEAD_COLS
        acc = _dot(cq, w_ref[:, lo:lo + Q_HEAD_COLS])
        qn = acc[:, :LANES]
        qr = acc[:, LANES:2 * LANES] * cos + acc[:, 2 * LANES:] * sin
        q_ref[0, h] = (jnp.concatenate([qn, qr], axis=-1) * scale).astype(BF)


def _q_up_call(cq, w_q, cos, sin, tm):
    b, s, _ = cq.shape
    return pl.pallas_call(
        _q_up_kernel,
        grid=(b, s // tm),
        in_specs=[pl.BlockSpec((1, tm, Q_LORA), lambda i, j: (i, j, 0)),
                  pl.BlockSpec(w_q.shape, lambda i, j: (0, 0)),
                  pl.BlockSpec((tm, LANES), lambda i, j: (j, 0)),
                  pl.BlockSpec((tm, LANES), lambda i, j: (j, 0))],
        out_specs=pl.BlockSpec((1, N_HEADS, tm, QK_PAD), lambda i, j: (i, 0, j, 0)),
        out_shape=jax.ShapeDtypeStruct((b, N_HEADS, s, QK_PAD), BF),
        compiler_params=_params(2),
        name="q_up_proj",
    )(cq, w_q, cos, sin)


ATTN_LAG = 1


def _attn_kernel(q_ref, k_ref, v_ref, eu_ref, ev_ref, *rest, kc, n_cast):
    cast_in = rest[:n_cast]
    o_ref, ub_ref, vt_ref = rest[n_cast:n_cast + 3]
    cast_out = rest[n_cast + 3:2 * n_cast + 3]
    s_scr, m_scr = rest[2 * n_cast + 3:]
    n = pl.program_id(0)
    tq = q_ref.shape[2]
    t = k_ref.shape[2]

    @pl.when(n == 0)
    def _():
        s_scr[...] = jnp.zeros_like(s_scr)
        m_scr[...] = jnp.zeros_like(m_scr)

    def step(cur, prev):
        q = q_ref[0, 0]
        m_prev = m_scr[prev]
        mrun = None
        lrun = jnp.zeros((tq, LANES), F32)
        acc = jnp.zeros((tq, V_DIM), F32)
        for c in range(t // kc):
            ks = slice(c * kc, (c + 1) * kc)
            s_c = lax.dot_general(q, k_ref[0, 0, ks, :], (((1,), (1,)), ((), ())),
                                  preferred_element_type=F32)
            s_scr[cur, :, ks] = s_c
            pieces = []
            for j in range(kc // LANES):
                lanes = slice(j * LANES, (j + 1) * LANES)
                col = slice(c * kc + j * LANES, c * kc + (j + 1) * LANES)
                p_j = jnp.exp(s_scr[prev, :, col] - m_prev)
                lrun = lrun + p_j
                pieces.append(p_j.astype(BF))
                mrun = s_c[:, lanes] if mrun is None else jnp.maximum(mrun, s_c[:, lanes])
            acc = acc + _dot(jnp.concatenate(pieces, axis=-1), v_ref[0, 0, ks, :])
        o_ref[0] = (acc / jnp.sum(lrun, axis=-1, keepdims=True)).astype(BF)
        m_scr[cur] = jnp.broadcast_to(jnp.max(mrun, axis=-1, keepdims=True), (tq, LANES))
        ub_ref[...] = eu_ref[...].astype(BF)
        vt_ref[...] = ev_ref[...].T.astype(BF)
        for src, dst in zip(cast_in, cast_out):
            dst[...] = src[...].astype(BF)

    @pl.when(n % 2 == 0)
    def _():
        step(0, 1)

    @pl.when(n % 2 == 1)
    def _():
        step(1, 0)


CAST_STEPS = 64


def _attn_call(q, k, v, eu, ev, weights, tq, kc):
    b, h, s, _ = q.shape
    t = k.shape[2]
    nq = s // tq
    total = b * h * nq
    ne, ed = eu.shape
    slab = ne // total
    assert slab * total == ne and slab % LANES == 0 and total >= CAST_STEPS
    slab_idx = lambda n: jnp.minimum(n, total - 1)
    cast_idx = lambda n: (jnp.minimum(n, CAST_STEPS - 1), 0)
    cast_specs = []
    for w in weights:
        assert w.shape[0] % (CAST_STEPS * BF16_ROWS) == 0
        cast_specs.append(pl.BlockSpec((w.shape[0] // CAST_STEPS, w.shape[1]), cast_idx))

    def block(n, lag):
        i = jnp.clip(n - lag, 0, total - 1)
        return i // (h * nq), (i // nq) % h, i % nq

    def q_map(n):
        bi, hi, qi = block(n, 0)
        return bi, hi, qi, 0

    def k_map(n):
        bi, hi, _ = block(n, 0)
        return bi, hi, 0, 0

    def v_map(n):
        bi, hi, _ = block(n, ATTN_LAG)
        return bi, hi, 0, 0

    def o_map(n):
        bi, hi, qi = block(n, ATTN_LAG)
        return bi, qi, hi

    outs = pl.pallas_call(
        functools.partial(_attn_kernel, kc=kc, n_cast=len(weights)),
        grid=(total + ATTN_LAG,),
        in_specs=[pl.BlockSpec((1, 1, tq, QK_PAD), q_map),
                  pl.BlockSpec((1, 1, t, QK_PAD), k_map),
                  pl.BlockSpec((1, 1, t, V_DIM), v_map),
                  pl.BlockSpec((slab, ed), lambda n: (slab_idx(n), 0)),
                  pl.BlockSpec((slab, ed), lambda n: (slab_idx(n), 0))] + cast_specs,
        out_specs=[pl.BlockSpec((1, tq, V_DIM), o_map),
                   pl.BlockSpec((slab, ed), lambda n: (slab_idx(n), 0)),
                   pl.BlockSpec((ed, slab), lambda n: (0, slab_idx(n)))] + cast_specs,
        out_shape=[jax.ShapeDtypeStruct((b, s, h * V_DIM), BF),
                   jax.ShapeDtypeStruct((ne, ed), BF),
                   jax.ShapeDtypeStruct((ed, ne), BF)]
        + [jax.ShapeDtypeStruct(w.shape, BF) for w in weights],
        scratch_shapes=[pltpu.VMEM((2, tq, t), F32), pltpu.VMEM((2, tq, LANES), F32)],
        compiler_params=_params(1),
        name="mla_attention",
    )(q, k, v, eu, ev, *weights)
    return outs[0], outs[1], outs[2], outs[3:]


def _pos_dft_kernel(c_ref, s_ref, gc_ref, gs_ref, o_ref):
    o_ref[0] = (_dot(c_ref[...], gc_ref[0]) - _dot(s_ref[...], gs_ref[0])).astype(BF)


def _pos_dft_call(cs, ss, gc, gs, tm, tn):
    b, s, n = gc.shape
    return pl.pallas_call(
        _pos_dft_kernel,
        grid=(b, n // tn, s // tm),
        in_specs=[pl.BlockSpec((tm, s), lambda i, j, m: (m, 0)),
                  pl.BlockSpec((tm, s), lambda i, j, m: (m, 0)),
                  pl.BlockSpec((1, s, tn), lambda i, j, m: (i, 0, j)),
                  pl.BlockSpec((1, s, tn), lambda i, j, m: (i, 0, j))],
        out_specs=pl.BlockSpec((1, tm, tn), lambda i, j, m: (i, m, j)),
        out_shape=jax.ShapeDtypeStruct((b, s, n), BF),
        compiler_params=_params(3),
        name="position_dft",
    )(cs, ss, gc, gs)


def _merge_kernel(a_ref, f_ref, wo_ref, wf_ref, ga_ref, gb_ref, o_ref):
    ya = _dot(a_ref[...], wo_ref[...])
    yb = _dot(f_ref[...], wf_ref[...])
    o_ref[...] = (ga_ref[...].astype(F32) * ya + gb_ref[...].astype(F32) * yb).astype(BF)


def _merge_call(attn, fm, w_o, w_f, gates, tm, tn):
    t, d = attn.shape
    nb = D_MODEL // tn
    return pl.pallas_call(
        _merge_kernel,
        grid=(nb, t // tm),
        in_specs=[pl.BlockSpec((tm, d), lambda j, i: (i, 0)),
                  pl.BlockSpec((tm, FOURIER_DIM), lambda j, i: (i, 0)),
                  pl.BlockSpec((d, tn), lambda j, i: (0, j)),
                  pl.BlockSpec((FOURIER_DIM, tn), lambda j, i: (0, j)),
                  pl.BlockSpec((tm, tn), lambda j, i: (i, j)),
                  pl.BlockSpec((tm, tn), lambda j, i: (i, j + nb))],
        out_specs=pl.BlockSpec((tm, tn), lambda j, i: (i, j)),
        out_shape=jax.ShapeDtypeStruct((t, D_MODEL), BF),
        compiler_params=_params(2),
        name="branch_merge",
    )(attn, fm, w_o, w_f, gates, gates)


def _outproj_kernel(m_ref, w_ref, x_ref, g1_ref, lg_ref, lb_ref, sh_ref, sc_ref,
                    x1_ref, h_ref, ht_ref):
    y = _dot(m_ref[...], w_ref[...])
    z = DEEPNORM_ALPHA * x_ref[0] + g1_ref[0] * y
    x1 = _layer_norm_rows(z) * lg_ref[...] + lb_ref[...]
    x1_ref[0] = x1
    h2 = _layer_norm_rows(x1) * (1.0 + sc_ref[0]) + sh_ref[0]
    h_ref[...] = h2.astype(BF)
    ht_ref[...] = h2.T.astype(BF)


def _outproj_call(merged, w_out, x, g1, ln_g, ln_b, sh2, sc2, tm):
    b, s, d = x.shape
    nb = s // tm
    bmap = lambda i, j: (i, 0, 0)
    const = lambda i, j: (0, 0)
    return pl.pallas_call(
        _outproj_kernel,
        grid=(b, nb),
        in_specs=[pl.BlockSpec((tm, d), lambda i, j: (i * nb + j, 0)),
                  pl.BlockSpec((d, d), const),
                  pl.BlockSpec((1, tm, d), lambda i, j: (i, j, 0)),
                  pl.BlockSpec((1, 1, d), bmap),
                  pl.BlockSpec((1, d), const),
                  pl.BlockSpec((1, d), const),
                  pl.BlockSpec((1, 1, d), bmap),
                  pl.BlockSpec((1, 1, d), bmap)],
        out_specs=[pl.BlockSpec((1, tm, d), lambda i, j: (i, j, 0)),
                   pl.BlockSpec((tm, d), lambda i, j: (i * nb + j, 0)),
                   pl.BlockSpec((d, tm), lambda i, j: (0, i * nb + j))],
        out_shape=[jax.ShapeDtypeStruct((b, s, d), F32),
                   jax.ShapeDtypeStruct((b * s, d), BF),
                   jax.ShapeDtypeStruct((d, b * s), BF)],
        compiler_params=_params(2),
        name="out_proj_deepnorm",
    )(merged, w_out, x, g1, ln_g, ln_b, sh2, sc2)


N_HP = 2 * PEER_HEADS
HALF_K = PEER_TOPK // 2
BF16_ROWS = 16


def _top_values(s, k):
    tops = []
    for _ in range(k):
        m = jnp.max(s, axis=0, keepdims=True)
        tops.append(m)
        s = jnp.where(s == m, -jnp.inf, s)
    return tops


def _sort_network(n):
    pairs = []
    p = 1
    while p < n:
        k = p
        while k >= 1:
            for j in range(k % p, n - k, 2 * k):
                for i in range(min(k, n - j - k)):
                    if (i + j) // (2 * p) == (i + j + k) // (2 * p):
                        pairs.append((i + j, i + j + k))
            k //= 2
        p *= 2
    return pairs


SUBLANES = 8


def _top_values_sorted(s, k):
    n = s.shape[0] // SUBLANES
    v = [s[j * SUBLANES:(j + 1) * SUBLANES, :] for j in range(n)]
    for i, j in _sort_network(n):
        v[i], v[j] = jnp.maximum(v[i], v[j]), jnp.minimum(v[i], v[j])
    tops = []
    for r in range(k):
        m = jnp.max(v[0], axis=0, keepdims=True)
        tops.append(m)
        depth = k - 1 - r
        if depth == 0:
            break
        hit = v[0] == m
        for j in range(min(depth, n - 1)):
            v[j] = jnp.where(hit, v[j + 1], v[j])
        if depth > n - 1:
            v[n - 1] = jnp.where(hit, -jnp.inf, v[n - 1])
    return tops


def _rank_among(s, tops):
    rank = jnp.full(s.shape, float(len(tops)), F32)
    for r in range(len(tops) - 1, -1, -1):
        rank = jnp.where(s >= tops[r], float(r), rank)
    return rank


def _select_kernel(wq_ref, keys_ref, h_ref, rank_ref, cnt_ref, e1_ref, e2_ref,
                   s_scr, top_scr):
    q = _dot(h_ref[...], wq_ref[...]).astype(BF)
    for hp in range(N_HP):
        s_scr[hp] = lax.dot_general(keys_ref[hp], q[:, hp * PEER_HALF:(hp + 1) * PEER_HALF],
                                    (((1,), (1,)), ((), ())), preferred_element_type=F32)

    def head_body(h, carry):
        s1 = s_scr[2 * h]
        s2 = s_scr[2 * h + 1]
        tops1 = _top_values_sorted(s1, PEER_TOPK + 1)
        tops2 = _top_values_sorted(s2, PEER_TOPK + 1)
        rank2 = _rank_among(s2, tops2)
        for r in range(PEER_TOPK):
            top_scr[0, r:r + 1, :] = tops1[r]
            top_scr[1, r:r + 1, :] = tops2[r]
        t1 = top_scr[0]
        t2 = top_scr[1]
        m1, m2 = tops1[0], tops2[0]
        cand = jnp.concatenate(
            [m1 + t2]
            + [tops1[a] + t2[:HALF_K] for a in range(1, HALF_K)]
            + [t1[HALF_K:] + m2], axis=0)
        best = _top_values(cand, PEER_TOPK + 1)
        outside = jnp.maximum(tops1[PEER_TOPK] + m2, m1 + tops2[PEER_TOPK])
        runner_up = jnp.maximum(best[PEER_TOPK], outside)
        tau = 0.5 * (best[PEER_TOPK - 1] + runner_up)
        sel = cand >= tau
        z = jnp.sum(jnp.where(sel, jnp.exp(cand - (m1 + m2)), 0.0), axis=0, keepdims=True)
        self32 = sel.astype(F32)
        counts = [jnp.sum(self32[:PEER_TOPK], axis=0, keepdims=True)]
        for a in range(1, HALF_K):
            lo = PEER_TOPK + (a - 1) * HALF_K
            counts.append(jnp.sum(self32[lo:lo + HALF_K], axis=0, keepdims=True))
        lo = PEER_TOPK + (HALF_K - 1) * HALF_K
        for a in range(HALF_K, PEER_TOPK):
            counts.append(self32[lo + a - HALF_K:lo + a - HALF_K + 1])
        cnt = jnp.zeros_like(s1)
        for a in range(PEER_TOPK):
            cnt = jnp.where(s1 == tops1[a], counts[a], cnt)
        rank_ref[h] = rank2.astype(BF)
        cnt_ref[h] = cnt
        e1_ref[h] = jnp.exp(s1 - m1) * (0.5 / z)
        e2_ref[h] = jnp.exp(s2 - m2).astype(BF)
        return carry
    lax.fori_loop(0, PEER_HEADS, head_body, 0)


def _select_call(wq, keys, h, tn):
    t, d = h.shape
    shape = (PEER_HEADS, N_KEYS, t)
    ospec = pl.BlockSpec((PEER_HEADS, N_KEYS, tn), lambda i: (0, 0, i))
    return pl.pallas_call(
        _select_kernel,
        grid=(t // tn,),
        in_specs=[pl.BlockSpec(wq.shape, lambda i: (0, 0)),
                  pl.BlockSpec(keys.shape, lambda i: (0, 0, 0)),
                  pl.BlockSpec((tn, d), lambda i: (i, 0))],
        out_specs=[ospec, ospec, ospec, ospec],
        out_shape=[jax.ShapeDtypeStruct(shape, BF), jax.ShapeDtypeStruct(shape, F32),
                   jax.ShapeDtypeStruct(shape, F32), jax.ShapeDtypeStruct(shape, BF)],
        scratch_shapes=[pltpu.VMEM((N_HP, N_KEYS, tn), F32),
                        pltpu.VMEM((2, PEER_TOPK, tn), F32)],
        compiler_params=_params(1),
        name="peer_select",
    )(wq, keys, h)


def _gelu_times_two(x):
    return x * (1.0 + lax.erf(x * math.sqrt(0.5)))


def _peer_kernel(u_ref, vt_ref, ht_ref, rank_ref, cnt_ref, e1_ref, e2_ref, o_ref,
                 a_scr, c_scr, *, rows, act_slices, mix_slices):
    e = pl.program_id(1)
    tn = ht_ref.shape[1]
    d = vt_ref.shape[0]
    half = rows // 2
    hrows = half * N_KEYS

    @pl.when(e == 0)
    def _():
        o_ref[...] = jnp.zeros_like(o_ref)

    def coef_row(r):
        w = None
        for h in range(PEER_HEADS):
            cnt = jnp.broadcast_to(cnt_ref[h, r:r + 1, :], (BF16_ROWS, tn)).astype(BF)
            e1 = jnp.broadcast_to(e1_ref[h, r:r + 1, :], (BF16_ROWS, tn)).astype(BF)
            gate = jnp.where(rank_ref[h] < cnt[None], e1[None], jnp.zeros((), BF))
            w = e2_ref[h] * gate if w is None else w + e2_ref[h] * gate
        act = _gelu_times_two(a_scr[r * N_KEYS:(r + 1) * N_KEYS, :]).astype(BF)
        c_scr[r * N_KEYS:(r + 1) * N_KEYS, :] = w.reshape(N_KEYS, tn) * act

    ht = ht_ref[...]
    a_scr[0:hrows, :] = _dot(u_ref[0:hrows, :], ht)
    srows = hrows // act_slices
    for j in range(act_slices):
        lo = hrows + j * srows
        a_scr[lo:lo + srows, :] = _dot(u_ref[lo:lo + srows, :], ht)
        for r in range(j * half // act_slices, (j + 1) * half // act_slices):
            coef_row(r)
    mrows = d // mix_slices
    c_a = c_scr[0:hrows, :]
    for j in range(mix_slices):
        o_ref[j * mrows:(j + 1) * mrows, :] += _dot(vt_ref[j * mrows:(j + 1) * mrows, 0:hrows], c_a)
        for r in range(j * half // mix_slices, (j + 1) * half // mix_slices):
            coef_row(half + r)
    o_ref[...] += _dot(vt_ref[:, hrows:], c_scr[hrows:, :])


def _peer_call(u, v_t, h_t, rank2, cnt, e1, e2, tn, te, act_slices, mix_slices):
    d, t = h_t.shape
    rows = te // N_KEYS
    groups = N_KEYS // BF16_ROWS
    rank4 = rank2.reshape(PEER_HEADS, groups, BF16_ROWS, t)
    e24 = e2.reshape(PEER_HEADS, groups, BF16_ROWS, t)
    sel3 = pl.BlockSpec((PEER_HEADS, rows, tn), lambda i, e: (0, e, i))
    sel4 = pl.BlockSpec((PEER_HEADS, groups, BF16_ROWS, tn), lambda i, e: (0, 0, 0, i))
    return pl.pallas_call(
        functools.partial(_peer_kernel, rows=rows, act_slices=act_slices,
                          mix_slices=mix_slices),
        grid=(t // tn, N_EXPERTS // te),
        in_specs=[pl.BlockSpec((te, d), lambda i, e: (e, 0)),
                  pl.BlockSpec((d, te), lambda i, e: (0, e)),
                  pl.BlockSpec((d, tn), lambda i, e: (0, i)),
                  sel4, sel3, sel3, sel4],
        out_specs=pl.BlockSpec((d, tn), lambda i, e: (0, i)),
        out_shape=jax.ShapeDtypeStruct((d, t), F32),
        scratch_shapes=[pltpu.VMEM((te, tn), F32), pltpu.VMEM((te, tn), BF)],
        compiler_params=_params(2),
        name="peer_dense",
    )(u, v_t, h_t, rank4, cnt, e1, e24)


def _final_kernel(yt_ref, x_ref, g2_ref, lg_ref, lb_ref, o_ref):
    z = DEEPNORM_ALPHA * x_ref[0] + g2_ref[0] * yt_ref[...].T
    o_ref[0] = _layer_norm_rows(z) * lg_ref[...] + lb_ref[...]


def _final_call(y_t, x1, g2, ln_g, ln_b, tm):
    b, s, d = x1.shape
    nb = s // tm
    return pl.pallas_call(
        _final_kernel,
        grid=(b, nb),
        in_specs=[pl.BlockSpec((d, tm), lambda i, j: (0, i * nb + j)),
                  pl.BlockSpec((1, tm, d), lambda i, j: (i, j, 0)),
                  pl.BlockSpec((1, 1, d), lambda i, j: (i, 0, 0)),
                  pl.BlockSpec((1, d), lambda i, j: (0, 0)),
                  pl.BlockSpec((1, d), lambda i, j: (0, 0))],
        out_specs=pl.BlockSpec((1, tm, d), lambda i, j: (i, j, 0)),
        out_shape=jax.ShapeDtypeStruct((b, s, d), F32),
        compiler_params=_params(2),
        name="final_deepnorm",
    )(y_t, x1, g2, ln_g, ln_b)


def _rope_rotation(w):
    pairs = w.reshape(w.shape[:-1] + (w.shape[-1] // 2, 2))
    return jnp.stack([-pairs[..., 1], pairs[..., 0]], axis=-1).reshape(w.shape)


def _pad_lanes(w):
    return jnp.pad(w, [(0, 0)] * (w.ndim - 1) + [(0, LANES - w.shape[-1])])


def _rope_tables(seq):
    rows = seq // GRID_W
    row = jnp.repeat(jnp.arange(rows, dtype=F32), GRID_W)
    col = jnp.tile(jnp.arange(GRID_W, dtype=F32), rows)
    half = QK_ROPE // 2
    inv = ROPE_THETA ** (-jnp.arange(0, half, 2, dtype=F32) / half)
    ang = jnp.concatenate([row[:, None] * inv, col[:, None] * inv], axis=-1)
    cos = _pad_lanes(jnp.repeat(jnp.cos(ang), 2, axis=-1))
    sin = _pad_lanes(jnp.repeat(jnp.sin(ang), 2, axis=-1))
    return cos, sin


def _dft_matrices(n, scale):
    k = np.arange(n, dtype=np.int64)
    ang = 2.0 * np.pi * ((k[:, None] * k[None, :]) % n).astype(np.float64) / n
    return np.cos(ang) * scale, np.sin(ang) * scale


def kernel(x, c, ctx, c_ctx, w_mod, b_mod, w_in, b_in, q_norm_g, w_uq, kv_norm_g, w_ukv,
           w_o_mla, w_fourier, w_out, ln1_g, ln1_b, peer_wq, peer_keys, peer_u, peer_v,
           ln2_g, ln2_b):
    B, S, D = x.shape
    T = B * S
    CT = ctx.shape[1]
    l = 0

    cmat = jnp.concatenate([c, c_ctx[None, :], jnp.zeros((8 - B - 1, D), F32)], axis=0)
    mod = _mod_call(cmat, w_mod[l], b_mod[l])
    mx = mod[:B].reshape(B, 1, 6, D)
    sh1, sc1, g1, sh2, sc2, g2 = [mx[:, :, i, :] for i in range(6)]
    mc = mod[B].reshape(1, 1, 6, D)
    sh1c, sc1c = mc[:, :, 0, :], mc[:, :, 1, :]

    wt, bi = w_in[l].T, b_in[l]
    w_kr, b_kr = wt[KV_LORA:KV_END].T, bi[KV_LORA:KV_END]
    q0 = KV_END
    f0 = KV_END + Q_LORA
    g0 = f0 + FOURIER_DIM
    w_all = jnp.concatenate(
        [wt[g0:], wt[f0:g0],
         wt[:KV_LORA], _pad_lanes(w_kr).T, _pad_lanes(_rope_rotation(w_kr)).T, wt[q0:f0]],
        axis=0).astype(BF)
    b_all = jnp.concatenate(
        [bi[g0:], bi[f0:g0],
         bi[:KV_LORA], _pad_lanes(b_kr), _pad_lanes(_rope_rotation(b_kr)), bi[q0:f0]])[None, :]
    wq3 = w_uq[l].reshape(Q_LORA, N_HEADS, QK_NOPE + QK_ROPE)
    wq_rope = wq3[:, :, QK_NOPE:]
    w_q = jnp.concatenate(
        [wq3[:, :, :QK_NOPE], _pad_lanes(wq_rope), _pad_lanes(_rope_rotation(wq_rope))],
        axis=-1).reshape(Q_LORA, N_HEADS * Q_HEAD_COLS).astype(BF)
    w_kv = w_ukv[l].astype(BF)
    gkv = kv_norm_g[l][None, :]
    gq = q_norm_g[l][None, :]

    cos, sin = _rope_tables(S)
    cos_c = _pad_lanes(jnp.ones((CT, QK_ROPE), F32))
    sin_c = jnp.zeros((CT, LANES), F32)
    dc_c, dc_s = _dft_matrices(FOURIER_GROUP_DIM, FOURIER_GROUP_DIM ** -0.5)
    dc = jnp.asarray(np.concatenate([dc_c, dc_s], axis=1), dtype=F32).astype(BF)
    ds_c, ds_s = _dft_matrices(S, S ** -0.5)
    cs = jnp.asarray(ds_c, dtype=F32).astype(BF)
    ss = jnp.asarray(ds_s, dtype=F32).astype(BF)

    hx = _ln_mod_call(x, sh1, sc1, 512)
    hc = _ln_mod_call(ctx, sh1c, sc1c, CT)
    ckv_x, kr_x, cq_x = _latent_call(hx, w_all, b_all, gkv, gq, cos, sin, 512)
    ckv_c, kr_c, _ = _latent_call(hc, w_all, b_all, gkv, gq, cos_c, sin_c, CT)
    lat = jnp.concatenate([ckv_c, ckv_x], axis=1)
    kr = jnp.concatenate([kr_c, kr_x], axis=1)
    k_all, v_all = _kv_up_call(lat, kr, w_kv, 768)
    q_all = _q_up_call(cq_x, w_q, cos, sin, 512)
    attn, u_b, v_t, (w_o_b, w_f_b, w_out_b, wq_b) = _attn_call(
        q_all, k_all, v_all, peer_u[l], peer_v[l],
        [w_o_mla[l], w_fourier[l], w_out[l], peer_wq[l]], 1024, 256)

    hx2d = hx.reshape(T, D)
    gc, gs = _fproj_call(hx2d, w_all, b_all, dc, 512)
    fm = _pos_dft_call(cs, ss, gc.reshape(B, S, FOURIER_DIM), gs.reshape(B, S, FOURIER_DIM),
                       512, 512)
    gates = _gate_call(hx2d, w_all, b_all, 512, 1024)
    merged = _merge_call(attn.reshape(T, D), fm.reshape(T, FOURIER_DIM),
                         w_o_b, w_f_b, gates, 512, 512)
    x1, h2, h2_t = _outproj_call(merged, w_out_b, x, g1, ln1_g[l][None, :],
                                 ln1_b[l][None, :], sh2, sc2, 512)

    keys = peer_keys[l].reshape(N_HP, N_KEYS, PEER_HALF).astype(BF)
    rank2, cnt, e1, e2 = _select_call(wq_b, keys, h2, 512)
    y_t = _peer_call(u_b, v_t, h2_t, rank2, cnt, e1, e2, 512, 1024, 2, 4)
    return _final_call(y_t, x1, g2, ln2_g[l][None, :], ln2_b[l][None, :], 256)
```

```python
import functools
import math

import numpy as np
import jax
import jax.numpy as jnp
from jax import lax
from jax.experimental import pallas as pl
from jax.experimental.pallas import tpu as pltpu

D_MODEL = 2048
GRID_W = 64
N_HEADS = 16
QK_NOPE = 128
QK_ROPE = 64
V_DIM = 128
Q_LORA = 512
KV_LORA = 512
ROPE_THETA = 10000.0
N_FOURIER_GROUPS = 4
FOURIER_GROUP_DIM = 256
FOURIER_DIM = N_FOURIER_GROUPS * FOURIER_GROUP_DIM
KV_END = KV_LORA + QK_ROPE
PEER_HEADS = 8
N_KEYS = 128
N_EXPERTS = N_KEYS * N_KEYS
PEER_HALF = 128
PEER_TOPK = 16
DEPTH = 1
DEEPNORM_ALPHA = (2.0 * DEPTH) ** 0.25
EPS = 1e-6

LANES = 128
QK_PAD = 2 * LANES
VMEM_LIMIT = 56 * 1024 * 1024

BF = jnp.bfloat16
F32 = jnp.float32


def _params(n_axes, vmem=VMEM_LIMIT):
    return pltpu.CompilerParams(
        dimension_semantics=("arbitrary",) * n_axes, vmem_limit_bytes=vmem)


def _dot(a, b):
    return jnp.dot(a, b, preferred_element_type=F32)


def _dot_nt(a, b):
    return lax.dot_general(a, b, (((1,), (1,)), ((), ())), preferred_element_type=F32)


def _layer_norm_rows(x):
    mu = jnp.mean(x, axis=-1, keepdims=True)
    xc = x - mu
    var = jnp.mean(xc * xc, axis=-1, keepdims=True)
    return xc * lax.rsqrt(var + EPS)


def _mod_kernel(c_ref, w_ref, b_ref, o_ref):
    a = jax.nn.silu(c_ref[...]).astype(BF)
    o_ref[...] = _dot(a, w_ref[...].astype(BF)) + b_ref[...]


def _mod_call(cmat, w_mod, b_mod):
    n = w_mod.shape[1]
    tn = 1024
    return pl.pallas_call(
        _mod_kernel,
        grid=(n // tn,),
        in_specs=[pl.BlockSpec((8, D_MODEL), lambda j: (0, 0)),
                  pl.BlockSpec((D_MODEL, tn), lambda j: (0, j)),
                  pl.BlockSpec((1, tn), lambda j: (0, j))],
        out_specs=pl.BlockSpec((8, tn), lambda j: (0, j)),
        out_shape=jax.ShapeDtypeStruct((8, n), F32),
        compiler_params=_params(1),
        name="adaln_mod",
    )(cmat, w_mod, b_mod.reshape(1, n))


def _ln_mod_kernel(x_ref, sh_ref, sc_ref, o_ref):
    y = _layer_norm_rows(x_ref[0])
    o_ref[0] = (y * (1.0 + sc_ref[0]) + sh_ref[0]).astype(BF)


def _ln_mod_call(x, shift, scale, tm):
    b, s, d = x.shape
    bm = shift.shape[0]
    mod_map = (lambda i, j: (i, 0, 0)) if bm == b else (lambda i, j: (0, 0, 0))
    return pl.pallas_call(
        _ln_mod_kernel,
        grid=(b, s // tm),
        in_specs=[pl.BlockSpec((1, tm, d), lambda i, j: (i, j, 0)),
                  pl.BlockSpec((1, 1, d), mod_map),
                  pl.BlockSpec((1, 1, d), mod_map)],
        out_specs=pl.BlockSpec((1, tm, d), lambda i, j: (i, j, 0)),
        out_shape=jax.ShapeDtypeStruct((b, s, d), BF),
        compiler_params=_params(2),
        name="ln_modulate",
    )(x, shift, scale)


LAT_COLS = KV_LORA + 2 * LANES + Q_LORA
GATE_COLS = 2 * D_MODEL
FOURIER_BLOCK = GATE_COLS // FOURIER_DIM
LAT_BLOCK = (GATE_COLS + FOURIER_DIM) // LAT_COLS
assert GATE_COLS % FOURIER_DIM == 0 and (GATE_COLS + FOURIER_DIM) % LAT_COLS == 0


def _latent_kernel(h_ref, w_ref, b_ref, gkv_ref, gq_ref, cos_ref, sin_ref,
                   ckv_ref, kr_ref, cq_ref):
    acc = _dot_nt(h_ref[0], w_ref[...]) + b_ref[...]
    ckv = acc[:, :KV_LORA]
    ka = acc[:, KV_LORA:KV_LORA + LANES]
    kb = acc[:, KV_LORA + LANES:KV_LORA + 2 * LANES]
    cq = acc[:, KV_LORA + 2 * LANES:]
    ckv_n = ckv * lax.rsqrt(jnp.mean(ckv * ckv, axis=-1, keepdims=True) + EPS)
    cq_n = cq * lax.rsqrt(jnp.mean(cq * cq, axis=-1, keepdims=True) + EPS)
    ckv_ref[0] = (ckv_n * gkv_ref[...]).astype(BF)
    cq_ref[0] = (cq_n * gq_ref[...]).astype(BF)
    kr_ref[0] = (ka * cos_ref[...] + kb * sin_ref[...]).astype(BF)


def _latent_call(h, w_all, b_all, gkv, gq, cos, sin, tm):
    b, s, d = h.shape
    row = lambda i, j: (i, j, 0)
    const = lambda i, j: (0, 0)
    return pl.pallas_call(
        _latent_kernel,
        grid=(b, s // tm),
        in_specs=[pl.BlockSpec((1, tm, d), row),
                  pl.BlockSpec((LAT_COLS, d), lambda i, j: (LAT_BLOCK, 0)),
                  pl.BlockSpec((1, LAT_COLS), lambda i, j: (0, LAT_BLOCK)),
                  pl.BlockSpec((1, KV_LORA), const),
                  pl.BlockSpec((1, Q_LORA), const),
                  pl.BlockSpec((tm, LANES), lambda i, j: (j, 0)),
                  pl.BlockSpec((tm, LANES), lambda i, j: (j, 0))],
        out_specs=[pl.BlockSpec((1, tm, KV_LORA), row),
                   pl.BlockSpec((1, tm, LANES), row),
                   pl.BlockSpec((1, tm, Q_LORA), row)],
        out_shape=[jax.ShapeDtypeStruct((b, s, KV_LORA), BF),
                   jax.ShapeDtypeStruct((b, s, LANES), BF),
                   jax.ShapeDtypeStruct((b, s, Q_LORA), BF)],
        compiler_params=_params(2),
        name="latent_proj",
    )(h, w_all, b_all, gkv, gq, cos, sin)


def _fproj_kernel(h_ref, w_ref, b_ref, dc_ref, gc_ref, gs_ref):
    f = (_dot_nt(h_ref[...], w_ref[...]) + b_ref[...]).astype(BF)
    for g in range(N_FOURIER_GROUPS):
        lo = g * FOURIER_GROUP_DIM
        r = _dot(f[:, lo:lo + FOURIER_GROUP_DIM], dc_ref[...])
        gc_ref[:, lo:lo + FOURIER_GROUP_DIM] = r[:, :FOURIER_GROUP_DIM].astype(BF)
        gs_ref[:, lo:lo + FOURIER_GROUP_DIM] = r[:, FOURIER_GROUP_DIM:].astype(BF)


def _fproj_call(h2d, w_all, b_all, dc, tm):
    t, d = h2d.shape
    const = lambda i: (0, 0)
    return pl.pallas_call(
        _fproj_kernel,
        grid=(t // tm,),
        in_specs=[pl.BlockSpec((tm, d), lambda i: (i, 0)),
                  pl.BlockSpec((FOURIER_DIM, d), lambda i: (FOURIER_BLOCK, 0)),
                  pl.BlockSpec((1, FOURIER_DIM), lambda i: (0, FOURIER_BLOCK)),
                  pl.BlockSpec((FOURIER_GROUP_DIM, 2 * FOURIER_GROUP_DIM), const)],
        out_specs=[pl.BlockSpec((tm, FOURIER_DIM), lambda i: (i, 0)),
                   pl.BlockSpec((tm, FOURIER_DIM), lambda i: (i, 0))],
        out_shape=[jax.ShapeDtypeStruct((t, FOURIER_DIM), BF),
                   jax.ShapeDtypeStruct((t, FOURIER_DIM), BF)],
        compiler_params=_params(1),
        name="fourier_in_proj",
    )(h2d, w_all, b_all, dc)


def _gate_kernel(h_ref, w_ref, b_ref, o_ref, *, pieces):
    h = h_ref[...]
    pn = w_ref.shape[0] // pieces
    for p in range(pieces):
        cols = slice(p * pn, (p + 1) * pn)
        o_ref[:, cols] = jax.nn.sigmoid(_dot_nt(h, w_ref[cols, :]) + b_ref[:, cols]).astype(BF)


def _gate_call(h2d, w_all, b_all, tm, tn, pieces):
    t, d = h2d.shape
    n = GATE_COLS
    return pl.pallas_call(
        functools.partial(_gate_kernel, pieces=pieces),
        grid=(n // tn, t // tm),
        in_specs=[pl.BlockSpec((tm, d), lambda j, i: (i, 0)),
                  pl.BlockSpec((tn, d), lambda j, i: (j, 0)),
                  pl.BlockSpec((1, tn), lambda j, i: (0, j))],
        out_specs=pl.BlockSpec((tm, tn), lambda j, i: (i, j)),
        out_shape=jax.ShapeDtypeStruct((t, n), BF),
        compiler_params=_params(2),
        name="gate_proj",
    )(h2d, w_all, b_all)


def _kv_up_kernel(lat_ref, kr_ref, w_ref, k_ref, v_ref):
    lat = lat_ref[0]
    kr = kr_ref[0]
    for h in range(N_HEADS):
        lo = h * (QK_NOPE + V_DIM)
        kv = _dot(lat, w_ref[:, lo:lo + QK_NOPE + V_DIM])
        k_ref[0, h] = jnp.concatenate([kv[:, :QK_NOPE].astype(BF), kr], axis=-1)
        v_ref[0, h] = kv[:, QK_NOPE:].astype(BF)


def _kv_up_call(lat, kr, w_ukv, tm):
    b, t, _ = lat.shape
    return pl.pallas_call(
        _kv_up_kernel,
        grid=(b, t // tm),
        in_specs=[pl.BlockSpec((1, tm, KV_LORA), lambda i, j: (i, j, 0)),
                  pl.BlockSpec((1, tm, LANES), lambda i, j: (i, j, 0)),
                  pl.BlockSpec(w_ukv.shape, lambda i, j: (0, 0))],
        out_specs=[pl.BlockSpec((1, N_HEADS, tm, QK_PAD), lambda i, j: (i, 0, j, 0)),
                   pl.BlockSpec((1, N_HEADS, tm, V_DIM), lambda i, j: (i, 0, j, 0))],
        out_shape=[jax.ShapeDtypeStruct((b, N_HEADS, t, QK_PAD), BF),
                   jax.ShapeDtypeStruct((b, N_HEADS, t, V_DIM), BF)],
        compiler_params=_params(2),
        name="kv_up_proj",
    )(lat, kr, w_ukv)


Q_HEAD_COLS = 3 * LANES


def _q_up_kernel(cq_ref, w_ref, cos_ref, sin_ref, q_ref):
    cq = cq_ref[0]
    cos = cos_ref[...]
    sin = sin_ref[...]
    scale = (QK_NOPE + QK_ROPE) ** -0.5
    for h in range(N_HEADS):
        lo = h * Q_HEAD_COLS
        acc = _dot(cq, w_ref[:, lo:lo + Q_HEAD_COLS])
        qn = acc[:, :LANES]
        qr = acc[:, LANES:2 * LANES] * cos + acc[:, 2 * LANES:] * sin
        q_ref[0, h] = (jnp.concatenate([qn, qr], axis=-1) * scale).astype(BF)


def _q_up_call(cq, w_q, cos, sin, tm):
    b, s, _ = cq.shape
    return pl.pallas_call(
        _q_up_kernel,
        grid=(b, s // tm),
        in_specs=[pl.BlockSpec((1, tm, Q_LORA), lambda i, j: (i, j, 0)),
                  pl.BlockSpec(w_q.shape, lambda i, j: (0, 0)),
                  pl.BlockSpec((tm, LANES), lambda i, j: (j, 0)),
                  pl.BlockSpec((tm, LANES), lambda i, j: (j, 0))],
        out_specs=pl.BlockSpec((1, N_HEADS, tm, QK_PAD), lambda i, j: (i, 0, j, 0)),
        out_shape=jax.ShapeDtypeStruct((b, N_HEADS, s, QK_PAD), BF),
        compiler_params=_params(2),
        name="q_up_proj",
    )(cq, w_q, cos, sin)


ATTN_LAG = 1


def _attn_kernel(q_ref, k_ref, v_ref, eu_ref, ev_ref, *rest, kc, n_cast):
    cast_in = rest[:n_cast]
    o_ref, ub_ref, vt_ref = rest[n_cast:n_cast + 3]
    cast_out = rest[n_cast + 3:2 * n_cast + 3]
    s_scr, m_scr = rest[2 * n_cast + 3:]
    n = pl.program_id(0)
    tq = q_ref.shape[2]
    t = k_ref.shape[2]

    @pl.when(n == 0)
    def _():
        s_scr[...] = jnp.zeros_like(s_scr)
        m_scr[...] = jnp.zeros_like(m_scr)

    def step(cur, prev):
        q = q_ref[0, 0]
        m_prev = m_scr[prev]
        mrun = None
        lrun = jnp.zeros((tq, LANES), F32)
        acc = jnp.zeros((tq, V_DIM), F32)
        for c in range(t // kc):
            ks = slice(c * kc, (c + 1) * kc)
            s_c = lax.dot_general(q, k_ref[0, 0, ks, :], (((1,), (1,)), ((), ())),
                                  preferred_element_type=F32)
            s_scr[cur, :, ks] = s_c
            pieces = []
            for j in range(kc // LANES):
                lanes = slice(j * LANES, (j + 1) * LANES)
                col = slice(c * kc + j * LANES, c * kc + (j + 1) * LANES)
                p_j = jnp.exp(s_scr[prev, :, col] - m_prev)
                lrun = lrun + p_j
                pieces.append(p_j.astype(BF))
                mrun = s_c[:, lanes] if mrun is None else jnp.maximum(mrun, s_c[:, lanes])
            acc = acc + _dot(jnp.concatenate(pieces, axis=-1), v_ref[0, 0, ks, :])
        o_ref[0] = (acc / jnp.sum(lrun, axis=-1, keepdims=True)).astype(BF)
        m_scr[cur] = jnp.broadcast_to(jnp.max(mrun, axis=-1, keepdims=True), (tq, LANES))
        ub_ref[...] = eu_ref[...].astype(BF)
        vt_ref[...] = ev_ref[...].T.astype(BF)
        for src, dst in zip(cast_in, cast_out):
            dst[...] = src[...].astype(BF)

    @pl.when(n % 2 == 0)
    def _():
        step(0, 1)

    @pl.when(n % 2 == 1)
    def _():
        step(1, 0)


CAST_STEPS = 64


def _attn_call(q, k, v, eu, ev, weights, tq, kc):
    b, h, s, _ = q.shape
    t = k.shape[2]
    nq = s // tq
    total = b * h * nq
    ne, ed = eu.shape
    slab = ne // total
    assert slab * total == ne and slab % LANES == 0 and total >= CAST_STEPS
    slab_idx = lambda n: jnp.minimum(n, total - 1)
    cast_idx = lambda n: (jnp.minimum(n, CAST_STEPS - 1), 0)
    cast_specs = []
    for w in weights:
        assert w.shape[0] % (CAST_STEPS * BF16_ROWS) == 0
        cast_specs.append(pl.BlockSpec((w.shape[0] // CAST_STEPS, w.shape[1]), cast_idx))

    def block(n, lag):
        i = jnp.clip(n - lag, 0, total - 1)
        return i // (h * nq), (i // nq) % h, i % nq

    def q_map(n):
        bi, hi, qi = block(n, 0)
        return bi, hi, qi, 0

    def k_map(n):
        bi, hi, _ = block(n, 0)
        return bi, hi, 0, 0

    def v_map(n):
        bi, hi, _ = block(n, ATTN_LAG)
        return bi, hi, 0, 0

    def o_map(n):
        bi, hi, qi = block(n, ATTN_LAG)
        return bi, qi, hi

    outs = pl.pallas_call(
        functools.partial(_attn_kernel, kc=kc, n_cast=len(weights)),
        grid=(total + ATTN_LAG,),
        in_specs=[pl.BlockSpec((1, 1, tq, QK_PAD), q_map),
                  pl.BlockSpec((1, 1, t, QK_PAD), k_map),
                  pl.BlockSpec((1, 1, t, V_DIM), v_map),
                  pl.BlockSpec((slab, ed), lambda n: (slab_idx(n), 0)),
                  pl.BlockSpec((slab, ed), lambda n: (slab_idx(n), 0))] + cast_specs,
        out_specs=[pl.BlockSpec((1, tq, V_DIM), o_map),
                   pl.BlockSpec((slab, ed), lambda n: (slab_idx(n), 0)),
                   pl.BlockSpec((ed, slab), lambda n: (0, slab_idx(n)))] + cast_specs,
        out_shape=[jax.ShapeDtypeStruct((b, s, h * V_DIM), BF),
                   jax.ShapeDtypeStruct((ne, ed), BF),
                   jax.ShapeDtypeStruct((ed, ne), BF)]
        + [jax.ShapeDtypeStruct(w.shape, BF) for w in weights],
        scratch_shapes=[pltpu.VMEM((2, tq, t), F32), pltpu.VMEM((2, tq, LANES), F32)],
        compiler_params=_params(1),
        name="mla_attention",
    )(q, k, v, eu, ev, *weights)
    return outs[0], outs[1], outs[2], outs[3:]


def _pos_dft_kernel(c_ref, s_ref, gc_ref, gs_ref, o_ref):
    o_ref[0] = (_dot(c_ref[...], gc_ref[0]) - _dot(s_ref[...], gs_ref[0])).astype(BF)


def _pos_dft_call(cs, ss, gc, gs, tm, tn):
    b, s, n = gc.shape
    return pl.pallas_call(
        _pos_dft_kernel,
        grid=(b, n // tn, s // tm),
        in_specs=[pl.BlockSpec((tm, s), lambda i, j, m: (m, 0)),
                  pl.BlockSpec((tm, s), lambda i, j, m: (m, 0)),
                  pl.BlockSpec((1, s, tn), lambda i, j, m: (i, 0, j)),
                  pl.BlockSpec((1, s, tn), lambda i, j, m: (i, 0, j))],
        out_specs=pl.BlockSpec((1, tm, tn), lambda i, j, m: (i, m, j)),
        out_shape=jax.ShapeDtypeStruct((b, s, n), BF),
        compiler_params=_params(3),
        name="position_dft",
    )(cs, ss, gc, gs)


def _merge_kernel(a_ref, f_ref, wo_ref, wf_ref, ga_ref, gb_ref, o_ref, *, pieces):
    a = a_ref[...]
    f = f_ref[...]
    pn = wo_ref.shape[1] // pieces
    for p in range(pieces):
        cols = slice(p * pn, (p + 1) * pn)
        ya = _dot(a, wo_ref[:, cols])
        yb = _dot(f, wf_ref[:, cols])
        o_ref[:, cols] = (ga_ref[:, cols].astype(F32) * ya
                          + gb_ref[:, cols].astype(F32) * yb).astype(BF)


def _merge_call(attn, fm, w_o, w_f, gates, tm, tn, pieces):
    t, d = attn.shape
    nb = D_MODEL // tn
    return pl.pallas_call(
        functools.partial(_merge_kernel, pieces=pieces),
        grid=(nb, t // tm),
        in_specs=[pl.BlockSpec((tm, d), lambda j, i: (i, 0)),
                  pl.BlockSpec((tm, FOURIER_DIM), lambda j, i: (i, 0)),
                  pl.BlockSpec((d, tn), lambda j, i: (0, j)),
                  pl.BlockSpec((FOURIER_DIM, tn), lambda j, i: (0, j)),
                  pl.BlockSpec((tm, tn), lambda j, i: (i, j)),
                  pl.BlockSpec((tm, tn), lambda j, i: (i, j + nb))],
        out_specs=pl.BlockSpec((tm, tn), lambda j, i: (i, j)),
        out_shape=jax.ShapeDtypeStruct((t, D_MODEL), BF),
        compiler_params=_params(2),
        name="branch_merge",
    )(attn, fm, w_o, w_f, gates, gates)


def _outproj_kernel(m_ref, w_ref, x_ref, g1_ref, lg_ref, lb_ref, sh_ref, sc_ref,
                    x1_ref, h_ref, ht_ref):
    y = _dot(m_ref[...], w_ref[...])
    z = DEEPNORM_ALPHA * x_ref[0] + g1_ref[0] * y
    x1 = _layer_norm_rows(z) * lg_ref[...] + lb_ref[...]
    x1_ref[0] = x1
    h2 = _layer_norm_rows(x1) * (1.0 + sc_ref[0]) + sh_ref[0]
    h_ref[...] = h2.astype(BF)
    ht_ref[...] = h2.T.astype(BF)


def _outproj_call(merged, w_out, x, g1, ln_g, ln_b, sh2, sc2, tm):
    b, s, d = x.shape
    nb = s // tm
    bmap = lambda i, j: (i, 0, 0)
    const = lambda i, j: (0, 0)
    return pl.pallas_call(
        _outproj_kernel,
        grid=(b, nb),
        in_specs=[pl.BlockSpec((tm, d), lambda i, j: (i * nb + j, 0)),
                  pl.BlockSpec((d, d), const),
                  pl.BlockSpec((1, tm, d), lambda i, j: (i, j, 0)),
                  pl.BlockSpec((1, 1, d), bmap),
                  pl.BlockSpec((1, d), const),
                  pl.BlockSpec((1, d), const),
                  pl.BlockSpec((1, 1, d), bmap),
                  pl.BlockSpec((1, 1, d), bmap)],
        out_specs=[pl.BlockSpec((1, tm, d), lambda i, j: (i, j, 0)),
                   pl.BlockSpec((tm, d), lambda i, j: (i * nb + j, 0)),
                   pl.BlockSpec((d, tm), lambda i, j: (0, i * nb + j))],
        out_shape=[jax.ShapeDtypeStruct((b, s, d), F32),
                   jax.ShapeDtypeStruct((b * s, d), BF),
                   jax.ShapeDtypeStruct((d, b * s), BF)],
        compiler_params=_params(2),
        name="out_proj_deepnorm",
    )(merged, w_out, x, g1, ln_g, ln_b, sh2, sc2)


N_HP = 2 * PEER_HEADS
HALF_K = PEER_TOPK // 2
BF16_ROWS = 16


def _top_values(s, k):
    tops = []
    for _ in range(k):
        m = jnp.max(s, axis=0, keepdims=True)
        tops.append(m)
        s = jnp.where(s == m, -jnp.inf, s)
    return tops


def _sort_network(n):
    pairs = []
    p = 1
    while p < n:
        k = p
        while k >= 1:
            for j in range(k % p, n - k, 2 * k):
                for i in range(min(k, n - j - k)):
                    if (i + j) // (2 * p) == (i + j + k) // (2 * p):
                        pairs.append((i + j, i + j + k))
            k //= 2
        p *= 2
    return pairs


SUBLANES = 8


def _top_values_sorted(s, k):
    n = s.shape[0] // SUBLANES
    v = [s[j * SUBLANES:(j + 1) * SUBLANES, :] for j in range(n)]
    for i, j in _sort_network(n):
        v[i], v[j] = jnp.maximum(v[i], v[j]), jnp.minimum(v[i], v[j])
    tops = []
    for r in range(k):
        m = jnp.max(v[0], axis=0, keepdims=True)
        tops.append(m)
        depth = k - 1 - r
        if depth == 0:
            break
        hit = v[0] == m
        for j in range(min(depth, n - 1)):
            v[j] = jnp.where(hit, v[j + 1], v[j])
        if depth > n - 1:
            v[n - 1] = jnp.where(hit, -jnp.inf, v[n - 1])
    return tops


def _rank_among(s, tops):
    rank = jnp.full(s.shape, float(len(tops)), F32)
    for r in range(len(tops) - 1, -1, -1):
        rank = jnp.where(s >= tops[r], float(r), rank)
    return rank


def _select_kernel(wq_ref, keys_ref, h_ref, rank_ref, cnt_ref, e1_ref, e2_ref,
                   s_scr, top_scr):
    q = _dot(h_ref[...], wq_ref[...]).astype(BF)
    for hp in range(N_HP):
        s_scr[hp] = lax.dot_general(keys_ref[hp], q[:, hp * PEER_HALF:(hp + 1) * PEER_HALF],
                                    (((1,), (1,)), ((), ())), preferred_element_type=F32)

    def head_body(h, carry):
        s1 = s_scr[2 * h]
        s2 = s_scr[2 * h + 1]
        tops1 = _top_values_sorted(s1, PEER_TOPK + 1)
        tops2 = _top_values_sorted(s2, PEER_TOPK + 1)
        rank2 = _rank_among(s2, tops2)
        for r in range(PEER_TOPK):
            top_scr[0, r:r + 1, :] = tops1[r]
            top_scr[1, r:r + 1, :] = tops2[r]
        t1 = top_scr[0]
        t2 = top_scr[1]
        m1, m2 = tops1[0], tops2[0]
        cand = jnp.concatenate(
            [m1 + t2]
            + [tops1[a] + t2[:HALF_K] for a in range(1, HALF_K)]
            + [t1[HALF_K:] + m2], axis=0)
        best = _top_values_sorted(cand, PEER_TOPK + 1)
        outside = jnp.maximum(tops1[PEER_TOPK] + m2, m1 + tops2[PEER_TOPK])
        runner_up = jnp.maximum(best[PEER_TOPK], outside)
        tau = 0.5 * (best[PEER_TOPK - 1] + runner_up)
        sel = cand >= tau
        z = jnp.sum(jnp.where(sel, jnp.exp(cand - (m1 + m2)), 0.0), axis=0, keepdims=True)
        self32 = sel.astype(F32)
        counts = [jnp.sum(self32[:PEER_TOPK], axis=0, keepdims=True)]
        for a in range(1, HALF_K):
            lo = PEER_TOPK + (a - 1) * HALF_K
            counts.append(jnp.sum(self32[lo:lo + HALF_K], axis=0, keepdims=True))
        lo = PEER_TOPK + (HALF_K - 1) * HALF_K
        for a in range(HALF_K, PEER_TOPK):
            counts.append(self32[lo + a - HALF_K:lo + a - HALF_K + 1])
        cnt = jnp.zeros_like(s1)
        for a in range(PEER_TOPK):
            cnt = jnp.where(s1 == tops1[a], counts[a], cnt)
        rank_ref[h] = rank2.astype(BF)
        cnt_ref[h] = cnt
        e1_ref[h] = jnp.exp(s1 - m1) * (0.5 / z)
        e2_ref[h] = jnp.exp(s2 - m2).astype(BF)
        return carry
    lax.fori_loop(0, PEER_HEADS, head_body, 0)


def _select_call(wq, keys, h, tn):
    t, d = h.shape
    shape = (PEER_HEADS, N_KEYS, t)
    ospec = pl.BlockSpec((PEER_HEADS, N_KEYS, tn), lambda i: (0, 0, i))
    return pl.pallas_call(
        _select_kernel,
        grid=(t // tn,),
        in_specs=[pl.BlockSpec(wq.shape, lambda i: (0, 0)),
                  pl.BlockSpec(keys.shape, lambda i: (0, 0, 0)),
                  pl.BlockSpec((tn, d), lambda i: (i, 0))],
        out_specs=[ospec, ospec, ospec, ospec],
        out_shape=[jax.ShapeDtypeStruct(shape, BF), jax.ShapeDtypeStruct(shape, F32),
                   jax.ShapeDtypeStruct(shape, F32), jax.ShapeDtypeStruct(shape, BF)],
        scratch_shapes=[pltpu.VMEM((N_HP, N_KEYS, tn), F32),
                        pltpu.VMEM((2, PEER_TOPK, tn), F32)],
        compiler_params=_params(1),
        name="peer_select",
    )(wq, keys, h)


def _gelu_times_two(x):
    return x * (1.0 + lax.erf(x * math.sqrt(0.5)))


def _peer_kernel(u_ref, vt_ref, ht_ref, rank_ref, cnt_ref, e1_ref, e2_ref, o_ref,
                 a_scr, c_scr, *, rows, act_slices, mix_slices):
    e = pl.program_id(1)
    tn = ht_ref.shape[1]
    d = vt_ref.shape[0]
    half = rows // 2
    hrows = half * N_KEYS

    @pl.when(e == 0)
    def _():
        o_ref[...] = jnp.zeros_like(o_ref)

    def coef_row(r):
        w = None
        for h in range(PEER_HEADS):
            cnt = jnp.broadcast_to(cnt_ref[h, r:r + 1, :], (BF16_ROWS, tn)).astype(BF)
            e1 = jnp.broadcast_to(e1_ref[h, r:r + 1, :], (BF16_ROWS, tn)).astype(BF)
            gate = jnp.where(rank_ref[h] < cnt[None], e1[None], jnp.zeros((), BF))
            w = e2_ref[h] * gate if w is None else w + e2_ref[h] * gate
        act = _gelu_times_two(a_scr[r * N_KEYS:(r + 1) * N_KEYS, :]).astype(BF)
        c_scr[r * N_KEYS:(r + 1) * N_KEYS, :] = w.reshape(N_KEYS, tn) * act

    ht = ht_ref[...]
    a_scr[0:hrows, :] = _dot(u_ref[0:hrows, :], ht)
    srows = hrows // act_slices
    for j in range(act_slices):
        lo = hrows + j * srows
        a_scr[lo:lo + srows, :] = _dot(u_ref[lo:lo + srows, :], ht)
        for r in range(j * half // act_slices, (j + 1) * half // act_slices):
            coef_row(r)
    mrows = d // mix_slices
    c_a = c_scr[0:hrows, :]
    for j in range(mix_slices):
        o_ref[j * mrows:(j + 1) * mrows, :] += _dot(vt_ref[j * mrows:(j + 1) * mrows, 0:hrows], c_a)
        for r in range(j * half // mix_slices, (j + 1) * half // mix_slices):
            coef_row(half + r)
    o_ref[...] += _dot(vt_ref[:, hrows:], c_scr[hrows:, :])


def _peer_call(u, v_t, h_t, rank2, cnt, e1, e2, tn, te, act_slices, mix_slices):
    d, t = h_t.shape
    rows = te // N_KEYS
    groups = N_KEYS // BF16_ROWS
    rank4 = rank2.reshape(PEER_HEADS, groups, BF16_ROWS, t)
    e24 = e2.reshape(PEER_HEADS, groups, BF16_ROWS, t)
    sel3 = pl.BlockSpec((PEER_HEADS, rows, tn), lambda i, e: (0, e, i))
    sel4 = pl.BlockSpec((PEER_HEADS, groups, BF16_ROWS, tn), lambda i, e: (0, 0, 0, i))
    return pl.pallas_call(
        functools.partial(_peer_kernel, rows=rows, act_slices=act_slices,
                          mix_slices=mix_slices),
        grid=(t // tn, N_EXPERTS // te),
        in_specs=[pl.BlockSpec((te, d), lambda i, e: (e, 0)),
                  pl.BlockSpec((d, te), lambda i, e: (0, e)),
                  pl.BlockSpec((d, tn), lambda i, e: (0, i)),
                  sel4, sel3, sel3, sel4],
        out_specs=pl.BlockSpec((d, tn), lambda i, e: (0, i)),
        out_shape=jax.ShapeDtypeStruct((d, t), F32),
        scratch_shapes=[pltpu.VMEM((te, tn), F32), pltpu.VMEM((te, tn), BF)],
        compiler_params=_params(2),
        name="peer_dense",
    )(u, v_t, h_t, rank4, cnt, e1, e24)


def _final_kernel(yt_ref, x_ref, g2_ref, lg_ref, lb_ref, o_ref):
    z = DEEPNORM_ALPHA * x_ref[0] + g2_ref[0] * yt_ref[...].T
    o_ref[0] = _layer_norm_rows(z) * lg_ref[...] + lb_ref[...]


def _final_call(y_t, x1, g2, ln_g, ln_b, tm):
    b, s, d = x1.shape
    nb = s // tm
    return pl.pallas_call(
        _final_kernel,
        grid=(b, nb),
        in_specs=[pl.BlockSpec((d, tm), lambda i, j: (0, i * nb + j)),
                  pl.BlockSpec((1, tm, d), lambda i, j: (i, j, 0)),
                  pl.BlockSpec((1, 1, d), lambda i, j: (i, 0, 0)),
                  pl.BlockSpec((1, d), lambda i, j: (0, 0)),
                  pl.BlockSpec((1, d), lambda i, j: (0, 0))],
        out_specs=pl.BlockSpec((1, tm, d), lambda i, j: (i, j, 0)),
        out_shape=jax.ShapeDtypeStruct((b, s, d), F32),
        compiler_params=_params(2),
        name="final_deepnorm",
    )(y_t, x1, g2, ln_g, ln_b)


def _rope_rotation(w):
    pairs = w.reshape(w.shape[:-1] + (w.shape[-1] // 2, 2))
    return jnp.stack([-pairs[..., 1], pairs[..., 0]], axis=-1).reshape(w.shape)


def _pad_lanes(w):
    return jnp.pad(w, [(0, 0)] * (w.ndim - 1) + [(0, LANES - w.shape[-1])])


def _rope_tables(seq):
    rows = seq // GRID_W
    row = jnp.repeat(jnp.arange(rows, dtype=F32), GRID_W)
    col = jnp.tile(jnp.arange(GRID_W, dtype=F32), rows)
    half = QK_ROPE // 2
    inv = ROPE_THETA ** (-jnp.arange(0, half, 2, dtype=F32) / half)
    ang = jnp.concatenate([row[:, None] * inv, col[:, None] * inv], axis=-1)
    cos = _pad_lanes(jnp.repeat(jnp.cos(ang), 2, axis=-1))
    sin = _pad_lanes(jnp.repeat(jnp.sin(ang), 2, axis=-1))
    return cos, sin


def _dft_matrices(n, scale):
    k = np.arange(n, dtype=np.int64)
    ang = 2.0 * np.pi * ((k[:, None] * k[None, :]) % n).astype(np.float64) / n
    return np.cos(ang) * scale, np.sin(ang) * scale


def kernel(x, c, ctx, c_ctx, w_mod, b_mod, w_in, b_in, q_norm_g, w_uq, kv_norm_g, w_ukv,
           w_o_mla, w_fourier, w_out, ln1_g, ln1_b, peer_wq, peer_keys, peer_u, peer_v,
           ln2_g, ln2_b):
    B, S, D = x.shape
    T = B * S
    CT = ctx.shape[1]
    l = 0

    cmat = jnp.concatenate([c, c_ctx[None, :], jnp.zeros((8 - B - 1, D), F32)], axis=0)
    mod = _mod_call(cmat, w_mod[l], b_mod[l])
    mx = mod[:B].reshape(B, 1, 6, D)
    sh1, sc1, g1, sh2, sc2, g2 = [mx[:, :, i, :] for i in range(6)]
    mc = mod[B].reshape(1, 1, 6, D)
    sh1c, sc1c = mc[:, :, 0, :], mc[:, :, 1, :]

    wt, bi = w_in[l].T, b_in[l]
    w_kr, b_kr = wt[KV_LORA:KV_END].T, bi[KV_LORA:KV_END]
    q0 = KV_END
    f0 = KV_END + Q_LORA
    g0 = f0 + FOURIER_DIM
    w_all = jnp.concatenate(
        [wt[g0:], wt[f0:g0],
         wt[:KV_LORA], _pad_lanes(w_kr).T, _pad_lanes(_rope_rotation(w_kr)).T, wt[q0:f0]],
        axis=0).astype(BF)
    b_all = jnp.concatenate(
        [bi[g0:], bi[f0:g0],
         bi[:KV_LORA], _pad_lanes(b_kr), _pad_lanes(_rope_rotation(b_kr)), bi[q0:f0]])[None, :]
    wq3 = w_uq[l].reshape(Q_LORA, N_HEADS, QK_NOPE + QK_ROPE)
    wq_rope = wq3[:, :, QK_NOPE:]
    w_q = jnp.concatenate(
        [wq3[:, :, :QK_NOPE], _pad_lanes(wq_rope), _pad_lanes(_rope_rotation(wq_rope))],
        axis=-1).reshape(Q_LORA, N_HEADS * Q_HEAD_COLS).astype(BF)
    w_kv = w_ukv[l].astype(BF)
    gkv = kv_norm_g[l][None, :]
    gq = q_norm_g[l][None, :]

    cos, sin = _rope_tables(S)
    cos_c = _pad_lanes(jnp.ones((CT, QK_ROPE), F32))
    sin_c = jnp.zeros((CT, LANES), F32)
    dc_c, dc_s = _dft_matrices(FOURIER_GROUP_DIM, FOURIER_GROUP_DIM ** -0.5)
    dc = jnp.asarray(np.concatenate([dc_c, dc_s], axis=1), dtype=F32).astype(BF)
    ds_c, ds_s = _dft_matrices(S, S ** -0.5)
    cs = jnp.asarray(ds_c, dtype=F32).astype(BF)
    ss = jnp.asarray(ds_s, dtype=F32).astype(BF)

    hx = _ln_mod_call(x, sh1, sc1, 512)
    hc = _ln_mod_call(ctx, sh1c, sc1c, CT)
    ckv_x, kr_x, cq_x = _latent_call(hx, w_all, b_all, gkv, gq, cos, sin, 512)
    ckv_c, kr_c, _ = _latent_call(hc, w_all, b_all, gkv, gq, cos_c, sin_c, CT)
    lat = jnp.concatenate([ckv_c, ckv_x], axis=1)
    kr = jnp.concatenate([kr_c, kr_x], axis=1)
    k_all, v_all = _kv_up_call(lat, kr, w_kv, 768)
    q_all = _q_up_call(cq_x, w_q, cos, sin, 512)
    attn, u_b, v_t, (w_o_b, w_f_b, w_out_b, wq_b) = _attn_call(
        q_all, k_all, v_all, peer_u[l], peer_v[l],
        [w_o_mla[l], w_fourier[l], w_out[l], peer_wq[l]], 1024, 256)

    hx2d = hx.reshape(T, D)
    gc, gs = _fproj_call(hx2d, w_all, b_all, dc, 512)
    fm = _pos_dft_call(cs, ss, gc.reshape(B, S, FOURIER_DIM), gs.reshape(B, S, FOURIER_DIM),
                       512, 512)
    gates = _gate_call(hx2d, w_all, b_all, 512, 1024, 2)
    merged = _merge_call(attn.reshape(T, D), fm.reshape(T, FOURIER_DIM),
                         w_o_b, w_f_b, gates, 512, 1024, 2)
    x1, h2, h2_t = _outproj_call(merged, w_out_b, x, g1, ln1_g[l][None, :],
                                 ln1_b[l][None, :], sh2, sc2, 512)

    keys = peer_keys[l].reshape(N_HP, N_KEYS, PEER_HALF).astype(BF)
    rank2, cnt, e1, e2 = _select_call(wq_b, keys, h2, 512)
    y_t = _peer_call(u_b, v_t, h2_t, rank2, cnt, e1, e2, 512, 1024, 2, 4)
    return _final_call(y_t, x1, g2, ln2_g[l][None, :], ln2_b[l][None, :], 256)
```

```python
import functools
import math

import numpy as np
import jax
import jax.numpy as jnp
from jax import lax
from jax.experimental import pallas as pl
from jax.experimental.pallas import tpu as pltpu

D_MODEL = 2048
GRID_W = 64
N_HEADS = 16
QK_NOPE = 128
QK_ROPE = 64
V_DIM = 128
Q_LORA = 512
KV_LORA = 512
ROPE_THETA = 10000.0
N_FOURIER_GROUPS = 4
FOURIER_GROUP_DIM = 256
FOURIER_DIM = N_FOURIER_GROUPS * FOURIER_GROUP_DIM
KV_END = KV_LORA + QK_ROPE
PEER_HEADS = 8
N_KEYS = 128
N_EXPERTS = N_KEYS * N_KEYS
PEER_HALF = 128
PEER_TOPK = 16
DEPTH = 1
DEEPNORM_ALPHA = (2.0 * DEPTH) ** 0.25
EPS = 1e-6

LANES = 128
QK_PAD = 2 * LANES
VMEM_LIMIT = 56 * 1024 * 1024

BF = jnp.bfloat16
F32 = jnp.float32


def _params(n_axes, vmem=VMEM_LIMIT):
    return pltpu.CompilerParams(
        dimension_semantics=("arbitrary",) * n_axes, vmem_limit_bytes=vmem)


def _dot(a, b):
    return jnp.dot(a, b, preferred_element_type=F32)


def _dot_nt(a, b):
    return lax.dot_general(a, b, (((1,), (1,)), ((), ())), preferred_element_type=F32)


def _layer_norm_rows(x):
    mu = jnp.mean(x, axis=-1, keepdims=True)
    xc = x - mu
    var = jnp.mean(xc * xc, axis=-1, keepdims=True)
    return xc * lax.rsqrt(var + EPS)


def _mod_kernel(c_ref, w_ref, b_ref, o_ref):
    a = jax.nn.silu(c_ref[...]).astype(BF)
    o_ref[...] = _dot(a, w_ref[...].astype(BF)) + b_ref[...]


def _mod_call(cmat, w_mod, b_mod):
    n = w_mod.shape[1]
    tn = 1024
    return pl.pallas_call(
        _mod_kernel,
        grid=(n // tn,),
        in_specs=[pl.BlockSpec((8, D_MODEL), lambda j: (0, 0)),
                  pl.BlockSpec((D_MODEL, tn), lambda j: (0, j)),
                  pl.BlockSpec((1, tn), lambda j: (0, j))],
        out_specs=pl.BlockSpec((8, tn), lambda j: (0, j)),
        out_shape=jax.ShapeDtypeStruct((8, n), F32),
        compiler_params=_params(1),
        name="adaln_mod",
    )(cmat, w_mod, b_mod.reshape(1, n))


def _ln_mod_kernel(x_ref, sh_ref, sc_ref, o_ref):
    y = _layer_norm_rows(x_ref[0])
    o_ref[0] = (y * (1.0 + sc_ref[0]) + sh_ref[0]).astype(BF)


def _ln_mod_call(x, shift, scale, tm):
    b, s, d = x.shape
    bm = shift.shape[0]
    mod_map = (lambda i, j: (i, 0, 0)) if bm == b else (lambda i, j: (0, 0, 0))
    return pl.pallas_call(
        _ln_mod_kernel,
        grid=(b, s // tm),
        in_specs=[pl.BlockSpec((1, tm, d), lambda i, j: (i, j, 0)),
                  pl.BlockSpec((1, 1, d), mod_map),
                  pl.BlockSpec((1, 1, d), mod_map)],
        out_specs=pl.BlockSpec((1, tm, d), lambda i, j: (i, j, 0)),
        out_shape=jax.ShapeDtypeStruct((b, s, d), BF),
        compiler_params=_params(2),
        name="ln_modulate",
    )(x, shift, scale)


LAT_COLS = KV_LORA + 2 * LANES + Q_LORA
GATE_COLS = 2 * D_MODEL
FOURIER_BLOCK = GATE_COLS // FOURIER_DIM
LAT_BLOCK = (GATE_COLS + FOURIER_DIM) // LAT_COLS
assert GATE_COLS % FOURIER_DIM == 0 and (GATE_COLS + FOURIER_DIM) % LAT_COLS == 0


def _latent_kernel(h_ref, w_ref, b_ref, gkv_ref, gq_ref, cos_ref, sin_ref,
                   ckv_ref, kr_ref, cq_ref):
    acc = _dot_nt(h_ref[0], w_ref[...]) + b_ref[...]
    ckv = acc[:, :KV_LORA]
    ka = acc[:, KV_LORA:KV_LORA + LANES]
    kb = acc[:, KV_LORA + LANES:KV_LORA + 2 * LANES]
    cq = acc[:, KV_LORA + 2 * LANES:]
    ckv_n = ckv * lax.rsqrt(jnp.mean(ckv * ckv, axis=-1, keepdims=True) + EPS)
    cq_n = cq * lax.rsqrt(jnp.mean(cq * cq, axis=-1, keepdims=True) + EPS)
    ckv_ref[0] = (ckv_n * gkv_ref[...]).astype(BF)
    cq_ref[0] = (cq_n * gq_ref[...]).astype(BF)
    kr_ref[0] = (ka * cos_ref[...] + kb * sin_ref[...]).astype(BF)


def _latent_call(h, w_all, b_all, gkv, gq, cos, sin, tm):
    b, s, d = h.shape
    row = lambda i, j: (i, j, 0)
    const = lambda i, j: (0, 0)
    return pl.pallas_call(
        _latent_kernel,
        grid=(b, s // tm),
        in_specs=[pl.BlockSpec((1, tm, d), row),
                  pl.BlockSpec((LAT_COLS, d), lambda i, j: (LAT_BLOCK, 0)),
                  pl.BlockSpec((1, LAT_COLS), lambda i, j: (0, LAT_BLOCK)),
                  pl.BlockSpec((1, KV_LORA), const),
                  pl.BlockSpec((1, Q_LORA), const),
                  pl.BlockSpec((tm, LANES), lambda i, j: (j, 0)),
                  pl.BlockSpec((tm, LANES), lambda i, j: (j, 0))],
        out_specs=[pl.BlockSpec((1, tm, KV_LORA), row),
                   pl.BlockSpec((1, tm, LANES), row),
                   pl.BlockSpec((1, tm, Q_LORA), row)],
        out_shape=[jax.ShapeDtypeStruct((b, s, KV_LORA), BF),
                   jax.ShapeDtypeStruct((b, s, LANES), BF),
                   jax.ShapeDtypeStruct((b, s, Q_LORA), BF)],
        compiler_params=_params(2),
        name="latent_proj",
    )(h, w_all, b_all, gkv, gq, cos, sin)


def _fproj_kernel(h_ref, w_ref, b_ref, dc_ref, gc_ref, gs_ref):
    f = (_dot_nt(h_ref[...], w_ref[...]) + b_ref[...]).astype(BF)
    for g in range(N_FOURIER_GROUPS):
        lo = g * FOURIER_GROUP_DIM
        r = _dot(f[:, lo:lo + FOURIER_GROUP_DIM], dc_ref[...])
        gc_ref[:, lo:lo + FOURIER_GROUP_DIM] = r[:, :FOURIER_GROUP_DIM].astype(BF)
        gs_ref[:, lo:lo + FOURIER_GROUP_DIM] = r[:, FOURIER_GROUP_DIM:].astype(BF)


def _fproj_call(h2d, w_all, b_all, dc, tm):
    t, d = h2d.shape
    const = lambda i: (0, 0)
    return pl.pallas_call(
        _fproj_kernel,
        grid=(t // tm,),
        in_specs=[pl.BlockSpec((tm, d), lambda i: (i, 0)),
                  pl.BlockSpec((FOURIER_DIM, d), lambda i: (FOURIER_BLOCK, 0)),
                  pl.BlockSpec((1, FOURIER_DIM), lambda i: (0, FOURIER_BLOCK)),
                  pl.BlockSpec((FOURIER_GROUP_DIM, 2 * FOURIER_GROUP_DIM), const)],
        out_specs=[pl.BlockSpec((tm, FOURIER_DIM), lambda i: (i, 0)),
                   pl.BlockSpec((tm, FOURIER_DIM), lambda i: (i, 0))],
        out_shape=[jax.ShapeDtypeStruct((t, FOURIER_DIM), BF),
                   jax.ShapeDtypeStruct((t, FOURIER_DIM), BF)],
        compiler_params=_params(1),
        name="fourier_in_proj",
    )(h2d, w_all, b_all, dc)


def _gate_kernel(h_ref, w_ref, b_ref, o_ref, *, pieces):
    h = h_ref[...]
    pn = w_ref.shape[0] // pieces
    for p in range(pieces):
        cols = slice(p * pn, (p + 1) * pn)
        o_ref[:, cols] = jax.nn.sigmoid(_dot_nt(h, w_ref[cols, :]) + b_ref[:, cols]).astype(BF)


def _gate_call(h2d, w_all, b_all, tm, tn, pieces):
    t, d = h2d.shape
    n = GATE_COLS
    return pl.pallas_call(
        functools.partial(_gate_kernel, pieces=pieces),
        grid=(n // tn, t // tm),
        in_specs=[pl.BlockSpec((tm, d), lambda j, i: (i, 0)),
                  pl.BlockSpec((tn, d), lambda j, i: (j, 0)),
                  pl.BlockSpec((1, tn), lambda j, i: (0, j))],
        out_specs=pl.BlockSpec((tm, tn), lambda j, i: (i, j)),
        out_shape=jax.ShapeDtypeStruct((t, n), BF),
        compiler_params=_params(2),
        name="gate_proj",
    )(h2d, w_all, b_all)


def _kv_up_kernel(lat_ref, kr_ref, w_ref, k_ref, v_ref):
    lat = lat_ref[0]
    kr = kr_ref[0]
    for h in range(N_HEADS):
        lo = h * (QK_NOPE + V_DIM)
        kv = _dot(lat, w_ref[:, lo:lo + QK_NOPE + V_DIM])
        k_ref[0, h] = jnp.concatenate([kv[:, :QK_NOPE].astype(BF), kr], axis=-1)
        v_ref[0, h] = kv[:, QK_NOPE:].astype(BF)


def _kv_up_call(lat, kr, w_ukv, tm):
    b, t, _ = lat.shape
    return pl.pallas_call(
        _kv_up_kernel,
        grid=(b, t // tm),
        in_specs=[pl.BlockSpec((1, tm, KV_LORA), lambda i, j: (i, j, 0)),
                  pl.BlockSpec((1, tm, LANES), lambda i, j: (i, j, 0)),
                  pl.BlockSpec(w_ukv.shape, lambda i, j: (0, 0))],
        out_specs=[pl.BlockSpec((1, N_HEADS, tm, QK_PAD), lambda i, j: (i, 0, j, 0)),
                   pl.BlockSpec((1, N_HEADS, tm, V_DIM), lambda i, j: (i, 0, j, 0))],
        out_shape=[jax.ShapeDtypeStruct((b, N_HEADS, t, QK_PAD), BF),
                   jax.ShapeDtypeStruct((b, N_HEADS, t, V_DIM), BF)],
        compiler_params=_params(2),
        name="kv_up_proj",
    )(lat, kr, w_ukv)


Q_HEAD_COLS = 2 * LANES


def _q_up_kernel(cq_ref, w_ref, cos_ref, sin_ref, q_ref):
    cq = cq_ref[0]
    cos = cos_ref[...]
    sin = sin_ref[...]
    scale = (QK_NOPE + QK_ROPE) ** -0.5
    for h in range(N_HEADS):
        lo = h * Q_HEAD_COLS
        acc = _dot(cq, w_ref[:, lo:lo + Q_HEAD_COLS])
        qn = acc[:, :LANES]
        x = acc[:, LANES:]
        qr = x * cos + pltpu.roll(x, QK_ROPE, 1) * sin
        q_ref[0, h] = (jnp.concatenate([qn, qr], axis=-1) * scale).astype(BF)


def _q_up_call(cq, w_q, cos, sin, tm):
    b, s, _ = cq.shape
    return pl.pallas_call(
        _q_up_kernel,
        grid=(b, s // tm),
        in_specs=[pl.BlockSpec((1, tm, Q_LORA), lambda i, j: (i, j, 0)),
                  pl.BlockSpec(w_q.shape, lambda i, j: (0, 0)),
                  pl.BlockSpec((tm, LANES), lambda i, j: (j, 0)),
                  pl.BlockSpec((tm, LANES), lambda i, j: (j, 0))],
        out_specs=pl.BlockSpec((1, N_HEADS, tm, QK_PAD), lambda i, j: (i, 0, j, 0)),
        out_shape=jax.ShapeDtypeStruct((b, N_HEADS, s, QK_PAD), BF),
        compiler_params=_params(2),
        name="q_up_proj",
    )(cq, w_q, cos, sin)


ATTN_LAG = 1


def _attn_kernel(q_ref, k_ref, v_ref, eu_ref, ev_ref, *rest, kc, n_cast):
    cast_in = rest[:n_cast]
    o_ref, ub_ref, vt_ref = rest[n_cast:n_cast + 3]
    cast_out = rest[n_cast + 3:2 * n_cast + 3]
    s_scr, m_scr = rest[2 * n_cast + 3:]
    n = pl.program_id(0)
    tq = q_ref.shape[2]
    t = k_ref.shape[2]

    @pl.when(n == 0)
    def _():
        s_scr[...] = jnp.zeros_like(s_scr)
        m_scr[...] = jnp.zeros_like(m_scr)

    def step(cur, prev):
        q = q_ref[0, 0]
        m_prev = m_scr[prev]
        mrun = None
        lrun = jnp.zeros((tq, LANES), F32)
        acc = jnp.zeros((tq, V_DIM), F32)
        for c in range(t // kc):
            ks = slice(c * kc, (c + 1) * kc)
            s_c = lax.dot_general(q, k_ref[0, 0, ks, :], (((1,), (1,)), ((), ())),
                                  preferred_element_type=F32)
            s_scr[cur, :, ks] = s_c
            pieces = []
            for j in range(kc // LANES):
                lanes = slice(j * LANES, (j + 1) * LANES)
                col = slice(c * kc + j * LANES, c * kc + (j + 1) * LANES)
                p_j = jnp.exp(s_scr[prev, :, col] - m_prev)
                lrun = lrun + p_j
                pieces.append(p_j.astype(BF))
                mrun = s_c[:, lanes] if mrun is None else jnp.maximum(mrun, s_c[:, lanes])
            acc = acc + _dot(jnp.concatenate(pieces, axis=-1), v_ref[0, 0, ks, :])
        o_ref[0] = (acc / jnp.sum(lrun, axis=-1, keepdims=True)).astype(BF)
        m_scr[cur] = jnp.broadcast_to(jnp.max(mrun, axis=-1, keepdims=True), (tq, LANES))
        ub_ref[...] = eu_ref[...].astype(BF)
        vt_ref[...] = ev_ref[...].T.astype(BF)
        for src, dst in zip(cast_in, cast_out):
            dst[...] = src[...].astype(BF)

    @pl.when(n % 2 == 0)
    def _():
        step(0, 1)

    @pl.when(n % 2 == 1)
    def _():
        step(1, 0)


CAST_STEPS = 64


def _attn_call(q, k, v, eu, ev, weights, tq, kc):
    b, h, s, _ = q.shape
    t = k.shape[2]
    nq = s // tq
    total = b * h * nq
    ne, ed = eu.shape
    slab = ne // total
    assert slab * total == ne and slab % LANES == 0 and total >= CAST_STEPS
    slab_idx = lambda n: jnp.minimum(n, total - 1)
    cast_idx = lambda n: (jnp.minimum(n, CAST_STEPS - 1), 0)
    cast_specs = []
    for w in weights:
        assert w.shape[0] % (CAST_STEPS * BF16_ROWS) == 0
        cast_specs.append(pl.BlockSpec((w.shape[0] // CAST_STEPS, w.shape[1]), cast_idx))

    def block(n, lag):
        i = jnp.clip(n - lag, 0, total - 1)
        return i // (h * nq), (i // nq) % h, i % nq

    def q_map(n):
        bi, hi, qi = block(n, 0)
        return bi, hi, qi, 0

    def k_map(n):
        bi, hi, _ = block(n, 0)
        return bi, hi, 0, 0

    def v_map(n):
        bi, hi, _ = block(n, ATTN_LAG)
        return bi, hi, 0, 0

    def o_map(n):
        bi, hi, qi = block(n, ATTN_LAG)
        return bi, qi, hi

    outs = pl.pallas_call(
        functools.partial(_attn_kernel, kc=kc, n_cast=len(weights)),
        grid=(total + ATTN_LAG,),
        in_specs=[pl.BlockSpec((1, 1, tq, QK_PAD), q_map),
                  pl.BlockSpec((1, 1, t, QK_PAD), k_map),
                  pl.BlockSpec((1, 1, t, V_DIM), v_map),
                  pl.BlockSpec((slab, ed), lambda n: (slab_idx(n), 0)),
                  pl.BlockSpec((slab, ed), lambda n: (slab_idx(n), 0))] + cast_specs,
        out_specs=[pl.BlockSpec((1, tq, V_DIM), o_map),
                   pl.BlockSpec((slab, ed), lambda n: (slab_idx(n), 0)),
                   pl.BlockSpec((ed, slab), lambda n: (0, slab_idx(n)))] + cast_specs,
        out_shape=[jax.ShapeDtypeStruct((b, s, h * V_DIM), BF),
                   jax.ShapeDtypeStruct((ne, ed), BF),
                   jax.ShapeDtypeStruct((ed, ne), BF)]
        + [jax.ShapeDtypeStruct(w.shape, BF) for w in weights],
        scratch_shapes=[pltpu.VMEM((2, tq, t), F32), pltpu.VMEM((2, tq, LANES), F32)],
        compiler_params=_params(1),
        name="mla_attention",
    )(q, k, v, eu, ev, *weights)
    return outs[0], outs[1], outs[2], outs[3:]


def _pos_dft_kernel(c_ref, s_ref, gc_ref, gs_ref, o_ref):
    o_ref[0] = (_dot(c_ref[...], gc_ref[0]) - _dot(s_ref[...], gs_ref[0])).astype(BF)


def _pos_dft_call(cs, ss, gc, gs, tm, tn):
    b, s, n = gc.shape
    return pl.pallas_call(
        _pos_dft_kernel,
        grid=(b, n // tn, s // tm),
        in_specs=[pl.BlockSpec((tm, s), lambda i, j, m: (m, 0)),
                  pl.BlockSpec((tm, s), lambda i, j, m: (m, 0)),
                  pl.BlockSpec((1, s, tn), lambda i, j, m: (i, 0, j)),
                  pl.BlockSpec((1, s, tn), lambda i, j, m: (i, 0, j))],
        out_specs=pl.BlockSpec((1, tm, tn), lambda i, j, m: (i, m, j)),
        out_shape=jax.ShapeDtypeStruct((b, s, n), BF),
        compiler_params=_params(3),
        name="position_dft",
    )(cs, ss, gc, gs)


def _merge_kernel(a_ref, f_ref, wo_ref, wf_ref, ga_ref, gb_ref, o_ref, *, pieces):
    a = a_ref[...]
    f = f_ref[...]
    pn = wo_ref.shape[1] // pieces
    for p in range(pieces):
        cols = slice(p * pn, (p + 1) * pn)
        ya = _dot(a, wo_ref[:, cols])
        yb = _dot(f, wf_ref[:, cols])
        o_ref[:, cols] = (ga_ref[:, cols].astype(F32) * ya
                          + gb_ref[:, cols].astype(F32) * yb).astype(BF)


def _merge_call(attn, fm, w_o, w_f, gates, tm, tn, pieces):
    t, d = attn.shape
    nb = D_MODEL // tn
    return pl.pallas_call(
        functools.partial(_merge_kernel, pieces=pieces),
        grid=(nb, t // tm),
        in_specs=[pl.BlockSpec((tm, d), lambda j, i: (i, 0)),
                  pl.BlockSpec((tm, FOURIER_DIM), lambda j, i: (i, 0)),
                  pl.BlockSpec((d, tn), lambda j, i: (0, j)),
                  pl.BlockSpec((FOURIER_DIM, tn), lambda j, i: (0, j)),
                  pl.BlockSpec((tm, tn), lambda j, i: (i, j)),
                  pl.BlockSpec((tm, tn), lambda j, i: (i, j + nb))],
        out_specs=pl.BlockSpec((tm, tn), lambda j, i: (i, j)),
        out_shape=jax.ShapeDtypeStruct((t, D_MODEL), BF),
        compiler_params=_params(2),
        name="branch_merge",
    )(attn, fm, w_o, w_f, gates, gates)


def _outproj_kernel(m_ref, w_ref, x_ref, g1_ref, lg_ref, lb_ref, sh_ref, sc_ref,
                    x1_ref, h_ref, ht_ref):
    y = _dot(m_ref[...], w_ref[...])
    z = DEEPNORM_ALPHA * x_ref[0] + g1_ref[0] * y
    x1 = _layer_norm_rows(z) * lg_ref[...] + lb_ref[...]
    x1_ref[0] = x1
    h2 = _layer_norm_rows(x1) * (1.0 + sc_ref[0]) + sh_ref[0]
    h_ref[...] = h2.astype(BF)
    ht_ref[...] = h2.T.astype(BF)


def _outproj_call(merged, w_out, x, g1, ln_g, ln_b, sh2, sc2, tm):
    b, s, d = x.shape
    nb = s // tm
    bmap = lambda i, j: (i, 0, 0)
    const = lambda i, j: (0, 0)
    return pl.pallas_call(
        _outproj_kernel,
        grid=(b, nb),
        in_specs=[pl.BlockSpec((tm, d), lambda i, j: (i * nb + j, 0)),
                  pl.BlockSpec((d, d), const),
                  pl.BlockSpec((1, tm, d), lambda i, j: (i, j, 0)),
                  pl.BlockSpec((1, 1, d), bmap),
                  pl.BlockSpec((1, d), const),
                  pl.BlockSpec((1, d), const),
                  pl.BlockSpec((1, 1, d), bmap),
                  pl.BlockSpec((1, 1, d), bmap)],
        out_specs=[pl.BlockSpec((1, tm, d), lambda i, j: (i, j, 0)),
                   pl.BlockSpec((tm, d), lambda i, j: (i * nb + j, 0)),
                   pl.BlockSpec((d, tm), lambda i, j: (0, i * nb + j))],
        out_shape=[jax.ShapeDtypeStruct((b, s, d), F32),
                   jax.ShapeDtypeStruct((b * s, d), BF),
                   jax.ShapeDtypeStruct((d, b * s), BF)],
        compiler_params=_params(2),
        name="out_proj_deepnorm",
    )(merged, w_out, x, g1, ln_g, ln_b, sh2, sc2)


N_HP = 2 * PEER_HEADS
HALF_K = PEER_TOPK // 2
BF16_ROWS = 16


def _top_values(s, k):
    tops = []
    for _ in range(k):
        m = jnp.max(s, axis=0, keepdims=True)
        tops.append(m)
        s = jnp.where(s == m, -jnp.inf, s)
    return tops


def _sort_network(n):
    pairs = []
    p = 1
    while p < n:
        k = p
        while k >= 1:
            for j in range(k % p, n - k, 2 * k):
                for i in range(min(k, n - j - k)):
                    if (i + j) // (2 * p) == (i + j + k) // (2 * p):
                        pairs.append((i + j, i + j + k))
            k //= 2
        p *= 2
    return pairs


SUBLANES = 8


def _top_values_sorted(s, k):
    n = s.shape[0] // SUBLANES
    v = [s[j * SUBLANES:(j + 1) * SUBLANES, :] for j in range(n)]
    for i, j in _sort_network(n):
        v[i], v[j] = jnp.maximum(v[i], v[j]), jnp.minimum(v[i], v[j])
    tops = []
    for r in range(k):
        m = jnp.max(v[0], axis=0, keepdims=True)
        tops.append(m)
        depth = k - 1 - r
        if depth == 0:
            break
        hit = v[0] == m
        for j in range(min(depth, n - 1)):
            v[j] = jnp.where(hit, v[j + 1], v[j])
        if depth > n - 1:
            v[n - 1] = jnp.where(hit, -jnp.inf, v[n - 1])
    return tops


def _rank_among(s, tops):
    rank = jnp.full(s.shape, float(len(tops)), F32)
    for r in range(len(tops) - 1, -1, -1):
        rank = jnp.where(s >= tops[r], float(r), rank)
    return rank


def _select_kernel(wq_ref, keys_ref, h_ref, rank_ref, cnt_ref, e1_ref, e2_ref,
                   s_scr, top_scr):
    q = _dot(h_ref[...], wq_ref[...]).astype(BF)
    for hp in range(N_HP):
        s_scr[hp] = lax.dot_general(keys_ref[hp], q[:, hp * PEER_HALF:(hp + 1) * PEER_HALF],
                                    (((1,), (1,)), ((), ())), preferred_element_type=F32)

    def head_body(h, carry):
        s1 = s_scr[2 * h]
        s2 = s_scr[2 * h + 1]
        tops1 = _top_values_sorted(s1, PEER_TOPK + 1)
        tops2 = _top_values_sorted(s2, PEER_TOPK + 1)
        rank2 = _rank_among(s2, tops2)
        for r in range(PEER_TOPK):
            top_scr[0, r:r + 1, :] = tops1[r]
            top_scr[1, r:r + 1, :] = tops2[r]
        t1 = top_scr[0]
        t2 = top_scr[1]
        m1, m2 = tops1[0], tops2[0]
        cand = jnp.concatenate(
            [m1 + t2]
            + [tops1[a] + t2[:HALF_K] for a in range(1, HALF_K)]
            + [t1[HALF_K:] + m2], axis=0)
        best = _top_values_sorted(cand, PEER_TOPK + 1)
        outside = jnp.maximum(tops1[PEER_TOPK] + m2, m1 + tops2[PEER_TOPK])
        runner_up = jnp.maximum(best[PEER_TOPK], outside)
        tau = 0.5 * (best[PEER_TOPK - 1] + runner_up)
        sel = cand >= tau
        z = jnp.sum(jnp.where(sel, jnp.exp(cand - (m1 + m2)), 0.0), axis=0, keepdims=True)
        self32 = sel.astype(F32)
        counts = [jnp.sum(self32[:PEER_TOPK], axis=0, keepdims=True)]
        for a in range(1, HALF_K):
            lo = PEER_TOPK + (a - 1) * HALF_K
            counts.append(jnp.sum(self32[lo:lo + HALF_K], axis=0, keepdims=True))
        lo = PEER_TOPK + (HALF_K - 1) * HALF_K
        for a in range(HALF_K, PEER_TOPK):
            counts.append(self32[lo + a - HALF_K:lo + a - HALF_K + 1])
        cnt = jnp.zeros_like(s1)
        for a in range(PEER_TOPK):
            cnt = jnp.where(s1 == tops1[a], counts[a], cnt)
        rank_ref[h] = rank2.astype(BF)
        cnt_ref[h] = cnt
        e1_ref[h] = jnp.exp(s1 - m1) * (0.5 / z)
        e2_ref[h] = jnp.exp(s2 - m2).astype(BF)
        return carry
    lax.fori_loop(0, PEER_HEADS, head_body, 0)


def _select_call(wq, keys, h, tn):
    t, d = h.shape
    shape = (PEER_HEADS, N_KEYS, t)
    ospec = pl.BlockSpec((PEER_HEADS, N_KEYS, tn), lambda i: (0, 0, i))
    return pl.pallas_call(
        _select_kernel,
        grid=(t // tn,),
        in_specs=[pl.BlockSpec(wq.shape, lambda i: (0, 0)),
                  pl.BlockSpec(keys.shape, lambda i: (0, 0, 0)),
                  pl.BlockSpec((tn, d), lambda i: (i, 0))],
        out_specs=[ospec, ospec, ospec, ospec],
        out_shape=[jax.ShapeDtypeStruct(shape, BF), jax.ShapeDtypeStruct(shape, F32),
                   jax.ShapeDtypeStruct(shape, F32), jax.ShapeDtypeStruct(shape, BF)],
        scratch_shapes=[pltpu.VMEM((N_HP, N_KEYS, tn), F32),
                        pltpu.VMEM((2, PEER_TOPK, tn), F32)],
        compiler_params=_params(1),
        name="peer_select",
    )(wq, keys, h)


def _gelu_times_two(x):
    return x * (1.0 + lax.erf(x * math.sqrt(0.5)))


def _peer_kernel(u_ref, vt_ref, ht_ref, rank_ref, cnt_ref, e1_ref, e2_ref, o_ref,
                 a_scr, c_scr, *, rows, act_slices, mix_slices):
    e = pl.program_id(1)
    tn = ht_ref.shape[1]
    d = vt_ref.shape[0]
    half = rows // 2
    hrows = half * N_KEYS

    @pl.when(e == 0)
    def _():
        o_ref[...] = jnp.zeros_like(o_ref)

    def coef_row(r):
        w = None
        for h in range(PEER_HEADS):
            cnt = jnp.broadcast_to(cnt_ref[h, r:r + 1, :], (BF16_ROWS, tn)).astype(BF)
            e1 = jnp.broadcast_to(e1_ref[h, r:r + 1, :], (BF16_ROWS, tn)).astype(BF)
            gate = jnp.where(rank_ref[h] < cnt[None], e1[None], jnp.zeros((), BF))
            w = e2_ref[h] * gate if w is None else w + e2_ref[h] * gate
        act = _gelu_times_two(a_scr[r * N_KEYS:(r + 1) * N_KEYS, :]).astype(BF)
        c_scr[r * N_KEYS:(r + 1) * N_KEYS, :] = w.reshape(N_KEYS, tn) * act

    ht = ht_ref[...]
    a_scr[0:hrows, :] = _dot(u_ref[0:hrows, :], ht)
    srows = hrows // act_slices
    for j in range(act_slices):
        lo = hrows + j * srows
        a_scr[lo:lo + srows, :] = _dot(u_ref[lo:lo + srows, :], ht)
        for r in range(j * half // act_slices, (j + 1) * half // act_slices):
            coef_row(r)
    mrows = d // mix_slices
    c_a = c_scr[0:hrows, :]
    for j in range(mix_slices):
        o_ref[j * mrows:(j + 1) * mrows, :] += _dot(vt_ref[j * mrows:(j + 1) * mrows, 0:hrows], c_a)
        for r in range(j * half // mix_slices, (j + 1) * half // mix_slices):
            coef_row(half + r)
    o_ref[...] += _dot(vt_ref[:, hrows:], c_scr[hrows:, :])


def _peer_call(u, v_t, h_t, rank2, cnt, e1, e2, tn, te, act_slices, mix_slices):
    d, t = h_t.shape
    rows = te // N_KEYS
    groups = N_KEYS // BF16_ROWS
    rank4 = rank2.reshape(PEER_HEADS, groups, BF16_ROWS, t)
    e24 = e2.reshape(PEER_HEADS, groups, BF16_ROWS, t)
    sel3 = pl.BlockSpec((PEER_HEADS, rows, tn), lambda i, e: (0, e, i))
    sel4 = pl.BlockSpec((PEER_HEADS, groups, BF16_ROWS, tn), lambda i, e: (0, 0, 0, i))
    return pl.pallas_call(
        functools.partial(_peer_kernel, rows=rows, act_slices=act_slices,
                          mix_slices=mix_slices),
        grid=(t // tn, N_EXPERTS // te),
        in_specs=[pl.BlockSpec((te, d), lambda i, e: (e, 0)),
                  pl.BlockSpec((d, te), lambda i, e: (0, e)),
                  pl.BlockSpec((d, tn), lambda i, e: (0, i)),
                  sel4, sel3, sel3, sel4],
        out_specs=pl.BlockSpec((d, tn), lambda i, e: (0, i)),
        out_shape=jax.ShapeDtypeStruct((d, t), F32),
        scratch_shapes=[pltpu.VMEM((te, tn), F32), pltpu.VMEM((te, tn), BF)],
        compiler_params=_params(2),
        name="peer_dense",
    )(u, v_t, h_t, rank4, cnt, e1, e24)


def _final_kernel(yt_ref, x_ref, g2_ref, lg_ref, lb_ref, o_ref):
    z = DEEPNORM_ALPHA * x_ref[0] + g2_ref[0] * yt_ref[...].T
    o_ref[0] = _layer_norm_rows(z) * lg_ref[...] + lb_ref[...]


def _final_call(y_t, x1, g2, ln_g, ln_b, tm):
    b, s, d = x1.shape
    nb = s // tm
    return pl.pallas_call(
        _final_kernel,
        grid=(b, nb),
        in_specs=[pl.BlockSpec((d, tm), lambda i, j: (0, i * nb + j)),
                  pl.BlockSpec((1, tm, d), lambda i, j: (i, j, 0)),
                  pl.BlockSpec((1, 1, d), lambda i, j: (i, 0, 0)),
                  pl.BlockSpec((1, d), lambda i, j: (0, 0)),
                  pl.BlockSpec((1, d), lambda i, j: (0, 0))],
        out_specs=pl.BlockSpec((1, tm, d), lambda i, j: (i, j, 0)),
        out_shape=jax.ShapeDtypeStruct((b, s, d), F32),
        compiler_params=_params(2),
        name="final_deepnorm",
    )(y_t, x1, g2, ln_g, ln_b)


def _rope_rotation(w):
    pairs = w.reshape(w.shape[:-1] + (w.shape[-1] // 2, 2))
    return jnp.stack([-pairs[..., 1], pairs[..., 0]], axis=-1).reshape(w.shape)


def _pad_lanes(w):
    return jnp.pad(w, [(0, 0)] * (w.ndim - 1) + [(0, LANES - w.shape[-1])])


def _rope_tables(seq):
    rows = seq // GRID_W
    row = jnp.repeat(jnp.arange(rows, dtype=F32), GRID_W)
    col = jnp.tile(jnp.arange(GRID_W, dtype=F32), rows)
    half = QK_ROPE // 2
    inv = ROPE_THETA ** (-jnp.arange(0, half, 2, dtype=F32) / half)
    ang = jnp.concatenate([row[:, None] * inv, col[:, None] * inv], axis=-1)
    cos = _pad_lanes(jnp.repeat(jnp.cos(ang), 2, axis=-1))
    sin = _pad_lanes(jnp.repeat(jnp.sin(ang), 2, axis=-1))
    return cos, sin


def _dft_matrices(n, scale):
    k = np.arange(n, dtype=np.int64)
    ang = 2.0 * np.pi * ((k[:, None] * k[None, :]) % n).astype(np.float64) / n
    return np.cos(ang) * scale, np.sin(ang) * scale


def kernel(x, c, ctx, c_ctx, w_mod, b_mod, w_in, b_in, q_norm_g, w_uq, kv_norm_g, w_ukv,
           w_o_mla, w_fourier, w_out, ln1_g, ln1_b, peer_wq, peer_keys, peer_u, peer_v,
           ln2_g, ln2_b):
    B, S, D = x.shape
    T = B * S
    CT = ctx.shape[1]
    l = 0

    cmat = jnp.concatenate([c, c_ctx[None, :], jnp.zeros((8 - B - 1, D), F32)], axis=0)
    mod = _mod_call(cmat, w_mod[l], b_mod[l])
    mx = mod[:B].reshape(B, 1, 6, D)
    sh1, sc1, g1, sh2, sc2, g2 = [mx[:, :, i, :] for i in range(6)]
    mc = mod[B].reshape(1, 1, 6, D)
    sh1c, sc1c = mc[:, :, 0, :], mc[:, :, 1, :]

    wt, bi = w_in[l].T, b_in[l]
    w_kr, b_kr = wt[KV_LORA:KV_END].T, bi[KV_LORA:KV_END]
    q0 = KV_END
    f0 = KV_END + Q_LORA
    g0 = f0 + FOURIER_DIM
    w_all = jnp.concatenate(
        [wt[g0:], wt[f0:g0],
         wt[:KV_LORA], _pad_lanes(w_kr).T, _pad_lanes(_rope_rotation(w_kr)).T, wt[q0:f0]],
        axis=0).astype(BF)
    b_all = jnp.concatenate(
        [bi[g0:], bi[f0:g0],
         bi[:KV_LORA], _pad_lanes(b_kr), _pad_lanes(_rope_rotation(b_kr)), bi[q0:f0]])[None, :]
    wq3 = w_uq[l].reshape(Q_LORA, N_HEADS, QK_NOPE + QK_ROPE)
    wq_rope = wq3[:, :, QK_NOPE:]
    w_q = jnp.concatenate(
        [wq3[:, :, :QK_NOPE], wq_rope, _rope_rotation(wq_rope)],
        axis=-1).reshape(Q_LORA, N_HEADS * Q_HEAD_COLS).astype(BF)
    w_kv = w_ukv[l].astype(BF)
    gkv = kv_norm_g[l][None, :]
    gq = q_norm_g[l][None, :]

    cos, sin = _rope_tables(S)
    cos_c = _pad_lanes(jnp.ones((CT, QK_ROPE), F32))
    sin_c = jnp.zeros((CT, LANES), F32)
    dc_c, dc_s = _dft_matrices(FOURIER_GROUP_DIM, FOURIER_GROUP_DIM ** -0.5)
    dc = jnp.asarray(np.concatenate([dc_c, dc_s], axis=1), dtype=F32).astype(BF)
    ds_c, ds_s = _dft_matrices(S, S ** -0.5)
    cs = jnp.asarray(ds_c, dtype=F32).astype(BF)
    ss = jnp.asarray(ds_s, dtype=F32).astype(BF)

    hx = _ln_mod_call(x, sh1, sc1, 512)
    hc = _ln_mod_call(ctx, sh1c, sc1c, CT)
    ckv_x, kr_x, cq_x = _latent_call(hx, w_all, b_all, gkv, gq, cos, sin, 512)
    ckv_c, kr_c, _ = _latent_call(hc, w_all, b_all, gkv, gq, cos_c, sin_c, CT)
    lat = jnp.concatenate([ckv_c, ckv_x], axis=1)
    kr = jnp.concatenate([kr_c, kr_x], axis=1)
    k_all, v_all = _kv_up_call(lat, kr, w_kv, 768)
    q_all = _q_up_call(cq_x, w_q, cos, sin, 512)
    attn, u_b, v_t, (w_o_b, w_f_b, w_out_b, wq_b) = _attn_call(
        q_all, k_all, v_all, peer_u[l], peer_v[l],
        [w_o_mla[l], w_fourier[l], w_out[l], peer_wq[l]], 1024, 256)

    hx2d = hx.reshape(T, D)
    gc, gs = _fproj_call(hx2d, w_all, b_all, dc, 512)
    fm = _pos_dft_call(cs, ss, gc.reshape(B, S, FOURIER_DIM), gs.reshape(B, S, FOURIER_DIM),
                       1024, 512)
    gates = _gate_call(hx2d, w_all, b_all, 1024, 1024, 2)
    merged = _merge_call(attn.reshape(T, D), fm.reshape(T, FOURIER_DIM),
                         w_o_b, w_f_b, gates, 512, 1024, 2)
    x1, h2, h2_t = _outproj_call(merged, w_out_b, x, g1, ln1_g[l][None, :],
                                 ln1_b[l][None, :], sh2, sc2, 512)

    keys = peer_keys[l].reshape(N_HP, N_KEYS, PEER_HALF).astype(BF)
    rank2, cnt, e1, e2 = _select_call(wq_b, keys, h2, 512)
    y_t = _peer_call(u_b, v_t, h2_t, rank2, cnt, e1, e2, 512, 1024, 2, 4)
    return _final_call(y_t, x1, g2, ln2_g[l][None, :], ln2_b[l][None, :], 512)
```

```python
import functools
import math

import numpy as np
import jax
import jax.numpy as jnp
from jax import lax
from jax.experimental import pallas as pl
from jax.experimental.pallas import tpu as pltpu

D_MODEL = 2048
GRID_W = 64
N_HEADS = 16
QK_NOPE = 128
QK_ROPE = 64
V_DIM = 128
Q_LORA = 512
KV_LORA = 512
ROPE_THETA = 10000.0
N_FOURIER_GROUPS = 4
FOURIER_GROUP_DIM = 256
FOURIER_DIM = N_FOURIER_GROUPS * FOURIER_GROUP_DIM
KV_END = KV_LORA + QK_ROPE
PEER_HEADS = 8
N_KEYS = 128
N_EXPERTS = N_KEYS * N_KEYS
PEER_HALF = 128
PEER_TOPK = 16
DEPTH = 1
DEEPNORM_ALPHA = (2.0 * DEPTH) ** 0.25
EPS = 1e-6

LANES = 128
QK_PAD = 2 * LANES
VMEM_LIMIT = 56 * 1024 * 1024

BF = jnp.bfloat16
F32 = jnp.float32


def _params(n_axes, vmem=VMEM_LIMIT):
    return pltpu.CompilerParams(
        dimension_semantics=("arbitrary",) * n_axes, vmem_limit_bytes=vmem)


def _dot(a, b):
    return jnp.dot(a, b, preferred_element_type=F32)


def _dot_nt(a, b):
    return lax.dot_general(a, b, (((1,), (1,)), ((), ())), preferred_element_type=F32)


def _layer_norm_rows(x):
    mu = jnp.mean(x, axis=-1, keepdims=True)
    xc = x - mu
    var = jnp.mean(xc * xc, axis=-1, keepdims=True)
    return xc * lax.rsqrt(var + EPS)


def _mod_kernel(c_ref, w_ref, b_ref, o_ref):
    a = jax.nn.silu(c_ref[...]).astype(BF)
    o_ref[...] = _dot(a, w_ref[...].astype(BF)) + b_ref[...]


def _mod_call(cmat, w_mod, b_mod):
    n = w_mod.shape[1]
    tn = 1024
    return pl.pallas_call(
        _mod_kernel,
        grid=(n // tn,),
        in_specs=[pl.BlockSpec((8, D_MODEL), lambda j: (0, 0)),
                  pl.BlockSpec((D_MODEL, tn), lambda j: (0, j)),
                  pl.BlockSpec((1, tn), lambda j: (0, j))],
        out_specs=pl.BlockSpec((8, tn), lambda j: (0, j)),
        out_shape=jax.ShapeDtypeStruct((8, n), F32),
        compiler_params=_params(1),
        name="adaln_mod",
    )(cmat, w_mod, b_mod.reshape(1, n))


def _ln_mod_kernel(x_ref, sh_ref, sc_ref, o_ref):
    y = _layer_norm_rows(x_ref[0])
    o_ref[0] = (y * (1.0 + sc_ref[0]) + sh_ref[0]).astype(BF)


def _ln_mod_call(x, shift, scale, tm):
    b, s, d = x.shape
    bm = shift.shape[0]
    mod_map = (lambda i, j: (i, 0, 0)) if bm == b else (lambda i, j: (0, 0, 0))
    return pl.pallas_call(
        _ln_mod_kernel,
        grid=(b, s // tm),
        in_specs=[pl.BlockSpec((1, tm, d), lambda i, j: (i, j, 0)),
                  pl.BlockSpec((1, 1, d), mod_map),
                  pl.BlockSpec((1, 1, d), mod_map)],
        out_specs=pl.BlockSpec((1, tm, d), lambda i, j: (i, j, 0)),
        out_shape=jax.ShapeDtypeStruct((b, s, d), BF),
        compiler_params=_params(2),
        name="ln_modulate",
    )(x, shift, scale)


LAT_COLS = KV_LORA + 2 * LANES + Q_LORA
GATE_COLS = 2 * D_MODEL
FOURIER_BLOCK = GATE_COLS // FOURIER_DIM
LAT_BLOCK = (GATE_COLS + FOURIER_DIM) // LAT_COLS
assert GATE_COLS % FOURIER_DIM == 0 and (GATE_COLS + FOURIER_DIM) % LAT_COLS == 0


def _latent_kernel(h_ref, w_ref, b_ref, gkv_ref, gq_ref, cos_ref, sin_ref,
                   ckv_ref, kr_ref, cq_ref):
    acc = _dot_nt(h_ref[0], w_ref[...]) + b_ref[...]
    ckv = acc[:, :KV_LORA]
    ka = acc[:, KV_LORA:KV_LORA + LANES]
    kb = acc[:, KV_LORA + LANES:KV_LORA + 2 * LANES]
    cq = acc[:, KV_LORA + 2 * LANES:]
    ckv_n = ckv * lax.rsqrt(jnp.mean(ckv * ckv, axis=-1, keepdims=True) + EPS)
    cq_n = cq * lax.rsqrt(jnp.mean(cq * cq, axis=-1, keepdims=True) + EPS)
    ckv_ref[0] = (ckv_n * gkv_ref[...]).astype(BF)
    cq_ref[0] = (cq_n * gq_ref[...]).astype(BF)
    kr_ref[0] = (ka * cos_ref[...] + kb * sin_ref[...]).astype(BF)


def _latent_call(h, w_all, b_all, gkv, gq, cos, sin, tm):
    b, s, d = h.shape
    row = lambda i, j: (i, j, 0)
    const = lambda i, j: (0, 0)
    return pl.pallas_call(
        _latent_kernel,
        grid=(b, s // tm),
        in_specs=[pl.BlockSpec((1, tm, d), row),
                  pl.BlockSpec((LAT_COLS, d), lambda i, j: (LAT_BLOCK, 0)),
                  pl.BlockSpec((1, LAT_COLS), lambda i, j: (0, LAT_BLOCK)),
                  pl.BlockSpec((1, KV_LORA), const),
                  pl.BlockSpec((1, Q_LORA), const),
                  pl.BlockSpec((tm, LANES), lambda i, j: (j, 0)),
                  pl.BlockSpec((tm, LANES), lambda i, j: (j, 0))],
        out_specs=[pl.BlockSpec((1, tm, KV_LORA), row),
                   pl.BlockSpec((1, tm, LANES), row),
                   pl.BlockSpec((1, tm, Q_LORA), row)],
        out_shape=[jax.ShapeDtypeStruct((b, s, KV_LORA), BF),
                   jax.ShapeDtypeStruct((b, s, LANES), BF),
                   jax.ShapeDtypeStruct((b, s, Q_LORA), BF)],
        compiler_params=_params(2),
        name="latent_proj",
    )(h, w_all, b_all, gkv, gq, cos, sin)


def _fproj_kernel(h_ref, w_ref, b_ref, dc_ref, gc_ref, gs_ref):
    f = (_dot_nt(h_ref[...], w_ref[...]) + b_ref[...]).astype(BF)
    for g in range(N_FOURIER_GROUPS):
        lo = g * FOURIER_GROUP_DIM
        r = _dot(f[:, lo:lo + FOURIER_GROUP_DIM], dc_ref[...])
        gc_ref[:, lo:lo + FOURIER_GROUP_DIM] = r[:, :FOURIER_GROUP_DIM].astype(BF)
        gs_ref[:, lo:lo + FOURIER_GROUP_DIM] = r[:, FOURIER_GROUP_DIM:].astype(BF)


def _fproj_call(h2d, w_all, b_all, dc, tm):
    t, d = h2d.shape
    const = lambda i: (0, 0)
    return pl.pallas_call(
        _fproj_kernel,
        grid=(t // tm,),
        in_specs=[pl.BlockSpec((tm, d), lambda i: (i, 0)),
                  pl.BlockSpec((FOURIER_DIM, d), lambda i: (FOURIER_BLOCK, 0)),
                  pl.BlockSpec((1, FOURIER_DIM), lambda i: (0, FOURIER_BLOCK)),
                  pl.BlockSpec((FOURIER_GROUP_DIM, 2 * FOURIER_GROUP_DIM), const)],
        out_specs=[pl.BlockSpec((tm, FOURIER_DIM), lambda i: (i, 0)),
                   pl.BlockSpec((tm, FOURIER_DIM), lambda i: (i, 0))],
        out_shape=[jax.ShapeDtypeStruct((t, FOURIER_DIM), BF),
                   jax.ShapeDtypeStruct((t, FOURIER_DIM), BF)],
        compiler_params=_params(1),
        name="fourier_in_proj",
    )(h2d, w_all, b_all, dc)


def _gate_kernel(h_ref, w_ref, b_ref, o_ref, *, pieces):
    h = h_ref[...]
    pn = w_ref.shape[0] // pieces
    for p in range(pieces):
        cols = slice(p * pn, (p + 1) * pn)
        o_ref[:, cols] = jax.nn.sigmoid(_dot_nt(h, w_ref[cols, :]) + b_ref[:, cols]).astype(BF)


def _gate_call(h2d, w_all, b_all, tm, tn, pieces):
    t, d = h2d.shape
    n = GATE_COLS
    return pl.pallas_call(
        functools.partial(_gate_kernel, pieces=pieces),
        grid=(n // tn, t // tm),
        in_specs=[pl.BlockSpec((tm, d), lambda j, i: (i, 0)),
                  pl.BlockSpec((tn, d), lambda j, i: (j, 0)),
                  pl.BlockSpec((1, tn), lambda j, i: (0, j))],
        out_specs=pl.BlockSpec((tm, tn), lambda j, i: (i, j)),
        out_shape=jax.ShapeDtypeStruct((t, n), BF),
        compiler_params=_params(2),
        name="gate_proj",
    )(h2d, w_all, b_all)


def _kv_up_kernel(lat_ref, kr_ref, w_ref, k_ref, v_ref):
    lat = lat_ref[0]
    kr = kr_ref[0]
    for h in range(N_HEADS):
        lo = h * (QK_NOPE + V_DIM)
        kv = _dot(lat, w_ref[:, lo:lo + QK_NOPE + V_DIM])
        k_ref[0, h] = jnp.concatenate([kv[:, :QK_NOPE].astype(BF), kr], axis=-1)
        v_ref[0, h] = kv[:, QK_NOPE:].astype(BF)


def _kv_up_call(lat, kr, w_ukv, tm):
    b, t, _ = lat.shape
    return pl.pallas_call(
        _kv_up_kernel,
        grid=(b, t // tm),
        in_specs=[pl.BlockSpec((1, tm, KV_LORA), lambda i, j: (i, j, 0)),
                  pl.BlockSpec((1, tm, LANES), lambda i, j: (i, j, 0)),
                  pl.BlockSpec(w_ukv.shape, lambda i, j: (0, 0))],
        out_specs=[pl.BlockSpec((1, N_HEADS, tm, QK_PAD), lambda i, j: (i, 0, j, 0)),
                   pl.BlockSpec((1, N_HEADS, tm, V_DIM), lambda i, j: (i, 0, j, 0))],
        out_shape=[jax.ShapeDtypeStruct((b, N_HEADS, t, QK_PAD), BF),
                   jax.ShapeDtypeStruct((b, N_HEADS, t, V_DIM), BF)],
        compiler_params=_params(2),
        name="kv_up_proj",
    )(lat, kr, w_ukv)


Q_HEAD_COLS = 2 * LANES


def _q_up_kernel(cq_ref, w_ref, cos_ref, sin_ref, q_ref):
    cq = cq_ref[0]
    cos = cos_ref[...]
    sin = sin_ref[...]
    scale = (QK_NOPE + QK_ROPE) ** -0.5
    for h in range(N_HEADS):
        lo = h * Q_HEAD_COLS
        acc = _dot(cq, w_ref[:, lo:lo + Q_HEAD_COLS])
        qn = acc[:, :LANES]
        x = acc[:, LANES:]
        qr = x * cos + pltpu.roll(x, QK_ROPE, 1) * sin
        q_ref[0, h] = (jnp.concatenate([qn, qr], axis=-1) * scale).astype(BF)


def _q_up_call(cq, w_q, cos, sin, tm):
    b, s, _ = cq.shape
    return pl.pallas_call(
        _q_up_kernel,
        grid=(b, s // tm),
        in_specs=[pl.BlockSpec((1, tm, Q_LORA), lambda i, j: (i, j, 0)),
                  pl.BlockSpec(w_q.shape, lambda i, j: (0, 0)),
                  pl.BlockSpec((tm, LANES), lambda i, j: (j, 0)),
                  pl.BlockSpec((tm, LANES), lambda i, j: (j, 0))],
        out_specs=pl.BlockSpec((1, N_HEADS, tm, QK_PAD), lambda i, j: (i, 0, j, 0)),
        out_shape=jax.ShapeDtypeStruct((b, N_HEADS, s, QK_PAD), BF),
        compiler_params=_params(2),
        name="q_up_proj",
    )(cq, w_q, cos, sin)


ATTN_LAG = 1


def _attn_kernel(q_ref, k_ref, v_ref, eu_ref, ev_ref, *rest, kc, n_cast):
    cast_in = rest[:n_cast]
    o_ref, ub_ref, vt_ref = rest[n_cast:n_cast + 3]
    cast_out = rest[n_cast + 3:2 * n_cast + 3]
    s_scr, m_scr = rest[2 * n_cast + 3:]
    n = pl.program_id(0)
    tq = q_ref.shape[2]
    t = k_ref.shape[2]

    @pl.when(n == 0)
    def _():
        s_scr[...] = jnp.zeros_like(s_scr)
        m_scr[...] = jnp.zeros_like(m_scr)

    def step(cur, prev):
        q = q_ref[0, 0]
        m_prev = m_scr[prev]
        mrun = None
        lrun = jnp.zeros((tq, LANES), F32)
        acc = jnp.zeros((tq, V_DIM), F32)
        for c in range(t // kc):
            ks = slice(c * kc, (c + 1) * kc)
            s_c = lax.dot_general(q, k_ref[0, 0, ks, :], (((1,), (1,)), ((), ())),
                                  preferred_element_type=F32)
            s_scr[cur, :, ks] = s_c
            pieces = []
            for j in range(kc // LANES):
                lanes = slice(j * LANES, (j + 1) * LANES)
                col = slice(c * kc + j * LANES, c * kc + (j + 1) * LANES)
                p_j = jnp.exp(s_scr[prev, :, col] - m_prev)
                lrun = lrun + p_j
                pieces.append(p_j.astype(BF))
                mrun = s_c[:, lanes] if mrun is None else jnp.maximum(mrun, s_c[:, lanes])
            acc = acc + _dot(jnp.concatenate(pieces, axis=-1), v_ref[0, 0, ks, :])
        o_ref[0] = (acc / jnp.sum(lrun, axis=-1, keepdims=True)).astype(BF)
        m_scr[cur] = jnp.broadcast_to(jnp.max(mrun, axis=-1, keepdims=True), (tq, LANES))
        ub_ref[...] = eu_ref[...].astype(BF)
        vt_ref[...] = ev_ref[...].T.astype(BF)
        for src, dst in zip(cast_in, cast_out):
            dst[...] = src[...].astype(BF)

    @pl.when(n % 2 == 0)
    def _():
        step(0, 1)

    @pl.when(n % 2 == 1)
    def _():
        step(1, 0)


CAST_STEPS = 64


def _attn_call(q, k, v, eu, ev, weights, tq, kc):
    b, h, s, _ = q.shape
    t = k.shape[2]
    nq = s // tq
    total = b * h * nq
    ne, ed = eu.shape
    slab = ne // total
    assert slab * total == ne and slab % LANES == 0 and total >= CAST_STEPS
    slab_idx = lambda n: jnp.minimum(n, total - 1)
    cast_idx = lambda n: (jnp.minimum(n, CAST_STEPS - 1), 0)
    cast_specs = []
    for w in weights:
        assert w.shape[0] % (CAST_STEPS * BF16_ROWS) == 0
        cast_specs.append(pl.BlockSpec((w.shape[0] // CAST_STEPS, w.shape[1]), cast_idx))

    def block(n, lag):
        i = jnp.clip(n - lag, 0, total - 1)
        return i // (h * nq), (i // nq) % h, i % nq

    def q_map(n):
        bi, hi, qi = block(n, 0)
        return bi, hi, qi, 0

    def k_map(n):
        bi, hi, _ = block(n, 0)
        return bi, hi, 0, 0

    def v_map(n):
        bi, hi, _ = block(n, ATTN_LAG)
        return bi, hi, 0, 0

    def o_map(n):
        bi, hi, qi = block(n, ATTN_LAG)
        return bi, qi, hi

    outs = pl.pallas_call(
        functools.partial(_attn_kernel, kc=kc, n_cast=len(weights)),
        grid=(total + ATTN_LAG,),
        in_specs=[pl.BlockSpec((1, 1, tq, QK_PAD), q_map),
                  pl.BlockSpec((1, 1, t, QK_PAD), k_map),
                  pl.BlockSpec((1, 1, t, V_DIM), v_map),
                  pl.BlockSpec((slab, ed), lambda n: (slab_idx(n), 0)),
                  pl.BlockSpec((slab, ed), lambda n: (slab_idx(n), 0))] + cast_specs,
        out_specs=[pl.BlockSpec((1, tq, V_DIM), o_map),
                   pl.BlockSpec((slab, ed), lambda n: (slab_idx(n), 0)),
                   pl.BlockSpec((ed, slab), lambda n: (0, slab_idx(n)))] + cast_specs,
        out_shape=[jax.ShapeDtypeStruct((b, s, h * V_DIM), BF),
                   jax.ShapeDtypeStruct((ne, ed), BF),
                   jax.ShapeDtypeStruct((ed, ne), BF)]
        + [jax.ShapeDtypeStruct(w.shape, BF) for w in weights],
        scratch_shapes=[pltpu.VMEM((2, tq, t), F32), pltpu.VMEM((2, tq, LANES), F32)],
        compiler_params=_params(1),
        name="mla_attention",
    )(q, k, v, eu, ev, *weights)
    return outs[0], outs[1], outs[2], outs[3:]


def _pos_dft_kernel(c_ref, s_ref, gc_ref, gs_ref, o_ref):
    o_ref[0] = (_dot(c_ref[...], gc_ref[0]) - _dot(s_ref[...], gs_ref[0])).astype(BF)


def _pos_dft_call(cs, ss, gc, gs, tm, tn):
    b, s, n = gc.shape
    return pl.pallas_call(
        _pos_dft_kernel,
        grid=(b, n // tn, s // tm),
        in_specs=[pl.BlockSpec((tm, s), lambda i, j, m: (m, 0)),
                  pl.BlockSpec((tm, s), lambda i, j, m: (m, 0)),
                  pl.BlockSpec((1, s, tn), lambda i, j, m: (i, 0, j)),
                  pl.BlockSpec((1, s, tn), lambda i, j, m: (i, 0, j))],
        out_specs=pl.BlockSpec((1, tm, tn), lambda i, j, m: (i, m, j)),
        out_shape=jax.ShapeDtypeStruct((b, s, n), BF),
        compiler_params=_params(3),
        name="position_dft",
    )(cs, ss, gc, gs)


def _merge_kernel(a_ref, f_ref, wo_ref, wf_ref, ga_ref, gb_ref, o_ref, *, pieces):
    a = a_ref[...]
    f = f_ref[...]
    pn = wo_ref.shape[1] // pieces
    for p in range(pieces):
        cols = slice(p * pn, (p + 1) * pn)
        ya = _dot(a, wo_ref[:, cols])
        yb = _dot(f, wf_ref[:, cols])
        o_ref[:, cols] = (ga_ref[:, cols].astype(F32) * ya
                          + gb_ref[:, cols].astype(F32) * yb).astype(BF)


def _merge_call(attn, fm, w_o, w_f, gates, tm, tn, pieces):
    t, d = attn.shape
    nb = D_MODEL // tn
    return pl.pallas_call(
        functools.partial(_merge_kernel, pieces=pieces),
        grid=(nb, t // tm),
        in_specs=[pl.BlockSpec((tm, d), lambda j, i: (i, 0)),
                  pl.BlockSpec((tm, FOURIER_DIM), lambda j, i: (i, 0)),
                  pl.BlockSpec((d, tn), lambda j, i: (0, j)),
                  pl.BlockSpec((FOURIER_DIM, tn), lambda j, i: (0, j)),
                  pl.BlockSpec((tm, tn), lambda j, i: (i, j)),
                  pl.BlockSpec((tm, tn), lambda j, i: (i, j + nb))],
        out_specs=pl.BlockSpec((tm, tn), lambda j, i: (i, j)),
        out_shape=jax.ShapeDtypeStruct((t, D_MODEL), BF),
        compiler_params=_params(2),
        name="branch_merge",
    )(attn, fm, w_o, w_f, gates, gates)


def _outproj_kernel(m_ref, w_ref, x_ref, g1_ref, lg_ref, lb_ref, sh_ref, sc_ref,
                    x1_ref, h_ref, ht_ref):
    y = _dot(m_ref[...], w_ref[...])
    z = DEEPNORM_ALPHA * x_ref[0] + g1_ref[0] * y
    x1 = _layer_norm_rows(z) * lg_ref[...] + lb_ref[...]
    x1_ref[0] = x1
    h2 = _layer_norm_rows(x1) * (1.0 + sc_ref[0]) + sh_ref[0]
    h_ref[...] = h2.astype(BF)
    ht_ref[...] = h2.T.astype(BF)


def _outproj_call(merged, w_out, x, g1, ln_g, ln_b, sh2, sc2, tm):
    b, s, d = x.shape
    nb = s // tm
    bmap = lambda i, j: (i, 0, 0)
    const = lambda i, j: (0, 0)
    return pl.pallas_call(
        _outproj_kernel,
        grid=(b, nb),
        in_specs=[pl.BlockSpec((tm, d), lambda i, j: (i * nb + j, 0)),
                  pl.BlockSpec((d, d), const),
                  pl.BlockSpec((1, tm, d), lambda i, j: (i, j, 0)),
                  pl.BlockSpec((1, 1, d), bmap),
                  pl.BlockSpec((1, d), const),
                  pl.BlockSpec((1, d), const),
                  pl.BlockSpec((1, 1, d), bmap),
                  pl.BlockSpec((1, 1, d), bmap)],
        out_specs=[pl.BlockSpec((1, tm, d), lambda i, j: (i, j, 0)),
                   pl.BlockSpec((tm, d), lambda i, j: (i * nb + j, 0)),
                   pl.BlockSpec((d, tm), lambda i, j: (0, i * nb + j))],
        out_shape=[jax.ShapeDtypeStruct((b, s, d), F32),
                   jax.ShapeDtypeStruct((b * s, d), BF),
                   jax.ShapeDtypeStruct((d, b * s), BF)],
        compiler_params=_params(2),
        name="out_proj_deepnorm",
    )(merged, w_out, x, g1, ln_g, ln_b, sh2, sc2)


N_HP = 2 * PEER_HEADS
HALF_K = PEER_TOPK // 2
BF16_ROWS = 16


def _top_values(s, k):
    tops = []
    for _ in range(k):
        m = jnp.max(s, axis=0, keepdims=True)
        tops.append(m)
        s = jnp.where(s == m, -jnp.inf, s)
    return tops


def _sort_network(n):
    pairs = []
    p = 1
    while p < n:
        k = p
        while k >= 1:
            for j in range(k % p, n - k, 2 * k):
                for i in range(min(k, n - j - k)):
                    if (i + j) // (2 * p) == (i + j + k) // (2 * p):
                        pairs.append((i + j, i + j + k))
            k //= 2
        p *= 2
    return pairs


SUBLANES = 8


def _top_values_sorted(s, k):
    n = s.shape[0] // SUBLANES
    v = [s[j * SUBLANES:(j + 1) * SUBLANES, :] for j in range(n)]
    for i, j in _sort_network(n):
        v[i], v[j] = jnp.maximum(v[i], v[j]), jnp.minimum(v[i], v[j])
    tops = []
    for r in range(k):
        m = jnp.max(v[0], axis=0, keepdims=True)
        tops.append(m)
        depth = k - 1 - r
        if depth == 0:
            break
        hit = v[0] == m
        for j in range(min(depth, n - 1)):
            v[j] = jnp.where(hit, v[j + 1], v[j])
        if depth > n - 1:
            v[n - 1] = jnp.where(hit, -jnp.inf, v[n - 1])
    return tops


def _rank_among(s, tops):
    rank = jnp.full(s.shape, float(len(tops)), F32)
    for r in range(len(tops) - 1, -1, -1):
        rank = jnp.where(s >= tops[r], float(r), rank)
    return rank


def _select_kernel(wq_ref, keys_ref, h_ref, rank_ref, cnt_ref, e1_ref, e2_ref,
                   s_scr, top_scr):
    q = _dot(h_ref[...], wq_ref[...]).astype(BF)
    for hp in range(N_HP):
        s_scr[hp] = lax.dot_general(keys_ref[hp], q[:, hp * PEER_HALF:(hp + 1) * PEER_HALF],
                                    (((1,), (1,)), ((), ())), preferred_element_type=F32)

    def head_body(h, carry):
        s1 = s_scr[2 * h]
        s2 = s_scr[2 * h + 1]
        tops1 = _top_values_sorted(s1, PEER_TOPK + 1)
        tops2 = _top_values_sorted(s2, PEER_TOPK + 1)
        rank2 = _rank_among(s2, tops2)
        for r in range(PEER_TOPK):
            top_scr[0, r:r + 1, :] = tops1[r]
            top_scr[1, r:r + 1, :] = tops2[r]
        t1 = top_scr[0]
        t2 = top_scr[1]
        m1, m2 = tops1[0], tops2[0]
        cand = jnp.concatenate(
            [m1 + t2]
            + [tops1[a] + t2[:HALF_K] for a in range(1, HALF_K)]
            + [t1[HALF_K:] + m2], axis=0)
        best = _top_values_sorted(cand, PEER_TOPK + 1)
        outside = jnp.maximum(tops1[PEER_TOPK] + m2, m1 + tops2[PEER_TOPK])
        runner_up = jnp.maximum(best[PEER_TOPK], outside)
        tau = 0.5 * (best[PEER_TOPK - 1] + runner_up)
        sel = cand >= tau
        z = jnp.sum(jnp.where(sel, jnp.exp(cand - (m1 + m2)), 0.0), axis=0, keepdims=True)
        self32 = sel.astype(F32)
        counts = [jnp.sum(self32[:PEER_TOPK], axis=0, keepdims=True)]
        for a in range(1, HALF_K):
            lo = PEER_TOPK + (a - 1) * HALF_K
            counts.append(jnp.sum(self32[lo:lo + HALF_K], axis=0, keepdims=True))
        lo = PEER_TOPK + (HALF_K - 1) * HALF_K
        for a in range(HALF_K, PEER_TOPK):
            counts.append(self32[lo + a - HALF_K:lo + a - HALF_K + 1])
        cnt = jnp.zeros_like(s1)
        for a in range(PEER_TOPK):
            cnt = jnp.where(s1 == tops1[a], counts[a], cnt)
        rank_ref[h] = rank2.astype(BF)
        cnt_ref[h] = cnt
        e1_ref[h] = jnp.exp(s1 - m1) * (0.5 / z)
        e2_ref[h] = jnp.exp(s2 - m2).astype(BF)
        return carry
    lax.fori_loop(0, PEER_HEADS, head_body, 0)


def _select_call(wq, keys, h, tn):
    t, d = h.shape
    shape = (PEER_HEADS, N_KEYS, t)
    ospec = pl.BlockSpec((PEER_HEADS, N_KEYS, tn), lambda i: (0, 0, i))
    return pl.pallas_call(
        _select_kernel,
        grid=(t // tn,),
        in_specs=[pl.BlockSpec(wq.shape, lambda i: (0, 0)),
                  pl.BlockSpec(keys.shape, lambda i: (0, 0, 0)),
                  pl.BlockSpec((tn, d), lambda i: (i, 0))],
        out_specs=[ospec, ospec, ospec, ospec],
        out_shape=[jax.ShapeDtypeStruct(shape, BF), jax.ShapeDtypeStruct(shape, F32),
                   jax.ShapeDtypeStruct(shape, F32), jax.ShapeDtypeStruct(shape, BF)],
        scratch_shapes=[pltpu.VMEM((N_HP, N_KEYS, tn), F32),
                        pltpu.VMEM((2, PEER_TOPK, tn), F32)],
        compiler_params=_params(1),
        name="peer_select",
    )(wq, keys, h)


def _gelu_times_two(x):
    return x * (1.0 + lax.erf(x * math.sqrt(0.5)))


def _peer_kernel(u_ref, vt_ref, ht_ref, rank_ref, cnt_ref, e1_ref, e2_ref, o_ref,
                 a_scr, c_scr, *, rows, act_slices, mix_slices):
    e = pl.program_id(1)
    tn = ht_ref.shape[1]
    d = vt_ref.shape[0]
    half = rows // 2
    hrows = half * N_KEYS

    @pl.when(e == 0)
    def _():
        o_ref[...] = jnp.zeros_like(o_ref)

    def coef_row(r):
        w = None
        for h in range(PEER_HEADS):
            cnt = jnp.broadcast_to(cnt_ref[h, r:r + 1, :], (BF16_ROWS, tn)).astype(BF)
            e1 = jnp.broadcast_to(e1_ref[h, r:r + 1, :], (BF16_ROWS, tn)).astype(BF)
            gate = jnp.where(rank_ref[h] < cnt[None], e1[None], jnp.zeros((), BF))
            w = e2_ref[h] * gate if w is None else w + e2_ref[h] * gate
        act = _gelu_times_two(a_scr[r * N_KEYS:(r + 1) * N_KEYS, :]).astype(BF)
        c_scr[r * N_KEYS:(r + 1) * N_KEYS, :] = w.reshape(N_KEYS, tn) * act

    ht = ht_ref[...]
    a_scr[0:hrows, :] = _dot(u_ref[0:hrows, :], ht)
    srows = hrows // act_slices
    for j in range(act_slices):
        lo = hrows + j * srows
        a_scr[lo:lo + srows, :] = _dot(u_ref[lo:lo + srows, :], ht)
        for r in range(j * half // act_slices, (j + 1) * half // act_slices):
            coef_row(r)
    mrows = d // mix_slices
    c_a = c_scr[0:hrows, :]
    for j in range(mix_slices):
        o_ref[j * mrows:(j + 1) * mrows, :] += _dot(vt_ref[j * mrows:(j + 1) * mrows, 0:hrows], c_a)
        for r in range(j * half // mix_slices, (j + 1) * half // mix_slices):
            coef_row(half + r)
    o_ref[...] += _dot(vt_ref[:, hrows:], c_scr[hrows:, :])


def _peer_call(u, v_t, h_t, rank2, cnt, e1, e2, tn, te, act_slices, mix_slices):
    d, t = h_t.shape
    rows = te // N_KEYS
    groups = N_KEYS // BF16_ROWS
    rank4 = rank2.reshape(PEER_HEADS, groups, BF16_ROWS, t)
    e24 = e2.reshape(PEER_HEADS, groups, BF16_ROWS, t)
    sel3 = pl.BlockSpec((PEER_HEADS, rows, tn), lambda i, e: (0, e, i))
    sel4 = pl.BlockSpec((PEER_HEADS, groups, BF16_ROWS, tn), lambda i, e: (0, 0, 0, i))
    return pl.pallas_call(
        functools.partial(_peer_kernel, rows=rows, act_slices=act_slices,
                          mix_slices=mix_slices),
        grid=(t // tn, N_EXPERTS // te),
        in_specs=[pl.BlockSpec((te, d), lambda i, e: (e, 0)),
                  pl.BlockSpec((d, te), lambda i, e: (0, e)),
                  pl.BlockSpec((d, tn), lambda i, e: (0, i)),
                  sel4, sel3, sel3, sel4],
        out_specs=pl.BlockSpec((d, tn), lambda i, e: (0, i)),
        out_shape=jax.ShapeDtypeStruct((d, t), F32),
        scratch_shapes=[pltpu.VMEM((te, tn), F32), pltpu.VMEM((te, tn), BF)],
        compiler_params=_params(2),
        name="peer_dense",
    )(u, v_t, h_t, rank4, cnt, e1, e24)


def _final_kernel(yt_ref, x_ref, g2_ref, lg_ref, lb_ref, o_ref):
    z = DEEPNORM_ALPHA * x_ref[0] + g2_ref[0] * yt_ref[...].T
    o_ref[0] = _layer_norm_rows(z) * lg_ref[...] + lb_ref[...]


def _final_call(y_t, x1, g2, ln_g, ln_b, tm):
    b, s, d = x1.shape
    nb = s // tm
    return pl.pallas_call(
        _final_kernel,
        grid=(b, nb),
        in_specs=[pl.BlockSpec((d, tm), lambda i, j: (0, i * nb + j)),
                  pl.BlockSpec((1, tm, d), lambda i, j: (i, j, 0)),
                  pl.BlockSpec((1, 1, d), lambda i, j: (i, 0, 0)),
                  pl.BlockSpec((1, d), lambda i, j: (0, 0)),
                  pl.BlockSpec((1, d), lambda i, j: (0, 0))],
        out_specs=pl.BlockSpec((1, tm, d), lambda i, j: (i, j, 0)),
        out_shape=jax.ShapeDtypeStruct((b, s, d), F32),
        compiler_params=_params(2),
        name="final_deepnorm",
    )(y_t, x1, g2, ln_g, ln_b)


def _rope_rotation(w):
    pairs = w.reshape(w.shape[:-1] + (w.shape[-1] // 2, 2))
    return jnp.stack([-pairs[..., 1], pairs[..., 0]], axis=-1).reshape(w.shape)


def _pad_lanes(w):
    return jnp.pad(w, [(0, 0)] * (w.ndim - 1) + [(0, LANES - w.shape[-1])])


def _rope_tables(seq):
    rows = seq // GRID_W
    row = jnp.repeat(jnp.arange(rows, dtype=F32), GRID_W)
    col = jnp.tile(jnp.arange(GRID_W, dtype=F32), rows)
    half = QK_ROPE // 2
    inv = ROPE_THETA ** (-jnp.arange(0, half, 2, dtype=F32) / half)
    ang = jnp.concatenate([row[:, None] * inv, col[:, None] * inv], axis=-1)
    cos = _pad_lanes(jnp.repeat(jnp.cos(ang), 2, axis=-1))
    sin = _pad_lanes(jnp.repeat(jnp.sin(ang), 2, axis=-1))
    return cos, sin


def _dft_matrices(n, scale):
    k = np.arange(n, dtype=np.int64)
    ang = 2.0 * np.pi * ((k[:, None] * k[None, :]) % n).astype(np.float64) / n
    return np.cos(ang) * scale, np.sin(ang) * scale


def kernel(x, c, ctx, c_ctx, w_mod, b_mod, w_in, b_in, q_norm_g, w_uq, kv_norm_g, w_ukv,
           w_o_mla, w_fourier, w_out, ln1_g, ln1_b, peer_wq, peer_keys, peer_u, peer_v,
           ln2_g, ln2_b):
    B, S, D = x.shape
    T = B * S
    CT = ctx.shape[1]
    l = 0

    cmat = jnp.concatenate([c, c_ctx[None, :], jnp.zeros((8 - B - 1, D), F32)], axis=0)
    mod = _mod_call(cmat, w_mod[l], b_mod[l])
    mx = mod[:B].reshape(B, 1, 6, D)
    sh1, sc1, g1, sh2, sc2, g2 = [mx[:, :, i, :] for i in range(6)]
    mc = mod[B].reshape(1, 1, 6, D)
    sh1c, sc1c = mc[:, :, 0, :], mc[:, :, 1, :]

    wt, bi = w_in[l].T, b_in[l]
    w_kr, b_kr = wt[KV_LORA:KV_END].T, bi[KV_LORA:KV_END]
    q0 = KV_END
    f0 = KV_END + Q_LORA
    g0 = f0 + FOURIER_DIM
    w_all = jnp.concatenate(
        [wt[g0:], wt[f0:g0],
         wt[:KV_LORA], _pad_lanes(w_kr).T, _pad_lanes(_rope_rotation(w_kr)).T, wt[q0:f0]],
        axis=0).astype(BF)
    b_all = jnp.concatenate(
        [bi[g0:], bi[f0:g0],
         bi[:KV_LORA], _pad_lanes(b_kr), _pad_lanes(_rope_rotation(b_kr)), bi[q0:f0]])[None, :]
    wq3 = w_uq[l].reshape(Q_LORA, N_HEADS, QK_NOPE + QK_ROPE)
    wq_rope = wq3[:, :, QK_NOPE:]
    w_q = jnp.concatenate(
        [wq3[:, :, :QK_NOPE], wq_rope, _rope_rotation(wq_rope)],
        axis=-1).reshape(Q_LORA, N_HEADS * Q_HEAD_COLS).astype(BF)
    w_kv = w_ukv[l].astype(BF)
    gkv = kv_norm_g[l][None, :]
    gq = q_norm_g[l][None, :]

    cos, sin = _rope_tables(S)
    cos_c = _pad_lanes(jnp.ones((CT, QK_ROPE), F32))
    sin_c = jnp.zeros((CT, LANES), F32)
    dc_c, dc_s = _dft_matrices(FOURIER_GROUP_DIM, FOURIER_GROUP_DIM ** -0.5)
    dc = jnp.asarray(np.concatenate([dc_c, dc_s], axis=1), dtype=F32).astype(BF)
    ds_c, ds_s = _dft_matrices(S, S ** -0.5)
    cs = jnp.asarray(ds_c, dtype=F32).astype(BF)
    ss = jnp.asarray(ds_s, dtype=F32).astype(BF)

    hx = _ln_mod_call(x, sh1, sc1, 512)
    hc = _ln_mod_call(ctx, sh1c, sc1c, CT)
    ckv_x, kr_x, cq_x = _latent_call(hx, w_all, b_all, gkv, gq, cos, sin, 1024)
    ckv_c, kr_c, _ = _latent_call(hc, w_all, b_all, gkv, gq, cos_c, sin_c, CT)
    lat = jnp.concatenate([ckv_c, ckv_x], axis=1)
    kr = jnp.concatenate([kr_c, kr_x], axis=1)
    k_all, v_all = _kv_up_call(lat, kr, w_kv, 768)
    q_all = _q_up_call(cq_x, w_q, cos, sin, 512)
    attn, u_b, v_t, (w_o_b, w_f_b, w_out_b, wq_b) = _attn_call(
        q_all, k_all, v_all, peer_u[l], peer_v[l],
        [w_o_mla[l], w_fourier[l], w_out[l], peer_wq[l]], 1024, 256)

    hx2d = hx.reshape(T, D)
    gc, gs = _fproj_call(hx2d, w_all, b_all, dc, 1024)
    fm = _pos_dft_call(cs, ss, gc.reshape(B, S, FOURIER_DIM), gs.reshape(B, S, FOURIER_DIM),
                       1024, 512)
    gates = _gate_call(hx2d, w_all, b_all, 1024, 2048, 4)
    merged = _merge_call(attn.reshape(T, D), fm.reshape(T, FOURIER_DIM),
                         w_o_b, w_f_b, gates, 512, 1024, 2)
    x1, h2, h2_t = _outproj_call(merged, w_out_b, x, g1, ln1_g[l][None, :],
                                 ln1_b[l][None, :], sh2, sc2, 512)

    keys = peer_keys[l].reshape(N_HP, N_KEYS, PEER_HALF).astype(BF)
    rank2, cnt, e1, e2 = _select_call(wq_b, keys, h2, 512)
    y_t = _peer_call(u_b, v_t, h2_t, rank2, cnt, e1, e2, 512, 1024, 2, 4)
    return _final_call(y_t, x1, g2, ln2_g[l][None, :], ln2_b[l][None, :], 512)
```

```python
import functools
import math

import numpy as np
import jax
import jax.numpy as jnp
from jax import lax
from jax.experimental import pallas as pl
from jax.experimental.pallas import tpu as pltpu

D_MODEL = 2048
GRID_W = 64
N_HEADS = 16
QK_NOPE = 128
QK_ROPE = 64
V_DIM = 128
Q_LORA = 512
KV_LORA = 512
ROPE_THETA = 10000.0
N_FOURIER_GROUPS = 4
FOURIER_GROUP_DIM = 256
FOURIER_DIM = N_FOURIER_GROUPS * FOURIER_GROUP_DIM
KV_END = KV_LORA + QK_ROPE
PEER_HEADS = 8
N_KEYS = 128
N_EXPERTS = N_KEYS * N_KEYS
PEER_HALF = 128
PEER_TOPK = 16
DEPTH = 1
DEEPNORM_ALPHA = (2.0 * DEPTH) ** 0.25
EPS = 1e-6

LANES = 128
QK_PAD = 2 * LANES
VMEM_LIMIT = 56 * 1024 * 1024

BF = jnp.bfloat16
F32 = jnp.float32


def _params(n_axes, vmem=VMEM_LIMIT):
    return pltpu.CompilerParams(
        dimension_semantics=("arbitrary",) * n_axes, vmem_limit_bytes=vmem)


def _dot(a, b):
    return jnp.dot(a, b, preferred_element_type=F32)


def _dot_nt(a, b):
    return lax.dot_general(a, b, (((1,), (1,)), ((), ())), preferred_element_type=F32)


def _layer_norm_rows(x):
    mu = jnp.mean(x, axis=-1, keepdims=True)
    xc = x - mu
    var = jnp.mean(xc * xc, axis=-1, keepdims=True)
    return xc * lax.rsqrt(var + EPS)


def _mod_kernel(c_ref, w_ref, b_ref, o_ref):
    a = jax.nn.silu(c_ref[...]).astype(BF)
    o_ref[...] = _dot(a, w_ref[...].astype(BF)) + b_ref[...]


def _mod_call(cmat, w_mod, b_mod):
    n = w_mod.shape[1]
    tn = 1024
    return pl.pallas_call(
        _mod_kernel,
        grid=(n // tn,),
        in_specs=[pl.BlockSpec((8, D_MODEL), lambda j: (0, 0)),
                  pl.BlockSpec((D_MODEL, tn), lambda j: (0, j)),
                  pl.BlockSpec((1, tn), lambda j: (0, j))],
        out_specs=pl.BlockSpec((8, tn), lambda j: (0, j)),
        out_shape=jax.ShapeDtypeStruct((8, n), F32),
        compiler_params=_params(1),
        name="adaln_mod",
    )(cmat, w_mod, b_mod.reshape(1, n))


def _ln_mod_kernel(x_ref, sh_ref, sc_ref, o_ref):
    y = _layer_norm_rows(x_ref[0])
    o_ref[0] = (y * (1.0 + sc_ref[0]) + sh_ref[0]).astype(BF)


def _ln_mod_call(x, shift, scale, tm):
    b, s, d = x.shape
    bm = shift.shape[0]
    mod_map = (lambda i, j: (i, 0, 0)) if bm == b else (lambda i, j: (0, 0, 0))
    return pl.pallas_call(
        _ln_mod_kernel,
        grid=(b, s // tm),
        in_specs=[pl.BlockSpec((1, tm, d), lambda i, j: (i, j, 0)),
                  pl.BlockSpec((1, 1, d), mod_map),
                  pl.BlockSpec((1, 1, d), mod_map)],
        out_specs=pl.BlockSpec((1, tm, d), lambda i, j: (i, j, 0)),
        out_shape=jax.ShapeDtypeStruct((b, s, d), BF),
        compiler_params=_params(2),
        name="ln_modulate",
    )(x, shift, scale)


LAT_COLS = KV_LORA + 2 * LANES + Q_LORA
GATE_COLS = 2 * D_MODEL
FOURIER_BLOCK = GATE_COLS // FOURIER_DIM
LAT_BLOCK = (GATE_COLS + FOURIER_DIM) // LAT_COLS
assert GATE_COLS % FOURIER_DIM == 0 and (GATE_COLS + FOURIER_DIM) % LAT_COLS == 0


def _latent_kernel(h_ref, w_ref, b_ref, gkv_ref, gq_ref, cos_ref, sin_ref,
                   ckv_ref, kr_ref, cq_ref):
    acc = _dot_nt(h_ref[0], w_ref[...]) + b_ref[...]
    ckv = acc[:, :KV_LORA]
    ka = acc[:, KV_LORA:KV_LORA + LANES]
    kb = acc[:, KV_LORA + LANES:KV_LORA + 2 * LANES]
    cq = acc[:, KV_LORA + 2 * LANES:]
    ckv_n = ckv * lax.rsqrt(jnp.mean(ckv * ckv, axis=-1, keepdims=True) + EPS)
    cq_n = cq * lax.rsqrt(jnp.mean(cq * cq, axis=-1, keepdims=True) + EPS)
    ckv_ref[0] = (ckv_n * gkv_ref[...]).astype(BF)
    cq_ref[0] = (cq_n * gq_ref[...]).astype(BF)
    kr_ref[0] = (ka * cos_ref[...] + kb * sin_ref[...]).astype(BF)


def _latent_call(h, w_all, b_all, gkv, gq, cos, sin, tm):
    b, s, d = h.shape
    row = lambda i, j: (i, j, 0)
    const = lambda i, j: (0, 0)
    return pl.pallas_call(
        _latent_kernel,
        grid=(b, s // tm),
        in_specs=[pl.BlockSpec((1, tm, d), row),
                  pl.BlockSpec((LAT_COLS, d), lambda i, j: (LAT_BLOCK, 0)),
                  pl.BlockSpec((1, LAT_COLS), lambda i, j: (0, LAT_BLOCK)),
                  pl.BlockSpec((1, KV_LORA), const),
                  pl.BlockSpec((1, Q_LORA), const),
                  pl.BlockSpec((tm, LANES), lambda i, j: (j, 0)),
                  pl.BlockSpec((tm, LANES), lambda i, j: (j, 0))],
        out_specs=[pl.BlockSpec((1, tm, KV_LORA), row),
                   pl.BlockSpec((1, tm, LANES), row),
                   pl.BlockSpec((1, tm, Q_LORA), row)],
        out_shape=[jax.ShapeDtypeStruct((b, s, KV_LORA), BF),
                   jax.ShapeDtypeStruct((b, s, LANES), BF),
                   jax.ShapeDtypeStruct((b, s, Q_LORA), BF)],
        compiler_params=_params(2),
        name="latent_proj",
    )(h, w_all, b_all, gkv, gq, cos, sin)


def _fproj_kernel(h_ref, w_ref, b_ref, dc_ref, gc_ref, gs_ref):
    f = (_dot_nt(h_ref[...], w_ref[...]) + b_ref[...]).astype(BF)
    for g in range(N_FOURIER_GROUPS):
        lo = g * FOURIER_GROUP_DIM
        r = _dot(f[:, lo:lo + FOURIER_GROUP_DIM], dc_ref[...])
        gc_ref[:, lo:lo + FOURIER_GROUP_DIM] = r[:, :FOURIER_GROUP_DIM].astype(BF)
        gs_ref[:, lo:lo + FOURIER_GROUP_DIM] = r[:, FOURIER_GROUP_DIM:].astype(BF)


def _fproj_call(h2d, w_all, b_all, dc, tm):
    t, d = h2d.shape
    const = lambda i: (0, 0)
    return pl.pallas_call(
        _fproj_kernel,
        grid=(t // tm,),
        in_specs=[pl.BlockSpec((tm, d), lambda i: (i, 0)),
                  pl.BlockSpec((FOURIER_DIM, d), lambda i: (FOURIER_BLOCK, 0)),
                  pl.BlockSpec((1, FOURIER_DIM), lambda i: (0, FOURIER_BLOCK)),
                  pl.BlockSpec((FOURIER_GROUP_DIM, 2 * FOURIER_GROUP_DIM), const)],
        out_specs=[pl.BlockSpec((tm, FOURIER_DIM), lambda i: (i, 0)),
                   pl.BlockSpec((tm, FOURIER_DIM), lambda i: (i, 0))],
        out_shape=[jax.ShapeDtypeStruct((t, FOURIER_DIM), BF),
                   jax.ShapeDtypeStruct((t, FOURIER_DIM), BF)],
        compiler_params=_params(1),
        name="fourier_in_proj",
    )(h2d, w_all, b_all, dc)


def _gate_kernel(h_ref, w_ref, b_ref, o_ref, *, pieces):
    h = h_ref[...]
    pn = w_ref.shape[0] // pieces
    for p in range(pieces):
        cols = slice(p * pn, (p + 1) * pn)
        o_ref[:, cols] = jax.nn.sigmoid(_dot_nt(h, w_ref[cols, :]) + b_ref[:, cols]).astype(BF)


def _gate_call(h2d, w_all, b_all, tm, tn, pieces):
    t, d = h2d.shape
    n = GATE_COLS
    return pl.pallas_call(
        functools.partial(_gate_kernel, pieces=pieces),
        grid=(n // tn, t // tm),
        in_specs=[pl.BlockSpec((tm, d), lambda j, i: (i, 0)),
                  pl.BlockSpec((tn, d), lambda j, i: (j, 0)),
                  pl.BlockSpec((1, tn), lambda j, i: (0, j))],
        out_specs=pl.BlockSpec((tm, tn), lambda j, i: (i, j)),
        out_shape=jax.ShapeDtypeStruct((t, n), BF),
        compiler_params=_params(2),
        name="gate_proj",
    )(h2d, w_all, b_all)


def _kv_up_kernel(lat_ref, kr_ref, w_ref, k_ref, v_ref):
    lat = lat_ref[0]
    kr = kr_ref[0]
    for h in range(N_HEADS):
        lo = h * (QK_NOPE + V_DIM)
        kv = _dot(lat, w_ref[:, lo:lo + QK_NOPE + V_DIM])
        k_ref[0, h] = jnp.concatenate([kv[:, :QK_NOPE].astype(BF), kr], axis=-1)
        v_ref[0, h] = kv[:, QK_NOPE:].astype(BF)


def _kv_up_call(lat, kr, w_ukv, tm):
    b, t, _ = lat.shape
    return pl.pallas_call(
        _kv_up_kernel,
        grid=(b, t // tm),
        in_specs=[pl.BlockSpec((1, tm, KV_LORA), lambda i, j: (i, j, 0)),
                  pl.BlockSpec((1, tm, LANES), lambda i, j: (i, j, 0)),
                  pl.BlockSpec(w_ukv.shape, lambda i, j: (0, 0))],
        out_specs=[pl.BlockSpec((1, N_HEADS, tm, QK_PAD), lambda i, j: (i, 0, j, 0)),
                   pl.BlockSpec((1, N_HEADS, tm, V_DIM), lambda i, j: (i, 0, j, 0))],
        out_shape=[jax.ShapeDtypeStruct((b, N_HEADS, t, QK_PAD), BF),
                   jax.ShapeDtypeStruct((b, N_HEADS, t, V_DIM), BF)],
        compiler_params=_params(2),
        name="kv_up_proj",
    )(lat, kr, w_ukv)


Q_HEAD_COLS = 2 * LANES


def _q_up_kernel(cq_ref, w_ref, cos_ref, sin_ref, q_ref):
    cq = cq_ref[0]
    cos = cos_ref[...]
    sin = sin_ref[...]
    scale = (QK_NOPE + QK_ROPE) ** -0.5
    for h in range(N_HEADS):
        lo = h * Q_HEAD_COLS
        acc = _dot(cq, w_ref[:, lo:lo + Q_HEAD_COLS])
        qn = acc[:, :LANES]
        x = acc[:, LANES:]
        qr = x * cos + pltpu.roll(x, QK_ROPE, 1) * sin
        q_ref[0, h] = (jnp.concatenate([qn, qr], axis=-1) * scale).astype(BF)


def _q_up_call(cq, w_q, cos, sin, tm):
    b, s, _ = cq.shape
    return pl.pallas_call(
        _q_up_kernel,
        grid=(b, s // tm),
        in_specs=[pl.BlockSpec((1, tm, Q_LORA), lambda i, j: (i, j, 0)),
                  pl.BlockSpec(w_q.shape, lambda i, j: (0, 0)),
                  pl.BlockSpec((tm, LANES), lambda i, j: (j, 0)),
                  pl.BlockSpec((tm, LANES), lambda i, j: (j, 0))],
        out_specs=pl.BlockSpec((1, N_HEADS, tm, QK_PAD), lambda i, j: (i, 0, j, 0)),
        out_shape=jax.ShapeDtypeStruct((b, N_HEADS, s, QK_PAD), BF),
        compiler_params=_params(2),
        name="q_up_proj",
    )(cq, w_q, cos, sin)


ATTN_LAG = 1


def _attn_kernel(q_ref, k_ref, v_ref, kctx_ref, vctx_ref, eu_ref, ev_ref, *rest, kc, n_cast):
    cast_in = rest[:n_cast]
    o_ref, ub_ref, vt_ref = rest[n_cast:n_cast + 3]
    cast_out = rest[n_cast + 3:2 * n_cast + 3]
    s_scr, m_scr = rest[2 * n_cast + 3:]
    n = pl.program_id(0)
    tq = q_ref.shape[2]
    tx = k_ref.shape[2]
    chunks = [(k_ref, v_ref, c * kc, c * kc) for c in range(tx // kc)]
    chunks += [(kctx_ref, vctx_ref, c * kc, tx + c * kc)
               for c in range(kctx_ref.shape[2] // kc)]

    @pl.when(n == 0)
    def _():
        s_scr[...] = jnp.zeros_like(s_scr)
        m_scr[...] = jnp.zeros_like(m_scr)

    def step(cur, prev):
        q = q_ref[0, 0]
        m_prev = m_scr[prev]
        mrun = None
        lrun = jnp.zeros((tq, LANES), F32)
        acc = jnp.zeros((tq, V_DIM), F32)
        for kk_ref, vv_ref, row0, col0 in chunks:
            ks = slice(row0, row0 + kc)
            s_c = lax.dot_general(q, kk_ref[0, 0, ks, :], (((1,), (1,)), ((), ())),
                                  preferred_element_type=F32)
            s_scr[cur, :, col0:col0 + kc] = s_c
            pieces = []
            for j in range(kc // LANES):
                lanes = slice(j * LANES, (j + 1) * LANES)
                col = slice(col0 + j * LANES, col0 + (j + 1) * LANES)
                p_j = jnp.exp(s_scr[prev, :, col] - m_prev)
                lrun = lrun + p_j
                pieces.append(p_j.astype(BF))
                mrun = s_c[:, lanes] if mrun is None else jnp.maximum(mrun, s_c[:, lanes])
            acc = acc + _dot(jnp.concatenate(pieces, axis=-1), vv_ref[0, 0, ks, :])
        o_ref[0] = (acc / jnp.sum(lrun, axis=-1, keepdims=True)).astype(BF)
        m_scr[cur] = jnp.broadcast_to(jnp.max(mrun, axis=-1, keepdims=True), (tq, LANES))
        ub_ref[...] = eu_ref[...].astype(BF)
        vt_ref[...] = ev_ref[...].T.astype(BF)
        for src, dst in zip(cast_in, cast_out):
            dst[...] = src[...].astype(BF)

    @pl.when(n % 2 == 0)
    def _():
        step(0, 1)

    @pl.when(n % 2 == 1)
    def _():
        step(1, 0)


CAST_STEPS = 64


def _attn_call(q, k, v, kctx, vctx, eu, ev, weights, tq, kc):
    b, h, s, _ = q.shape
    t = k.shape[2]
    tc = kctx.shape[2]
    nq = s // tq
    total = b * h * nq
    ne, ed = eu.shape
    slab = ne // total
    assert slab * total == ne and slab % LANES == 0 and total >= CAST_STEPS
    slab_idx = lambda n: jnp.minimum(n, total - 1)
    cast_idx = lambda n: (jnp.minimum(n, CAST_STEPS - 1), 0)
    cast_specs = []
    for w in weights:
        assert w.shape[0] % (CAST_STEPS * BF16_ROWS) == 0
        cast_specs.append(pl.BlockSpec((w.shape[0] // CAST_STEPS, w.shape[1]), cast_idx))

    def block(n, lag):
        i = jnp.clip(n - lag, 0, total - 1)
        return i // (h * nq), (i // nq) % h, i % nq

    def q_map(n):
        bi, hi, qi = block(n, 0)
        return bi, hi, qi, 0

    def k_map(n):
        bi, hi, _ = block(n, 0)
        return bi, hi, 0, 0

    def v_map(n):
        bi, hi, _ = block(n, ATTN_LAG)
        return bi, hi, 0, 0

    def o_map(n):
        bi, hi, qi = block(n, ATTN_LAG)
        return bi, qi, hi

    outs = pl.pallas_call(
        functools.partial(_attn_kernel, kc=kc, n_cast=len(weights)),
        grid=(total + ATTN_LAG,),
        in_specs=[pl.BlockSpec((1, 1, tq, QK_PAD), q_map),
                  pl.BlockSpec((1, 1, t, QK_PAD), k_map),
                  pl.BlockSpec((1, 1, t, V_DIM), v_map),
                  pl.BlockSpec((1, 1, tc, QK_PAD), k_map),
                  pl.BlockSpec((1, 1, tc, V_DIM), v_map),
                  pl.BlockSpec((slab, ed), lambda n: (slab_idx(n), 0)),
                  pl.BlockSpec((slab, ed), lambda n: (slab_idx(n), 0))] + cast_specs,
        out_specs=[pl.BlockSpec((1, tq, V_DIM), o_map),
                   pl.BlockSpec((slab, ed), lambda n: (slab_idx(n), 0)),
                   pl.BlockSpec((ed, slab), lambda n: (0, slab_idx(n)))] + cast_specs,
        out_shape=[jax.ShapeDtypeStruct((b, s, h * V_DIM), BF),
                   jax.ShapeDtypeStruct((ne, ed), BF),
                   jax.ShapeDtypeStruct((ed, ne), BF)]
        + [jax.ShapeDtypeStruct(w.shape, BF) for w in weights],
        scratch_shapes=[pltpu.VMEM((2, tq, t + tc), F32), pltpu.VMEM((2, tq, LANES), F32)],
        compiler_params=_params(1),
        name="mla_attention",
    )(q, k, v, kctx, vctx, eu, ev, *weights)
    return outs[0], outs[1], outs[2], outs[3:]


def _pos_dft_kernel(c_ref, s_ref, gc_ref, gs_ref, o_ref):
    o_ref[0] = (_dot(c_ref[...], gc_ref[0]) - _dot(s_ref[...], gs_ref[0])).astype(BF)


def _pos_dft_call(cs, ss, gc, gs, tm, tn):
    b, s, n = gc.shape
    return pl.pallas_call(
        _pos_dft_kernel,
        grid=(b, n // tn, s // tm),
        in_specs=[pl.BlockSpec((tm, s), lambda i, j, m: (m, 0)),
                  pl.BlockSpec((tm, s), lambda i, j, m: (m, 0)),
                  pl.BlockSpec((1, s, tn), lambda i, j, m: (i, 0, j)),
                  pl.BlockSpec((1, s, tn), lambda i, j, m: (i, 0, j))],
        out_specs=pl.BlockSpec((1, tm, tn), lambda i, j, m: (i, m, j)),
        out_shape=jax.ShapeDtypeStruct((b, s, n), BF),
        compiler_params=_params(3),
        name="position_dft",
    )(cs, ss, gc, gs)


def _merge_kernel(a_ref, f_ref, wo_ref, wf_ref, ga_ref, gb_ref, o_ref, *, pieces):
    a = a_ref[...]
    f = f_ref[...]
    pn = wo_ref.shape[1] // pieces
    for p in range(pieces):
        cols = slice(p * pn, (p + 1) * pn)
        ya = _dot(a, wo_ref[:, cols])
        yb = _dot(f, wf_ref[:, cols])
        o_ref[:, cols] = (ga_ref[:, cols].astype(F32) * ya
                          + gb_ref[:, cols].astype(F32) * yb).astype(BF)


def _merge_call(attn, fm, w_o, w_f, gates, tm, tn, pieces):
    t, d = attn.shape
    nb = D_MODEL // tn
    return pl.pallas_call(
        functools.partial(_merge_kernel, pieces=pieces),
        grid=(nb, t // tm),
        in_specs=[pl.BlockSpec((tm, d), lambda j, i: (i, 0)),
                  pl.BlockSpec((tm, FOURIER_DIM), lambda j, i: (i, 0)),
                  pl.BlockSpec((d, tn), lambda j, i: (0, j)),
                  pl.BlockSpec((FOURIER_DIM, tn), lambda j, i: (0, j)),
                  pl.BlockSpec((tm, tn), lambda j, i: (i, j)),
                  pl.BlockSpec((tm, tn), lambda j, i: (i, j + nb))],
        out_specs=pl.BlockSpec((tm, tn), lambda j, i: (i, j)),
        out_shape=jax.ShapeDtypeStruct((t, D_MODEL), BF),
        compiler_params=_params(2),
        name="branch_merge",
    )(attn, fm, w_o, w_f, gates, gates)


def _outproj_kernel(m_ref, w_ref, x_ref, g1_ref, lg_ref, lb_ref, sh_ref, sc_ref,
                    x1_ref, h_ref, ht_ref):
    y = _dot(m_ref[...], w_ref[...])
    z = DEEPNORM_ALPHA * x_ref[0] + g1_ref[0] * y
    x1 = _layer_norm_rows(z) * lg_ref[...] + lb_ref[...]
    x1_ref[0] = x1
    h2 = _layer_norm_rows(x1) * (1.0 + sc_ref[0]) + sh_ref[0]
    h_ref[...] = h2.astype(BF)
    ht_ref[...] = h2.T.astype(BF)


def _outproj_call(merged, w_out, x, g1, ln_g, ln_b, sh2, sc2, tm):
    b, s, d = x.shape
    nb = s // tm
    bmap = lambda i, j: (i, 0, 0)
    const = lambda i, j: (0, 0)
    return pl.pallas_call(
        _outproj_kernel,
        grid=(b, nb),
        in_specs=[pl.BlockSpec((tm, d), lambda i, j: (i * nb + j, 0)),
                  pl.BlockSpec((d, d), const),
                  pl.BlockSpec((1, tm, d), lambda i, j: (i, j, 0)),
                  pl.BlockSpec((1, 1, d), bmap),
                  pl.BlockSpec((1, d), const),
                  pl.BlockSpec((1, d), const),
                  pl.BlockSpec((1, 1, d), bmap),
                  pl.BlockSpec((1, 1, d), bmap)],
        out_specs=[pl.BlockSpec((1, tm, d), lambda i, j: (i, j, 0)),
                   pl.BlockSpec((tm, d), lambda i, j: (i * nb + j, 0)),
                   pl.BlockSpec((d, tm), lambda i, j: (0, i * nb + j))],
        out_shape=[jax.ShapeDtypeStruct((b, s, d), F32),
                   jax.ShapeDtypeStruct((b * s, d), BF),
                   jax.ShapeDtypeStruct((d, b * s), BF)],
        compiler_params=_params(2),
        name="out_proj_deepnorm",
    )(merged, w_out, x, g1, ln_g, ln_b, sh2, sc2)


N_HP = 2 * PEER_HEADS
HALF_K = PEER_TOPK // 2
BF16_ROWS = 16


def _top_values(s, k):
    tops = []
    for _ in range(k):
        m = jnp.max(s, axis=0, keepdims=True)
        tops.append(m)
        s = jnp.where(s == m, -jnp.inf, s)
    return tops


def _sort_network(n):
    pairs = []
    p = 1
    while p < n:
        k = p
        while k >= 1:
            for j in range(k % p, n - k, 2 * k):
                for i in range(min(k, n - j - k)):
                    if (i + j) // (2 * p) == (i + j + k) // (2 * p):
                        pairs.append((i + j, i + j + k))
            k //= 2
        p *= 2
    return pairs


SUBLANES = 8


def _top_values_sorted(s, k):
    n = s.shape[0] // SUBLANES
    v = [s[j * SUBLANES:(j + 1) * SUBLANES, :] for j in range(n)]
    for i, j in _sort_network(n):
        v[i], v[j] = jnp.maximum(v[i], v[j]), jnp.minimum(v[i], v[j])
    tops = []
    for r in range(k):
        m = jnp.max(v[0], axis=0, keepdims=True)
        tops.append(m)
        depth = k - 1 - r
        if depth == 0:
            break
        hit = v[0] == m
        for j in range(min(depth, n - 1)):
            v[j] = jnp.where(hit, v[j + 1], v[j])
        if depth > n - 1:
            v[n - 1] = jnp.where(hit, -jnp.inf, v[n - 1])
    return tops


def _rank_among(s, tops):
    rank = jnp.full(s.shape, float(len(tops)), F32)
    for r in range(len(tops) - 1, -1, -1):
        rank = jnp.where(s >= tops[r], float(r), rank)
    return rank


def _select_kernel(wq_ref, keys_ref, h_ref, rank_ref, cnt_ref, e1_ref, e2_ref,
                   s_scr, top_scr):
    q = _dot(h_ref[...], wq_ref[...]).astype(BF)
    for hp in range(N_HP):
        s_scr[hp] = lax.dot_general(keys_ref[hp], q[:, hp * PEER_HALF:(hp + 1) * PEER_HALF],
                                    (((1,), (1,)), ((), ())), preferred_element_type=F32)

    def head_body(h, carry):
        s1 = s_scr[2 * h]
        s2 = s_scr[2 * h + 1]
        tops1 = _top_values_sorted(s1, PEER_TOPK + 1)
        tops2 = _top_values_sorted(s2, PEER_TOPK + 1)
        rank2 = _rank_among(s2, tops2)
        for r in range(PEER_TOPK):
            top_scr[0, r:r + 1, :] = tops1[r]
            top_scr[1, r:r + 1, :] = tops2[r]
        t1 = top_scr[0]
        t2 = top_scr[1]
        m1, m2 = tops1[0], tops2[0]
        cand = jnp.concatenate(
            [m1 + t2]
            + [tops1[a] + t2[:HALF_K] for a in range(1, HALF_K)]
            + [t1[HALF_K:] + m2], axis=0)
        best = _top_values_sorted(cand, PEER_TOPK + 1)
        outside = jnp.maximum(tops1[PEER_TOPK] + m2, m1 + tops2[PEER_TOPK])
        runner_up = jnp.maximum(best[PEER_TOPK], outside)
        tau = 0.5 * (best[PEER_TOPK - 1] + runner_up)
        sel = cand >= tau
        z = jnp.sum(jnp.where(sel, jnp.exp(cand - (m1 + m2)), 0.0), axis=0, keepdims=True)
        self32 = sel.astype(F32)
        counts = [jnp.sum(self32[:PEER_TOPK], axis=0, keepdims=True)]
        for a in range(1, HALF_K):
            lo = PEER_TOPK + (a - 1) * HALF_K
            counts.append(jnp.sum(self32[lo:lo + HALF_K], axis=0, keepdims=True))
        lo = PEER_TOPK + (HALF_K - 1) * HALF_K
        for a in range(HALF_K, PEER_TOPK):
            counts.append(self32[lo + a - HALF_K:lo + a - HALF_K + 1])
        cnt = jnp.zeros_like(s1)
        for a in range(PEER_TOPK):
            cnt = jnp.where(s1 == tops1[a], counts[a], cnt)
        rank_ref[h] = rank2.astype(BF)
        cnt_ref[h] = cnt
        e1_ref[h] = jnp.exp(s1 - m1) * (0.5 / z)
        e2_ref[h] = jnp.exp(s2 - m2).astype(BF)
        return carry
    lax.fori_loop(0, PEER_HEADS, head_body, 0)


def _select_call(wq, keys, h, tn):
    t, d = h.shape
    shape = (PEER_HEADS, N_KEYS, t)
    ospec = pl.BlockSpec((PEER_HEADS, N_KEYS, tn), lambda i: (0, 0, i))
    return pl.pallas_call(
        _select_kernel,
        grid=(t // tn,),
        in_specs=[pl.BlockSpec(wq.shape, lambda i: (0, 0)),
                  pl.BlockSpec(keys.shape, lambda i: (0, 0, 0)),
                  pl.BlockSpec((tn, d), lambda i: (i, 0))],
        out_specs=[ospec, ospec, ospec, ospec],
        out_shape=[jax.ShapeDtypeStruct(shape, BF), jax.ShapeDtypeStruct(shape, F32),
                   jax.ShapeDtypeStruct(shape, F32), jax.ShapeDtypeStruct(shape, BF)],
        scratch_shapes=[pltpu.VMEM((N_HP, N_KEYS, tn), F32),
                        pltpu.VMEM((2, PEER_TOPK, tn), F32)],
        compiler_params=_params(1),
        name="peer_select",
    )(wq, keys, h)


def _gelu_times_two(x):
    return x * (1.0 + lax.erf(x * math.sqrt(0.5)))


def _peer_kernel(u_ref, vt_ref, ht_ref, rank_ref, cnt_ref, e1_ref, e2_ref, o_ref,
                 a_scr, c_scr, *, rows, act_slices, mix_slices):
    e = pl.program_id(1)
    tn = ht_ref.shape[1]
    d = vt_ref.shape[0]
    half = rows // 2
    hrows = half * N_KEYS

    @pl.when(e == 0)
    def _():
        o_ref[...] = jnp.zeros_like(o_ref)

    def coef_row(r):
        w = None
        for h in range(PEER_HEADS):
            cnt = jnp.broadcast_to(cnt_ref[h, r:r + 1, :], (BF16_ROWS, tn)).astype(BF)
            e1 = jnp.broadcast_to(e1_ref[h, r:r + 1, :], (BF16_ROWS, tn)).astype(BF)
            gate = jnp.where(rank_ref[h] < cnt[None], e1[None], jnp.zeros((), BF))
            w = e2_ref[h] * gate if w is None else w + e2_ref[h] * gate
        act = _gelu_times_two(a_scr[r * N_KEYS:(r + 1) * N_KEYS, :]).astype(BF)
        c_scr[r * N_KEYS:(r + 1) * N_KEYS, :] = w.reshape(N_KEYS, tn) * act

    ht = ht_ref[...]
    a_scr[0:hrows, :] = _dot(u_ref[0:hrows, :], ht)
    srows = hrows // act_slices
    for j in range(act_slices):
        lo = hrows + j * srows
        a_scr[lo:lo + srows, :] = _dot(u_ref[lo:lo + srows, :], ht)
        for r in range(j * half // act_slices, (j + 1) * half // act_slices):
            coef_row(r)
    mrows = d // mix_slices
    c_a = c_scr[0:hrows, :]
    for j in range(mix_slices):
        o_ref[j * mrows:(j + 1) * mrows, :] += _dot(vt_ref[j * mrows:(j + 1) * mrows, 0:hrows], c_a)
        for r in range(j * half // mix_slices, (j + 1) * half // mix_slices):
            coef_row(half + r)
    o_ref[...] += _dot(vt_ref[:, hrows:], c_scr[hrows:, :])


def _peer_call(u, v_t, h_t, rank2, cnt, e1, e2, tn, te, act_slices, mix_slices):
    d, t = h_t.shape
    rows = te // N_KEYS
    groups = N_KEYS // BF16_ROWS
    rank4 = rank2.reshape(PEER_HEADS, groups, BF16_ROWS, t)
    e24 = e2.reshape(PEER_HEADS, groups, BF16_ROWS, t)
    sel3 = pl.BlockSpec((PEER_HEADS, rows, tn), lambda i, e: (0, e, i))
    sel4 = pl.BlockSpec((PEER_HEADS, groups, BF16_ROWS, tn), lambda i, e: (0, 0, 0, i))
    return pl.pallas_call(
        functools.partial(_peer_kernel, rows=rows, act_slices=act_slices,
                          mix_slices=mix_slices),
        grid=(t // tn, N_EXPERTS // te),
        in_specs=[pl.BlockSpec((te, d), lambda i, e: (e, 0)),
                  pl.BlockSpec((d, te), lambda i, e: (0, e)),
                  pl.BlockSpec((d, tn), lambda i, e: (0, i)),
                  sel4, sel3, sel3, sel4],
        out_specs=pl.BlockSpec((d, tn), lambda i, e: (0, i)),
        out_shape=jax.ShapeDtypeStruct((d, t), F32),
        scratch_shapes=[pltpu.VMEM((te, tn), F32), pltpu.VMEM((te, tn), BF)],
        compiler_params=_params(2),
        name="peer_dense",
    )(u, v_t, h_t, rank4, cnt, e1, e24)


def _final_kernel(yt_ref, x_ref, g2_ref, lg_ref, lb_ref, o_ref):
    z = DEEPNORM_ALPHA * x_ref[0] + g2_ref[0] * yt_ref[...].T
    o_ref[0] = _layer_norm_rows(z) * lg_ref[...] + lb_ref[...]


def _final_call(y_t, x1, g2, ln_g, ln_b, tm):
    b, s, d = x1.shape
    nb = s // tm
    return pl.pallas_call(
        _final_kernel,
        grid=(b, nb),
        in_specs=[pl.BlockSpec((d, tm), lambda i, j: (0, i * nb + j)),
                  pl.BlockSpec((1, tm, d), lambda i, j: (i, j, 0)),
                  pl.BlockSpec((1, 1, d), lambda i, j: (i, 0, 0)),
                  pl.BlockSpec((1, d), lambda i, j: (0, 0)),
                  pl.BlockSpec((1, d), lambda i, j: (0, 0))],
        out_specs=pl.BlockSpec((1, tm, d), lambda i, j: (i, j, 0)),
        out_shape=jax.ShapeDtypeStruct((b, s, d), F32),
        compiler_params=_params(2),
        name="final_deepnorm",
    )(y_t, x1, g2, ln_g, ln_b)


def _rope_rotation(w):
    pairs = w.reshape(w.shape[:-1] + (w.shape[-1] // 2, 2))
    return jnp.stack([-pairs[..., 1], pairs[..., 0]], axis=-1).reshape(w.shape)


def _pad_lanes(w):
    return jnp.pad(w, [(0, 0)] * (w.ndim - 1) + [(0, LANES - w.shape[-1])])


def _rope_tables(seq):
    rows = seq // GRID_W
    row = jnp.repeat(jnp.arange(rows, dtype=F32), GRID_W)
    col = jnp.tile(jnp.arange(GRID_W, dtype=F32), rows)
    half = QK_ROPE // 2
    inv = ROPE_THETA ** (-jnp.arange(0, half, 2, dtype=F32) / half)
    ang = jnp.concatenate([row[:, None] * inv, col[:, None] * inv], axis=-1)
    cos = _pad_lanes(jnp.repeat(jnp.cos(ang), 2, axis=-1))
    sin = _pad_lanes(jnp.repeat(jnp.sin(ang), 2, axis=-1))
    return cos, sin


def _dft_matrices(n, scale):
    k = np.arange(n, dtype=np.int64)
    ang = 2.0 * np.pi * ((k[:, None] * k[None, :]) % n).astype(np.float64) / n
    return np.cos(ang) * scale, np.sin(ang) * scale


def kernel(x, c, ctx, c_ctx, w_mod, b_mod, w_in, b_in, q_norm_g, w_uq, kv_norm_g, w_ukv,
           w_o_mla, w_fourier, w_out, ln1_g, ln1_b, peer_wq, peer_keys, peer_u, peer_v,
           ln2_g, ln2_b):
    B, S, D = x.shape
    T = B * S
    CT = ctx.shape[1]
    l = 0

    cmat = jnp.concatenate([c, c_ctx[None, :], jnp.zeros((8 - B - 1, D), F32)], axis=0)
    mod = _mod_call(cmat, w_mod[l], b_mod[l])
    mx = mod[:B].reshape(B, 1, 6, D)
    sh1, sc1, g1, sh2, sc2, g2 = [mx[:, :, i, :] for i in range(6)]
    mc = mod[B].reshape(1, 1, 6, D)
    sh1c, sc1c = mc[:, :, 0, :], mc[:, :, 1, :]

    wt, bi = w_in[l].T, b_in[l]
    w_kr, b_kr = wt[KV_LORA:KV_END].T, bi[KV_LORA:KV_END]
    q0 = KV_END
    f0 = KV_END + Q_LORA
    g0 = f0 + FOURIER_DIM
    w_all = jnp.concatenate(
        [wt[g0:], wt[f0:g0],
         wt[:KV_LORA], _pad_lanes(w_kr).T, _pad_lanes(_rope_rotation(w_kr)).T, wt[q0:f0]],
        axis=0).astype(BF)
    b_all = jnp.concatenate(
        [bi[g0:], bi[f0:g0],
         bi[:KV_LORA], _pad_lanes(b_kr), _pad_lanes(_rope_rotation(b_kr)), bi[q0:f0]])[None, :]
    wq3 = w_uq[l].reshape(Q_LORA, N_HEADS, QK_NOPE + QK_ROPE)
    wq_rope = wq3[:, :, QK_NOPE:]
    w_q = jnp.concatenate(
        [wq3[:, :, :QK_NOPE], wq_rope, _rope_rotation(wq_rope)],
        axis=-1).reshape(Q_LORA, N_HEADS * Q_HEAD_COLS).astype(BF)
    w_kv = w_ukv[l].astype(BF)
    gkv = kv_norm_g[l][None, :]
    gq = q_norm_g[l][None, :]

    cos, sin = _rope_tables(S)
    cos_c = _pad_lanes(jnp.ones((CT, QK_ROPE), F32))
    sin_c = jnp.zeros((CT, LANES), F32)
    dc_c, dc_s = _dft_matrices(FOURIER_GROUP_DIM, FOURIER_GROUP_DIM ** -0.5)
    dc = jnp.asarray(np.concatenate([dc_c, dc_s], axis=1), dtype=F32).astype(BF)
    ds_c, ds_s = _dft_matrices(S, S ** -0.5)
    cs = jnp.asarray(ds_c, dtype=F32).astype(BF)
    ss = jnp.asarray(ds_s, dtype=F32).astype(BF)

    hx = _ln_mod_call(x, sh1, sc1, 512)
    hc = _ln_mod_call(ctx, sh1c, sc1c, CT)
    ckv_x, kr_x, cq_x = _latent_call(hx, w_all, b_all, gkv, gq, cos, sin, 1024)
    ckv_c, kr_c, _ = _latent_call(hc, w_all, b_all, gkv, gq, cos_c, sin_c, CT)
    k_all, v_all = _kv_up_call(ckv_x, kr_x, w_kv, 1024)
    k_ctx, v_ctx = _kv_up_call(ckv_c, kr_c, w_kv, CT)
    q_all = _q_up_call(cq_x, w_q, cos, sin, 512)
    attn, u_b, v_t, (w_o_b, w_f_b, w_out_b, wq_b) = _attn_call(
        q_all, k_all, v_all, k_ctx, v_ctx, peer_u[l], peer_v[l],
        [w_o_mla[l], w_fourier[l], w_out[l], peer_wq[l]], 1024, 256)

    hx2d = hx.reshape(T, D)
    gc, gs = _fproj_call(hx2d, w_all, b_all, dc, 1024)
    fm = _pos_dft_call(cs, ss, gc.reshape(B, S, FOURIER_DIM), gs.reshape(B, S, FOURIER_DIM),
                       1024, 512)
    gates = _gate_call(hx2d, w_all, b_all, 1024, 2048, 4)
    merged = _merge_call(attn.reshape(T, D), fm.reshape(T, FOURIER_DIM),
                         w_o_b, w_f_b, gates, 512, 1024, 2)
    x1, h2, h2_t = _outproj_call(merged, w_out_b, x, g1, ln1_g[l][None, :],
                                 ln1_b[l][None, :], sh2, sc2, 512)

    keys = peer_keys[l].reshape(N_HP, N_KEYS, PEER_HALF).astype(BF)
    rank2, cnt, e1, e2 = _select_call(wq_b, keys, h2, 512)
    y_t = _peer_call(u_b, v_t, h2_t, rank2, cnt, e1, e2, 512, 1024, 2, 4)
    return _final_call(y_t, x1, g2, ln2_g[l][None, :], ln2_b[l][None, :], 512)
```
